```python
import jax, jax.numpy as jnp
from jax import lax
import numpy as np

D_MODEL = 1024
BATCH = 2
SEQ = 8192
DEPTH = 2
DEC_BATCH = 32
DEC_SEQ = 8
PAST_LEN = 8192
PAGE_SIZE = 128

N_A_LAYERS = DEPTH // 2
N_B_LAYERS = DEPTH - N_A_LAYERS
HEAD_DIM = 64
ROT_DIM = HEAD_DIM // 4
ROPE_THETA = 500000.0
NORM_EPS = 1e-6
A_GROUPS = ((128, 1), (512, 4), (2048, 16))
A_HEADS = 8
A_BLOCK = 128
B_HEADS = 16
B_KV_HEADS = 4
B_GROUP = B_HEADS // B_KV_HEADS
CMP_LEN = 32
CMP_STRIDE = 16
CMP_HIDDEN = 64
SEL_BLOCK = 64
N_SEL = 16
SEL_OVERLAP_W = (1.0, 2.0, 2.0, 2.0, 1.0)
B_WINDOW = 512
Q_BLOCK = 128
PEER_HEADS = 8
PEER_KEYS = 128
PEER_EXPERTS = PEER_KEYS * PEER_KEYS
PEER_QDIM = 256
PEER_TOPK = 16
PEER_CHUNK = 128

kernel_name = 'yoco_dilated_nsa_peer_step'


def rmsnorm(x, g):
    x32 = x.astype(jnp.float32)
    y = x32 * lax.rsqrt(jnp.mean(x32 * x32, axis=-1, keepdims=True) + NORM_EPS)
    return (y * g.astype(jnp.float32)).astype(x.dtype)


def rope(x, pos):
    half = ROT_DIM // 2
    inv = jnp.exp(-jnp.log(jnp.float32(ROPE_THETA)) * jnp.arange(half, dtype=jnp.float32) * (2.0 / ROT_DIM))
    ang = pos.astype(jnp.float32)[:, None] * inv[None, :]
    cos = jnp.cos(ang)[:, None, :].astype(x.dtype)
    sin = jnp.sin(ang)[:, None, :].astype(x.dtype)
    x1 = x[..., :half]
    x2 = x[..., half:ROT_DIM]
    return jnp.concatenate([x1 * cos - x2 * sin, x2 * cos + x1 * sin, x[..., ROT_DIM:]], axis=-1)


def masked_softmax(s, mask):
    s = jnp.where(mask, s.astype(jnp.float32), -jnp.inf)
    m = jnp.max(s, axis=-1, keepdims=True)
    m = jnp.where(jnp.isfinite(m), m, 0.0)
    e = jnp.where(mask, jnp.exp(s - m), 0.0)
    den = jnp.sum(e, axis=-1, keepdims=True)
    p = e / jnp.where(den > 0, den, 1.0)
    lse = (m + jnp.log(den))[..., 0]
    return p, lse


def dilated_attend_prompt(q, k, v, window, dil):
    B, S, H, Dh = q.shape
    nw = window // dil
    L = S // dil
    Lp = -(-L // A_BLOCK) * A_BLOCK
    nb = Lp // A_BLOCK

    def to_residue(x):
        x = x.reshape(B, L, dil, H, Dh).transpose(0, 2, 1, 3, 4)
        return jnp.pad(x, ((0, 0), (0, 0), (0, Lp - L), (0, 0), (0, 0)))

    def band_keys(x):
        prev = jnp.pad(x, ((0, 0), (0, 0), (A_BLOCK, 0), (0, 0), (0, 0)))[:, :, :Lp]
        shp = (B, dil, nb, A_BLOCK, H, Dh)
        return jnp.concatenate([prev.reshape(shp), x.reshape(shp)], axis=3)

    qb = to_residue(q).reshape(B, dil, nb, A_BLOCK, H, Dh)
    kb = band_keys(to_residue(k))
    vb = band_keys(to_residue(v))
    s = jnp.einsum('brnqhd,brnkhd->brnhqk', qb, kb) * (Dh ** -0.5)
    i = jnp.arange(A_BLOCK)[:, None]
    j = jnp.arange(2 * A_BLOCK)[None, :]
    rel = A_BLOCK + i - j
    key_idx = jnp.arange(nb)[:, None, None] * A_BLOCK - A_BLOCK + j[None]
    mask = ((rel >= 0) & (rel <= nw))[None] & (key_idx >= 0)
    p, lse = masked_softmax(s, mask[:, None])
    o = jnp.einsum('brnhqk,brnkhd->brnqhd', p.astype(v.dtype), vb)
    o = o.reshape(B, dil, Lp, H, Dh)[:, :, :L].transpose(0, 2, 1, 3, 4).reshape(B, S, H, Dh)
    lse = lse.transpose(0, 1, 2, 4, 3).reshape(B, dil, Lp, H)[:, :, :L].transpose(0, 2, 1, 3).reshape(B, S, H)
    return o, lse


def dilated_attend_step(q, k_all, v_all, window, dil, n_buf):
    Bd, Sd, H, Dh = q.shape
    nw = window // dil
    idx = n_buf + jnp.arange(Sd)[:, None] - dil * jnp.arange(nw + 1)[None, :]
    valid = idx >= 0
    idx = jnp.maximum(idx, 0)
    kg = k_all[:, idx]
    vg = v_all[:, idx]
    s = jnp.einsum('bqhd,bqjhd->bhqj', q, kg) * (Dh ** -0.5)
    p, lse = masked_softmax(s, valid)
    o = jnp.einsum('bhqj,bqjhd->bqhd', p.astype(v_all.dtype), vg)
    return o, lse.transpose(0, 2, 1)


def mixer_a(h, pos, w_in, w_o, bufs):
    B, T, _ = h.shape
    proj = (h @ w_in).reshape(B, T, len(A_GROUPS), 3, A_HEADS, HEAD_DIM)
    outs, lses, states = [], [], []
    for g, (win, dil) in enumerate(A_GROUPS):
        q = rope(proj[:, :, g, 0], pos)
        k = rope(proj[:, :, g, 1], pos)
        v = proj[:, :, g, 2]
        if bufs is None:
            o, lse = dilated_attend_prompt(q, k, v, win, dil)
            keep = min(win, T)
            states.append(jnp.stack([k[:, T - keep:], v[:, T - keep:]], axis=2))
        else:
            keep = bufs[g].shape[1]
            k_all = jnp.concatenate([bufs[g][:, :, 0], k], axis=1)
            v_all = jnp.concatenate([bufs[g][:, :, 1], v], axis=1)
            o, lse = dilated_attend_step(q, k_all, v_all, win, dil, keep)
            states.append(jnp.stack([k_all, v_all], axis=2)[:, k_all.shape[1] - keep:])
        outs.append(o)
        lses.append(lse)
    wts = jax.nn.softmax(jnp.stack(lses, axis=0), axis=0)
    o = jnp.sum(wts[..., None].astype(h.dtype) * jnp.stack(outs, axis=0), axis=0)
    return o.reshape(B, T, A_HEADS * HEAD_DIM) @ w_o, states


def compress_blocks(k_raw, v_raw, w_cmp1, w_cmp2, pe_cmp):
    B, T = k_raw.shape[:2]
    n_chunk = T // CMP_STRIDE
    kv = jnp.stack([k_raw, v_raw], axis=0)[:, :, :n_chunk * CMP_STRIDE]
    chunks = kv.reshape(2, B, n_chunk, CMP_STRIDE, B_KV_HEADS, HEAD_DIM)
    lo = jnp.einsum('zbclhd,zldf->zbchf', chunks, w_cmp1[:, :CMP_STRIDE])
    hi = jnp.einsum('zbclhd,zldf->zbchf', chunks, w_cmp1[:, CMP_STRIDE:])
    pe_term = jnp.einsum('zld,zldf->zf', pe_cmp, w_cmp1)
    hid = jax.nn.gelu(lo[:, :, :-1] + hi[:, :, 1:] + pe_term[:, None, None, None, :])
    out = jnp.einsum('zbnhf,zfd->zbnhd', hid, w_cmp2)
    end_pos = jnp.arange(n_chunk - 1, dtype=jnp.int32) * CMP_STRIDE + (CMP_LEN - 1)
    return rope(out[0], end_pos), out[1], end_pos


def to_blocks(x):
    B, T = x.shape[:2]
    nsb = -(-T // SEL_BLOCK)
    x = jnp.pad(x, ((0, 0), (0, nsb * SEL_BLOCK - T), (0, 0), (0, 0)))
    return x.reshape(B, nsb, SEL_BLOCK, B_KV_HEADS, HEAD_DIM).transpose(0, 3, 1, 2, 4)


def shared_b_context(x, pos, past, g_kv, w_kv_b, w_cmp1, w_cmp2, pe_cmp, past_rows, win_buf):
    B, T, _ = x.shape
    kv = (rmsnorm(x, g_kv) @ w_kv_b).reshape(B, T, 6, B_KV_HEADS, HEAD_DIM)
    new_rows = jnp.stack([kv[:, :, 0], kv[:, :, 1], rope(kv[:, :, 2], pos), kv[:, :, 3]], axis=2)
    new_win = jnp.stack([rope(kv[:, :, 4], pos), kv[:, :, 5]], axis=2)
    if past_rows is None:
        rows = new_rows
        keep = min(B_WINDOW, T)
        win_state = new_win[:, T - keep:]
        win_all = jnp.pad(new_win, ((0, 0), (B_WINDOW, 0), (0, 0), (0, 0), (0, 0)))
        win_pos0 = -B_WINDOW
    else:
        rows = jnp.concatenate([past_rows, new_rows], axis=1)
        keep = win_buf.shape[1]
        win_all = jnp.concatenate([win_buf, new_win], axis=1)
        win_state = win_all[:, win_all.shape[1] - keep:]
        win_pos0 = past - keep
    kc, vc, cmp_end = compress_blocks(rows[:, :, 0], rows[:, :, 1], w_cmp1, w_cmp2, pe_cmp)
    ctx = {'kc': kc, 'vc': vc, 'cmp_end': cmp_end,
           'ksb': to_blocks(rows[:, :, 2]), 'vsb': to_blocks(rows[:, :, 3]),
           'kw': win_all[:, :, 0], 'vw': win_all[:, :, 1], 'win_pos0': win_pos0}
    return ctx, new_rows, win_state


def nsa_attend(q, gate, qpos, kc, vc, cmp_end, ksb, vsb, kw, vw, wpos):
    B, Q = q.shape[:2]
    scale = HEAD_DIM ** -0.5
    qg = q.reshape(B, Q, B_KV_HEADS, B_GROUP, HEAD_DIM)
    s_c = jnp.einsum('bqhgd,bnhd->bhgqn', qg, kc) * scale
    p_c, _ = masked_softmax(s_c, cmp_end[None, :] <= qpos[:, None])
    o_c = jnp.einsum('bhgqn,bnhd->bqhgd', p_c.astype(vc.dtype), vc)
    n_cmp = kc.shape[1]
    n_sb = ksb.shape[2]
    imp = jnp.sum(p_c, axis=2)
    cidx = (SEL_BLOCK // CMP_STRIDE) * jnp.arange(n_sb)[:, None] - 1 + jnp.arange(len(SEL_OVERLAP_W))[None, :]
    wts = jnp.where((cidx >= 0) & (cidx < n_cmp), jnp.asarray(SEL_OVERLAP_W, jnp.float32), 0.0)
    imp_sel = jnp.einsum('bhqjo,jo->bhqj', imp[..., jnp.clip(cidx, 0, n_cmp - 1)], wts)
    blk = jnp.arange(n_sb)[None, :]
    cur = (qpos // SEL_BLOCK)[:, None]
    forced = (blk == 0) | (blk == cur) | (blk == cur - 1)
    imp_sel = jnp.where(blk > cur, -jnp.inf, jnp.where(forced, jnp.inf, imp_sel))
    _, sel = lax.top_k(imp_sel, min(N_SEL, n_sb))
    n_k = sel.shape[-1]
    bi = jnp.arange(B)[:, None, None, None]
    hi = jnp.arange(B_KV_HEADS)[None, :, None, None]
    kg = ksb[bi, hi, sel]
    vg = vsb[bi, hi, sel]
    tok = sel[..., None] * SEL_BLOCK + jnp.arange(SEL_BLOCK)
    s_s = jnp.einsum('bqhgd,bhqkld->bhgqkl', qg, kg) * scale
    m_s = (tok <= qpos[None, None, :, None, None])[:, :, None]
    flat = (B, B_KV_HEADS, B_GROUP, Q, n_k * SEL_BLOCK)
    p_s, _ = masked_softmax(s_s.reshape(flat), m_s.reshape(B, B_KV_HEADS, 1, Q, n_k * SEL_BLOCK))
    o_s = jnp.einsum('bhgqkl,bhqkld->bqhgd', p_s.reshape(s_s.shape).astype(vg.dtype), vg)
    s_w = jnp.einsum('bqhgd,bkhd->bhgqk', qg, kw) * scale
    rel = qpos[:, None] - wpos[None, :]
    p_w, _ = masked_softmax(s_w, (rel >= 0) & (rel <= B_WINDOW) & (wpos >= 0)[None, :])
    o_w = jnp.einsum('bhgqk,bkhd->bqhgd', p_w.astype(vw.dtype), vw)
    g = gate.reshape(B, Q, B_KV_HEADS, B_GROUP, 3).astype(q.dtype)
    o = g[..., 0:1] * o_c + g[..., 1:2] * o_s + g[..., 2:3] * o_w
    return o.reshape(B, Q, B_HEADS * HEAD_DIM)


def mixer_b(h, pos, ctx, w_qg, b_gate, w_o, is_prompt):
    B, T, _ = h.shape
    qg = h @ w_qg
    q = rope(qg[..., :B_HEADS * HEAD_DIM].reshape(B, T, B_HEADS, HEAD_DIM), pos)
    gate = jax.nn.sigmoid((qg[..., B_HEADS * HEAD_DIM:] + b_gate).astype(jnp.float32)).reshape(B, T, B_HEADS, 3)
    if is_prompt:
        nq = T // Q_BLOCK

        def block_fn(args):
            qb, gb, t0 = args
            qpos = t0 + jnp.arange(Q_BLOCK, dtype=jnp.int32)
            kw = lax.dynamic_slice_in_dim(ctx['kw'], t0, Q_BLOCK + B_WINDOW, axis=1)
            vw = lax.dynamic_slice_in_dim(ctx['vw'], t0, Q_BLOCK + B_WINDOW, axis=1)
            wpos = t0 - B_WINDOW + jnp.arange(Q_BLOCK + B_WINDOW, dtype=jnp.int32)
            return nsa_attend(qb, gb, qpos, ctx['kc'], ctx['vc'], ctx['cmp_end'], ctx['ksb'], ctx['vsb'], kw, vw, wpos)

        qs = q.reshape(B, nq, Q_BLOCK, B_HEADS, HEAD_DIM).swapaxes(0, 1)
        gs = gate.reshape(B, nq, Q_BLOCK, B_HEADS, 3).swapaxes(0, 1)
        t0s = jnp.arange(nq, dtype=jnp.int32) * Q_BLOCK
        o = lax.map(block_fn, (qs, gs, t0s)).swapaxes(0, 1).reshape(B, T, B_HEADS * HEAD_DIM)
    else:
        wpos = ctx['win_pos0'] + jnp.arange(ctx['kw'].shape[1], dtype=jnp.int32)
        o = nsa_attend(q, gate, pos, ctx['kc'], ctx['vc'], ctx['cmp_end'], ctx['ksb'], ctx['vsb'], ctx['kw'], ctx['vw'], wpos)
    return o @ w_o


def peer(h, w_q, subkeys, u, v):
    B, T, D = h.shape
    n = B * T
    n_pad = -(-n // PEER_CHUNK) * PEER_CHUNK
    flat = jnp.pad(h.reshape(n, D), ((0, n_pad - n), (0, 0)))

    def chunk_fn(xc):
        qh = (xc @ w_q).reshape(PEER_CHUNK, PEER_HEADS, 2, PEER_QDIM // 2)
        s = jnp.einsum('thzd,znd->thzn', qh, subkeys).astype(jnp.float32)
        s_top, i_top = lax.top_k(s, PEER_TOPK)
        cand_s = (s_top[:, :, 0, :, None] + s_top[:, :, 1, None, :]).reshape(PEER_CHUNK, PEER_HEADS, PEER_TOPK * PEER_TOPK)
        cand_e = (i_top[:, :, 0, :, None] * PEER_KEYS + i_top[:, :, 1, None, :]).reshape(PEER_CHUNK, PEER_HEADS, PEER_TOPK * PEER_TOPK)
        best_s, best_j = lax.top_k(cand_s, PEER_TOPK)
        e = jnp.take_along_axis(cand_e, best_j, axis=-1)
        g = jax.nn.softmax(best_s, axis=-1)
        act = jax.nn.gelu(jnp.einsum('td,thkd->thk', xc, u[e]).astype(jnp.float32))
        return jnp.einsum('thk,thkd->td', (g * act).astype(xc.dtype), v[e])

    out = lax.map(chunk_fn, flat.reshape(n_pad // PEER_CHUNK, PEER_CHUNK, D))
    return out.reshape(n_pad, D)[:n].reshape(B, T, D)


def trunk(x, past, a_bufs, past_rows, b_win_buf, is_prompt, weights):
    (g_mix, g_ffn, w_in_a, w_o_a, g_kv, w_kv_b, w_cmp1, w_cmp2, pe_cmp,
     w_qg_b, b_gate_b, w_o_b, w_peer_q, peer_subkeys, peer_u, peer_v, g_final) = weights
    B, T, _ = x.shape
    pos = past + jnp.arange(T, dtype=jnp.int32)
    a_state = [[] for _ in A_GROUPS]
    ctx, b_rows, b_win = None, None, None
    for layer in range(DEPTH):
        if layer == N_A_LAYERS:
            ctx, b_rows, b_win = shared_b_context(x, pos, past, g_kv, w_kv_b, w_cmp1, w_cmp2, pe_cmp, past_rows, b_win_buf)
        h = rmsnorm(x, g_mix[layer])
        if layer < N_A_LAYERS:
            bufs = None if a_bufs is None else [c[layer] for c in a_bufs]
            o, new_bufs = mixer_a(h, pos, w_in_a[layer], w_o_a[layer], bufs)
            for g in range(len(A_GROUPS)):
                a_state[g].append(new_bufs[g])
        else:
            lb = layer - N_A_LAYERS
            o = mixer_b(h, pos, ctx, w_qg_b[lb], b_gate_b[lb], w_o_b[lb], is_prompt)
        x = x + o
        x = x + peer(rmsnorm(x, g_ffn[layer]), w_peer_q[layer], peer_subkeys[layer], peer_u[layer], peer_v[layer])
    y = rmsnorm(x, g_final)
    return y, [jnp.stack(s, axis=0) for s in a_state], b_rows, b_win


def setup_inputs(seed: int = 0) -> dict:
    key = jax.random.key(seed)
    ks = jax.random.split(key, 32)
    f32 = jnp.float32

    def nrm(k, shape, scale):
        return jax.random.normal(k, shape, f32) * scale

    n_pages = PAST_LEN // PAGE_SIZE
    n_used = DEC_BATCH * n_pages
    n_phys = n_used + max(1, n_used // 4)
    page_table = jax.random.permutation(ks[0], n_phys)[:n_used].reshape(DEC_BATCH, n_pages).astype(jnp.int32)

    def a_shape(w):
        return (N_A_LAYERS, DEC_BATCH, min(w, PAST_LEN), 2, A_HEADS, HEAD_DIM)

    return {
        'x_prompt': nrm(ks[1], (BATCH, SEQ, D_MODEL), 1.0),
        'x_sample': nrm(ks[2], (DEC_BATCH, DEC_SEQ, D_MODEL), 1.0),
        'cache_a_g1': nrm(ks[3], a_shape(A_GROUPS[0][0]), 1.0),
        'cache_a_g2': nrm(ks[4], a_shape(A_GROUPS[1][0]), 1.0),
        'cache_a_g3': nrm(ks[5], a_shape(A_GROUPS[2][0]), 1.0),
        'cache_b_kv': nrm(ks[6], (n_phys, PAGE_SIZE, 4, B_KV_HEADS, HEAD_DIM), 1.0),
        'cache_b_win': nrm(ks[7], (DEC_BATCH, min(B_WINDOW, PAST_LEN), 2, B_KV_HEADS, HEAD_DIM), 1.0),
        'page_table': page_table,
        'g_mix': 1.0 + nrm(ks[8], (DEPTH, D_MODEL), 0.02),
        'g_ffn': 1.0 + nrm(ks[9], (DEPTH, D_MODEL), 0.02),
        'w_in_a': nrm(ks[10], (N_A_LAYERS, D_MODEL, len(A_GROUPS) * 3 * A_HEADS * HEAD_DIM), D_MODEL ** -0.5),
        'w_o_a': nrm(ks[11], (N_A_LAYERS, A_HEADS * HEAD_DIM, D_MODEL), (A_HEADS * HEAD_DIM) ** -0.5),
        'g_kv': 1.0 + nrm(ks[12], (D_MODEL,), 0.02),
        'w_kv_b': nrm(ks[13], (D_MODEL, 6 * B_KV_HEADS * HEAD_DIM), D_MODEL ** -0.5),
        'w_cmp1': nrm(ks[14], (2, CMP_LEN, HEAD_DIM, CMP_HIDDEN), (CMP_LEN * HEAD_DIM) ** -0.5),
        'w_cmp2': nrm(ks[15], (2, CMP_HIDDEN, HEAD_DIM), CMP_HIDDEN ** -0.5),
        'pe_cmp': nrm(ks[16], (2, CMP_LEN, HEAD_DIM), 0.1),
        'w_qg_b': nrm(ks[17], (N_B_LAYERS, D_MODEL, B_HEADS * HEAD_DIM + 3 * B_HEADS), D_MODEL ** -0.5),
        'b_gate_b': nrm(ks[18], (N_B_LAYERS, 3 * B_HEADS), 0.02),
        'w_o_b': nrm(ks[19], (N_B_LAYERS, B_HEADS * HEAD_DIM, D_MODEL), (B_HEADS * HEAD_DIM) ** -0.5),
        'w_peer_q': nrm(ks[20], (DEPTH, D_MODEL, PEER_HEADS * PEER_QDIM), D_MODEL ** -0.5),
        'peer_subkeys': nrm(ks[21], (DEPTH, 2, PEER_KEYS, PEER_QDIM // 2), (PEER_QDIM // 2) ** -0.5),
        'peer_u': nrm(ks[22], (DEPTH, PEER_EXPERTS, D_MODEL), D_MODEL ** -0.5),
        'peer_v': nrm(ks[23], (DEPTH, PEER_EXPERTS, D_MODEL), (PEER_HEADS * PEER_TOPK) ** -0.5),
        'g_final': 1.0 + nrm(ks[24], (D_MODEL,), 0.02),
    }


def reference(x_prompt, x_sample, cache_a_g1, cache_a_g2, cache_a_g3, cache_b_kv, cache_b_win, page_table,
              g_mix, g_ffn, w_in_a, w_o_a, g_kv, w_kv_b, w_cmp1, w_cmp2, pe_cmp,
              w_qg_b, b_gate_b, w_o_b, w_peer_q, peer_subkeys, peer_u, peer_v, g_final):
    weights = (g_mix, g_ffn, w_in_a, w_o_a, g_kv, w_kv_b, w_cmp1, w_cmp2, pe_cmp,
               w_qg_b, b_gate_b, w_o_b, w_peer_q, peer_subkeys, peer_u, peer_v, g_final)
    y_prompt, a_p, bkv_p, bwin_p = trunk(x_prompt, 0, None, None, None, True, weights)
    dec_b, n_pages = page_table.shape
    past = n_pages * cache_b_kv.shape[1]
    past_rows = cache_b_kv[page_table].reshape(dec_b, past, 4, B_KV_HEADS, HEAD_DIM)
    y_sample, a_s, bkv_s, bwin_s = trunk(x_sample, past, (cache_a_g1, cache_a_g2, cache_a_g3),
                                         past_rows, cache_b_win, False, weights)
    return (y_prompt, y_sample, a_p[0], a_s[0], a_p[1], a_s[1], a_p[2], a_s[2], bkv_p, bkv_s, bwin_p, bwin_s)
```

```python
import functools

import jax
import jax.numpy as jnp
from jax import lax
from jax.experimental import pallas as pl
from jax.experimental.pallas import tpu as pltpu

F32 = jnp.float32
BF16 = jnp.bfloat16

D_MODEL = 1024
HEAD_DIM = 64
ROT_DIM = HEAD_DIM // 4
ROPE_THETA = 500000.0
NORM_EPS = 1e-6
A_GROUPS = ((128, 1), (512, 4), (2048, 16))
A_HEADS = 8
A_BLOCK = 128
B_HEADS = 16
B_KV_HEADS = 4
B_GROUP = B_HEADS // B_KV_HEADS
CMP_LEN = 32
CMP_STRIDE = 16
SEL_BLOCK = 64
N_SEL = 16
SEL_OVERLAP_W = (1.0, 2.0, 2.0, 2.0, 1.0)
B_WINDOW = 512
Q_BLOCK = 128
PEER_HEADS = 8
PEER_KEYS = 128
PEER_QDIM = 256
PEER_TOPK = 16

NEG = -1e30
VMEM_LIMIT = 50 * 1024 * 1024
G_PITCH = 136

_NT = (((1,), (1,)), ((), ()))


def _cparams(sem):
    return pltpu.CompilerParams(dimension_semantics=sem, vmem_limit_bytes=VMEM_LIMIT)


def _split2(a):
    hi = a.astype(BF16)
    lo = (a - hi.astype(F32)).astype(BF16)
    return hi, lo


def _rms_norm_rows(x, g):
    ms = jnp.mean(x * x, axis=-1, keepdims=True)
    return x * lax.rsqrt(ms + NORM_EPS) * g


def _rms_mm_kernel(flags_ref, x_ref, g_ref, w_ref, *rest, tn, epilogue):
    if epilogue == "rope":
        cos_ref, sa_ref, sb_ref, o_ref, xn_ref = rest
    elif epilogue == "sigmoid":
        b_ref, o_ref, xn_ref = rest
    else:
        o_ref, xn_ref = rest
    j = pl.program_id(1)

    @pl.when(j == 0)
    def _():
        xn_ref[...] = _rms_norm_rows(x_ref[...], g_ref[...]).astype(BF16)

    y = jnp.dot(xn_ref[...], w_ref[...], preferred_element_type=F32)
    if epilogue == "rope":
        f = flags_ref[j]

        @pl.when(f == 0)
        def _():
            o_ref[...] = y

        @pl.when(f != 0)
        def _():
            o_ref[...] = (y * cos_ref[...] + pltpu.roll(y, 8, 1) * sb_ref[...]
                          + pltpu.roll(y, tn - 8, 1) * sa_ref[...])
    elif epilogue == "sigmoid":
        o_ref[...] = jax.nn.sigmoid(y + b_ref[...])
    else:
        o_ref[...] = y


def rms_matmul(x, g, w, *, tm, tn, epilogue="none", flags=None, tables=None, bias=None):
    M, D = x.shape
    N = w.shape[1]
    assert M % tm == 0 and N % tn == 0
    nj = N // tn
    if flags is None:
        flags = jnp.zeros((nj,), jnp.int32)
    in_specs = [
        pl.BlockSpec((tm, D), lambda i, j, f: (i, 0)),
        pl.BlockSpec((1, D), lambda i, j, f: (0, 0)),
        pl.BlockSpec((D, tn), lambda i, j, f: (0, j)),
    ]
    args = [x, g.reshape(1, D), w]
    if epilogue == "rope":
        for t in tables:
            in_specs.append(pl.BlockSpec((tm, tn), lambda i, j, f: (i, 0)))
            args.append(t)
    elif epilogue == "sigmoid":
        in_specs.append(pl.BlockSpec((1, tn), lambda i, j, f: (0, j)))
        args.append(bias.reshape(1, N))
    return pl.pallas_call(
        functools.partial(_rms_mm_kernel, tn=tn, epilogue=epilogue),
        out_shape=jax.ShapeDtypeStruct((M, N), F32),
        grid_spec=pltpu.PrefetchScalarGridSpec(
            num_scalar_prefetch=1,
            grid=(M // tm, nj),
            in_specs=in_specs,
            out_specs=pl.BlockSpec((tm, tn), lambda i, j, f: (i, j)),
            scratch_shapes=[pltpu.VMEM((tm, D), BF16)],
        ),
        compiler_params=_cparams(("parallel", "arbitrary")),
        name="rms_matmul_" + epilogue,
    )(flags, *args)


def _mm_res_kernel(a_ref, w_ref, r_ref, o_ref):
    o_ref[...] = r_ref[...] + jnp.dot(a_ref[...].astype(BF16), w_ref[...], preferred_element_type=F32)


def matmul_residual(a, w, res, *, tm):
    M, K = a.shape
    N = w.shape[1]
    return pl.pallas_call(
        _mm_res_kernel,
        out_shape=jax.ShapeDtypeStruct((M, N), F32),
        grid=(M // tm,),
        in_specs=[pl.BlockSpec((tm, K), lambda i: (i, 0)),
                  pl.BlockSpec((K, N), lambda i: (0, 0)),
                  pl.BlockSpec((tm, N), lambda i: (i, 0))],
        out_specs=pl.BlockSpec((tm, N), lambda i: (i, 0)),
        compiler_params=_cparams(("parallel",)),
        name="matmul_residual",
    )(a, w, res)


def _merge_mm_res_kernel(o1, o2, o3, l1, l2, l3, w_ref, r_ref, out_ref):
    a1, a2, a3 = l1[...], l2[...], l3[...]
    m = jnp.maximum(jnp.maximum(a1, a2), a3)
    e1, e2, e3 = jnp.exp(a1 - m), jnp.exp(a2 - m), jnp.exp(a3 - m)
    den = e1 + e2 + e3
    a = (e1 / den) * o1[...] + (e2 / den) * o2[...] + (e3 / den) * o3[...]
    out_ref[...] = r_ref[...] + jnp.dot(a.astype(BF16), w_ref[...], preferred_element_type=F32)


def merge_matmul_residual(os_, ls_, w, res, *, tm):
    M, K = os_[0].shape
    N = w.shape[1]
    row = pl.BlockSpec((tm, K), lambda i: (i, 0))
    return pl.pallas_call(
        _merge_mm_res_kernel,
        out_shape=jax.ShapeDtypeStruct((M, N), F32),
        grid=(M // tm,),
        in_specs=[row] * 6 + [pl.BlockSpec((K, N), lambda i: (0, 0)),
                              pl.BlockSpec((tm, N), lambda i: (i, 0))],
        out_specs=pl.BlockSpec((tm, N), lambda i: (i, 0)),
        compiler_params=_cparams(("parallel",)),
        name="merge_matmul_residual",
    )(*os_, *ls_, w, res)


def _rmsnorm_kernel(x_ref, g_ref, o_ref):
    o_ref[...] = _rms_norm_rows(x_ref[...], g_ref[...])


def rmsnorm_rows(x, g, *, tm):
    M, D = x.shape
    return pl.pallas_call(
        _rmsnorm_kernel,
        out_shape=jax.ShapeDtypeStruct((M, D), F32),
        grid=(M // tm,),
        in_specs=[pl.BlockSpec((tm, D), lambda i: (i, 0)), pl.BlockSpec((1, D), lambda i: (0, 0))],
        out_specs=pl.BlockSpec((tm, D), lambda i: (i, 0)),
        compiler_params=_cparams(("parallel",)),
        name="rmsnorm",
    )(x, g.reshape(1, D))


def _band_attn_kernel(q_ref, kp_ref, kc_ref, vp_ref, vc_ref, o_ref, l_ref, *, nw):
    n = pl.program_id(2)
    q = q_ref[...]
    kp, kc, vp, vc = kp_ref[...], kc_ref[...], vp_ref[...], vc_ref[...]
    i = lax.broadcasted_iota(jnp.int32, (A_BLOCK, 2 * A_BLOCK), 0)
    j = lax.broadcasted_iota(jnp.int32, (A_BLOCK, 2 * A_BLOCK), 1)
    rel = A_BLOCK + i - j
    mask = (rel >= 0) & (rel <= nw) & ((j >= A_BLOCK) | (n > 0))
    outs, lses = [], []
    for h in range(A_HEADS):
        hs = slice(h * HEAD_DIM, (h + 1) * HEAD_DIM)
        qh = q[:, hs].astype(BF16)
        kk = jnp.concatenate([kp[:, hs], kc[:, hs]], axis=0).astype(BF16)
        vv = jnp.concatenate([vp[:, hs], vc[:, hs]], axis=0).astype(BF16)
        s = lax.dot_general(qh, kk, _NT, preferred_element_type=F32) * (HEAD_DIM ** -0.5)
        s = jnp.where(mask, s, NEG)
        m = jnp.max(s, axis=-1, keepdims=True)
        e = jnp.exp(s - m)
        den = jnp.sum(e, axis=-1, keepdims=True)
        p = e / den
        outs.append(jnp.dot(p.astype(BF16), vv, preferred_element_type=F32))
        lses.append(jnp.broadcast_to(m + jnp.log(den), (A_BLOCK, HEAD_DIM)))
    o_ref[...] = jnp.concatenate(outs, axis=1)
    l_ref[...] = jnp.concatenate(lses, axis=1)


def band_attention(proj, g, window, dil, B, S):
    T, ncol = proj.shape
    L = S // dil
    nb = L // A_BLOCK
    hw = A_HEADS * HEAD_DIM
    per_res = ncol // hw
    view = proj.reshape(T // dil, dil * ncol)

    def spec(c, prev):
        def imap(b, r, n):
            nn = jnp.maximum(n - 1, 0) if prev else n
            return (b * nb + nn, r * per_res + g * 3 + c)
        return pl.BlockSpec((A_BLOCK, hw), imap)

    out_spec = pl.BlockSpec((A_BLOCK, hw), lambda b, r, n: (b * nb + n, r))
    o, l = pl.pallas_call(
        functools.partial(_band_attn_kernel, nw=window // dil),
        out_shape=[jax.ShapeDtypeStruct((B * L, dil * hw), F32)] * 2,
        grid=(B, dil, nb),
        in_specs=[spec(0, False), spec(1, True), spec(1, False), spec(2, True), spec(2, False)],
        out_specs=[out_spec, out_spec],
        compiler_params=_cparams(("parallel", "parallel", "arbitrary")),
        name=f"band_attention_d{dil}",
    )(view, view, view, view, view)
    return o.reshape(B * S, hw), l.reshape(B * S, hw)


def _topk_cols(s, k, payload=None):
    N = s.shape[0]
    iota = lax.broadcasted_iota(jnp.int32, s.shape, 0).astype(F32)
    vals, picks = [], []
    for _ in range(k):
        m = jnp.max(s, axis=0, keepdims=True)
        idx = jnp.min(jnp.where(s == m, iota, float(N)), axis=0, keepdims=True)
        sel = iota == idx
        vals.append(m)
        if payload is None:
            picks.append(idx)
        else:
            picks.append(jnp.max(jnp.where(sel, payload, -1.0), axis=0, keepdims=True))
        s = jnp.where(sel, -jnp.inf, s)
    return vals, picks


def _peer_route_kernel(x_ref, g_ref, wqh_ref, wql_ref, skh_ref, skl_ref,
                       hn_ref, e_ref, w_ref, xh_ref, xl_ref, et_ref, wt_ref):
    h = pl.program_id(1)

    @pl.when(h == 0)
    def _():
        xn = _rms_norm_rows(x_ref[...], g_ref[...])
        hi, lo = _split2(xn)
        xh_ref[...] = hi
        xl_ref[...] = lo
        hn_ref[...] = hi

    xh, xl = xh_ref[...], xl_ref[...]
    wqh, wql = wqh_ref[...], wql_ref[...]
    q = (jnp.dot(xh, wqh, preferred_element_type=F32) + jnp.dot(xh, wql, preferred_element_type=F32)
         + jnp.dot(xl, wqh, preferred_element_type=F32))
    tops = []
    for z in range(2):
        qh, ql = _split2(q[:, z * PEER_KEYS:(z + 1) * PEER_KEYS])
        skh, skl = skh_ref[z], skl_ref[z]
        st = (lax.dot_general(skh, qh, _NT, preferred_element_type=F32)
              + lax.dot_general(skl, qh, _NT, preferred_element_type=F32)
              + lax.dot_general(skh, ql, _NT, preferred_element_type=F32))
        tops.append(_topk_cols(st, PEER_TOPK))
    (v1, i1), (v2, i2) = tops
    v2m = jnp.concatenate(v2, axis=0)
    i2m = jnp.concatenate(i2, axis=0)
    cand = jnp.concatenate([v1[a] + v2m for a in range(PEER_TOPK)], axis=0)
    cand_e = jnp.concatenate([i1[a] * float(PEER_KEYS) + i2m for a in range(PEER_TOPK)], axis=0)
    best, best_e = _topk_cols(cand, PEER_TOPK, payload=cand_e)
    bs = jnp.concatenate(best, axis=0)
    be = jnp.concatenate(best_e, axis=0)
    ex = jnp.exp(bs - best[0])
    gw = ex / jnp.sum(ex, axis=0, keepdims=True)
    row = pl.multiple_of(h * PEER_TOPK, PEER_TOPK)
    et_ref[pl.ds(row, PEER_TOPK), :] = be
    wt_ref[pl.ds(row, PEER_TOPK), :] = gw

    @pl.when(h == PEER_HEADS - 1)
    def _():
        e_ref[...] = et_ref[...].T.astype(jnp.int32)
        w_ref[...] = wt_ref[...].T


def peer_route(x, g, wq_hi, wq_lo, sk_hi, sk_lo, *, tm):
    T, D = x.shape
    nslot = PEER_HEADS * PEER_TOPK
    return pl.pallas_call(
        _peer_route_kernel,
        out_shape=[jax.ShapeDtypeStruct((T, D), BF16),
                   jax.ShapeDtypeStruct((T, nslot), jnp.int32),
                   jax.ShapeDtypeStruct((T, nslot), F32)],
        grid=(T // tm, PEER_HEADS),
        in_specs=[pl.BlockSpec((tm, D), lambda i, h: (i, 0)),
                  pl.BlockSpec((1, D), lambda i, h: (0, 0)),
                  pl.BlockSpec((D, PEER_QDIM), lambda i, h: (0, h)),
                  pl.BlockSpec((D, PEER_QDIM), lambda i, h: (0, h)),
                  pl.BlockSpec((2, PEER_KEYS, PEER_QDIM // 2), lambda i, h: (0, 0, 0)),
                  pl.BlockSpec((2, PEER_KEYS, PEER_QDIM // 2), lambda i, h: (0, 0, 0))],
        out_specs=[pl.BlockSpec((tm, D), lambda i, h: (i, 0)),
                   pl.BlockSpec((tm, nslot), lambda i, h: (i, 0)),
                   pl.BlockSpec((tm, nslot), lambda i, h: (i, 0))],
        scratch_shapes=[pltpu.VMEM((tm, D), BF16), pltpu.VMEM((tm, D), BF16),
                        pltpu.VMEM((nslot, tm), F32), pltpu.VMEM((nslot, tm), F32)],
        compiler_params=_cparams(("parallel", "arbitrary")),
        name="peer_route",
    )(x, g.reshape(1, D), wq_hi, wq_lo, sk_hi, sk_lo)


def _peer_expert_kernel(hn_ref, e_ref, w_ref, x_ref, u_ref, v_ref, o_ref, gs_ref, acc_ref, *, tm, rows_per_chunk):
    c = pl.program_id(1)
    nk = PEER_KEYS

    @pl.when(c == 0)
    def _():
        acc_ref[...] = jnp.zeros_like(acc_ref)
        sub = lax.broadcasted_iota(jnp.int32, (nk, nk), 0)

        def body(t, carry):
            e = e_ref[pl.ds(t, 1), :]
            w = w_ref[pl.ds(t, 1), :]
            i1 = jnp.right_shift(e, 7)
            i2 = jnp.bitwise_and(e, nk - 1)
            pt = jnp.where(sub == i1, w, 0.0).astype(BF16)
            qt = jnp.where(sub == i2, 1.0, 0.0).astype(BF16)
            gt = lax.dot_general(pt, qt, _NT, preferred_element_type=F32)
            gs_ref[pl.ds(pl.multiple_of(t * G_PITCH, 8), nk), :] = gt
            return carry

        lax.fori_loop(0, tm, body, 0)

    a = lax.dot_general(hn_ref[...], u_ref[...], _NT, preferred_element_type=F32)
    act = jax.nn.gelu(a)
    parts = []
    for r in range(rows_per_chunk):
        gr = gs_ref[pl.ds(c * rows_per_chunk + r, tm, stride=G_PITCH), :]
        parts.append((act[:, r * nk:(r + 1) * nk] * gr).astype(BF16))
    p = jnp.concatenate(parts, axis=1)
    acc_ref[...] += jnp.dot(p, v_ref[...], preferred_element_type=F32)

    @pl.when(c == pl.num_programs(1) - 1)
    def _():
        o_ref[...] = x_ref[...] + acc_ref[...]


def peer_experts(hn, e, w, x, u, v, *, tm, chunk):
    T, D = x.shape
    n_exp = u.shape[0]
    nslot = e.shape[1]
    rows_per_chunk = chunk // PEER_KEYS
    return pl.pallas_call(
        functools.partial(_peer_expert_kernel, tm=tm, rows_per_chunk=rows_per_chunk),
        out_shape=jax.ShapeDtypeStruct((T, D), F32),
        grid=(T // tm, n_exp // chunk),
        in_specs=[pl.BlockSpec((tm, D), lambda i, c: (i, 0)),
                  pl.BlockSpec((tm, nslot), lambda i, c: (i, 0)),
                  pl.BlockSpec((tm, nslot), lambda i, c: (i, 0)),
                  pl.BlockSpec((tm, D), lambda i, c: (i, 0)),
                  pl.BlockSpec((chunk, D), lambda i, c: (c, 0)),
                  pl.BlockSpec((chunk, D), lambda i, c: (c, 0))],
        out_specs=pl.BlockSpec((tm, D), lambda i, c: (i, 0)),
        scratch_shapes=[pltpu.VMEM((tm * G_PITCH, PEER_KEYS), F32), pltpu.VMEM((tm, D), F32)],
        compiler_params=_cparams(("parallel", "arbitrary")),
        name="peer_experts",
    )(hn, e, w, x, u, v)


def peer_layer(x, g, w_q, subkeys, u, v, *, tm_route, tm_exp, chunk):
    wq_hi, wq_lo = _split2(w_q)
    sk_hi, sk_lo = _split2(subkeys)
    hn, e, w = peer_route(x, g, wq_hi, wq_lo, sk_hi, sk_lo, tm=tm_route)
    return peer_experts(hn, e, w, x, u.astype(BF16), v.astype(BF16), tm=tm_exp, chunk=chunk)


def _softmax_rows(s, valid):
    s = jnp.where(valid, s, NEG)
    m = jnp.max(s, axis=-1, keepdims=True)
    e = jnp.where(valid, jnp.exp(s - m), 0.0)
    den = jnp.sum(e, axis=-1, keepdims=True)
    return e / jnp.where(den > 0, den, 1.0)


def _nsa_prompt_kernel(q_ref, gate_ref, kc_ref, vc_ref, ks_ref, vs_ref, kw_ref, vw_ref, wsel_ref, exp_ref,
                       o_ref, *, n_cmp, kt_size):
    n = pl.program_id(2)
    t0 = n * Q_BLOCK
    scale = HEAD_DIM ** -0.5
    G, TQ = B_GROUP, Q_BLOCK
    R = G * TQ
    qf = q_ref[0]
    qs = jnp.concatenate([qf[:, g * HEAD_DIM:(g + 1) * HEAD_DIM] for g in range(G)], axis=0)
    qb = qs.astype(BF16)

    qh, ql = _split2(qs)
    kch, kcl = _split2(kc_ref[0, 0])
    s_c = (lax.dot_general(qh, kch, _NT, preferred_element_type=F32)
           + lax.dot_general(qh, kcl, _NT, preferred_element_type=F32)
           + lax.dot_general(ql, kch, _NT, preferred_element_type=F32)) * scale
    NC = s_c.shape[1]
    row = lax.broadcasted_iota(jnp.int32, (R, NC), 0)
    col = lax.broadcasted_iota(jnp.int32, (R, NC), 1)
    qpos_c = t0 + jnp.bitwise_and(row, TQ - 1)
    valid_c = (col * CMP_STRIDE + (CMP_LEN - 1) <= qpos_c) & (col < n_cmp)
    p_c = _softmax_rows(s_c, valid_c)
    o_c = jnp.dot(p_c.astype(BF16), vc_ref[0, 0].astype(BF16), preferred_element_type=F32)

    imp = p_c[0:TQ]
    for g in range(1, G):
        imp = imp + p_c[g * TQ:(g + 1) * TQ]
    i_hi = imp.astype(BF16)
    r1 = imp - i_hi.astype(F32)
    i_mid = r1.astype(BF16)
    i_lo = (r1 - i_mid.astype(F32)).astype(BF16)
    wsel = wsel_ref[...]
    imp_sel = (jnp.dot(i_hi, wsel, preferred_element_type=F32) + jnp.dot(i_mid, wsel, preferred_element_type=F32)
               + jnp.dot(i_lo, wsel, preferred_element_type=F32))
    NSB = imp_sel.shape[1]
    blk = lax.broadcasted_iota(jnp.int32, (TQ, NSB), 1)
    cur = jnp.right_shift(t0 + lax.broadcasted_iota(jnp.int32, (TQ, NSB), 0), 6)
    forced = (blk == 0) | (blk == cur) | (blk == cur - 1)
    imp_sel = jnp.where(blk > cur, -jnp.inf, jnp.where(forced, jnp.inf, imp_sel))
    st = imp_sel.T
    iota_b = lax.broadcasted_iota(jnp.int32, st.shape, 0).astype(F32)
    chosen = jnp.zeros(st.shape, F32)
    for _ in range(min(N_SEL, NSB)):
        m = jnp.max(st, axis=0, keepdims=True)
        idx = jnp.min(jnp.where(st == m, iota_b, float(NSB)), axis=0, keepdims=True)
        sel = iota_b == idx
        chosen = jnp.where(sel, 1.0, chosen)
        st = jnp.where(sel, -jnp.inf, st)
    chosen_q = chosen.T.astype(BF16)

    qpos_k = t0 + lax.broadcasted_iota(jnp.int32, (TQ, kt_size), 0)
    kcol = lax.broadcasted_iota(jnp.int32, (TQ, kt_size), 1)

    def sel_body(kt, carry):
        m_i, l_i, acc = carry
        k0 = pl.multiple_of(kt * kt_size, kt_size)
        k = ks_ref[0, 0, pl.ds(k0, kt_size), :]
        v = vs_ref[0, 0, pl.ds(k0, kt_size), :]
        s = lax.dot_general(qb, k, _NT, preferred_element_type=F32) * scale
        mk = jnp.dot(chosen_q, exp_ref[:, pl.ds(k0, kt_size)], preferred_element_type=F32) > 0.5
        mk = mk & (k0 + kcol <= qpos_k)
        mk4 = jnp.concatenate([mk] * G, axis=0)
        s = jnp.where(mk4, s, NEG)
        m_new = jnp.maximum(m_i, jnp.max(s, axis=-1, keepdims=True))
        alpha = jnp.exp(m_i - m_new)
        p = jnp.where(mk4, jnp.exp(s - m_new), 0.0)
        l_new = alpha * l_i + jnp.sum(p, axis=-1, keepdims=True)
        acc = alpha * acc + jnp.dot(p.astype(BF16), v, preferred_element_type=F32)
        return m_new, l_new, acc

    n_kt = (t0 + TQ + kt_size - 1) // kt_size
    init = (jnp.full((R, 1), NEG, F32), jnp.zeros((R, 1), F32), jnp.zeros((R, HEAD_DIM), F32))
    _, l_s, acc_s = lax.fori_loop(0, n_kt, sel_body, init)
    o_s = acc_s / jnp.where(l_s > 0, l_s, 1.0)

    WK = B_WINDOW + TQ
    w0 = pl.multiple_of(t0, TQ)
    kw = kw_ref[0, 0, pl.ds(w0, WK), :]
    vw = vw_ref[0, 0, pl.ds(w0, WK), :]
    s_w = lax.dot_general(qb, kw, _NT, preferred_element_type=F32) * scale
    wrow = lax.broadcasted_iota(jnp.int32, (R, WK), 0)
    wpos = t0 - B_WINDOW + lax.broadcasted_iota(jnp.int32, (R, WK), 1)
    rel = t0 + jnp.bitwise_and(wrow, TQ - 1) - wpos
    p_w = _softmax_rows(s_w, (rel >= 0) & (rel <= B_WINDOW) & (wpos >= 0))
    o_w = jnp.dot(p_w.astype(BF16), vw, preferred_element_type=F32)

    gt = gate_ref[0, 0]
    outs = []
    for g in range(G):
        rs = slice(g * TQ, (g + 1) * TQ)
        outs.append(gt[:, 3 * g:3 * g + 1] * o_c[rs] + gt[:, 3 * g + 1:3 * g + 2] * o_s[rs]
                    + gt[:, 3 * g + 2:3 * g + 3] * o_w[rs])
    o_ref[0] = jnp.concatenate(outs, axis=1)


def nsa_prompt(q, gate, kc, vc, ks, vs, kw, vw, wsel, expand, *, n_cmp, kt_size):
    B, S, _ = q.shape
    NC = kc.shape[2]
    gw = B_GROUP * HEAD_DIM
    kvmap = lambda b, h, n: (b, h, 0, 0)
    return pl.pallas_call(
        functools.partial(_nsa_prompt_kernel, n_cmp=n_cmp, kt_size=kt_size),
        out_shape=jax.ShapeDtypeStruct((B, S, B_HEADS * HEAD_DIM), F32),
        grid=(B, B_KV_HEADS, S // Q_BLOCK),
        in_specs=[pl.BlockSpec((1, Q_BLOCK, gw), lambda b, h, n: (b, n, h)),
                  pl.BlockSpec((1, 1, Q_BLOCK, 3 * B_GROUP), lambda b, h, n: (b, h, n, 0)),
                  pl.BlockSpec((1, 1, NC, HEAD_DIM), kvmap),
                  pl.BlockSpec((1, 1, NC, HEAD_DIM), kvmap),
                  pl.BlockSpec((1, 1, S, HEAD_DIM), kvmap),
                  pl.BlockSpec((1, 1, S, HEAD_DIM), kvmap),
                  pl.BlockSpec((1, 1, S + B_WINDOW, HEAD_DIM), kvmap),
                  pl.BlockSpec((1, 1, S + B_WINDOW, HEAD_DIM), kvmap),
                  pl.BlockSpec(wsel.shape, lambda b, h, n: (0, 0)),
                  pl.BlockSpec(expand.shape, lambda b, h, n: (0, 0))],
        out_specs=pl.BlockSpec((1, Q_BLOCK, gw), lambda b, h, n: (b, n, h)),
        compiler_params=_cparams(("parallel", "parallel", "arbitrary")),
        name="nsa_prompt",
    )(q, gate, kc, vc, ks, vs, kw, vw, wsel, expand)


def _masked_softmax(s, mask):
    s = jnp.where(mask, s.astype(F32), -jnp.inf)
    m = jnp.max(s, axis=-1, keepdims=True)
    m = jnp.where(jnp.isfinite(m), m, 0.0)
    e = jnp.where(mask, jnp.exp(s - m), 0.0)
    den = jnp.sum(e, axis=-1, keepdims=True)
    p = e / jnp.where(den > 0, den, 1.0)
    return p, (m + jnp.log(den))[..., 0]


def _rope_tables(pos, width):
    half = ROT_DIM // 2
    inv = jnp.exp(-jnp.log(jnp.float32(ROPE_THETA)) * jnp.arange(half, dtype=F32) * (2.0 / ROT_DIM))
    ang = pos.astype(F32)[:, None] * inv[None, :]
    cos, sin = jnp.cos(ang), jnp.sin(ang)
    T = pos.shape[0]
    z8 = jnp.zeros((T, half), F32)
    rest = HEAD_DIM - ROT_DIM
    c = jnp.concatenate([cos, cos, jnp.ones((T, rest), F32)], axis=1)
    sa = jnp.concatenate([-sin, z8, jnp.zeros((T, rest), F32)], axis=1)
    sb = jnp.concatenate([z8, sin, jnp.zeros((T, rest), F32)], axis=1)
    reps = width // HEAD_DIM
    return tuple(jnp.tile(t, (1, reps)) for t in (c, sa, sb))


def _rope_apply(x, pos):
    half = ROT_DIM // 2
    inv = jnp.exp(-jnp.log(jnp.float32(ROPE_THETA)) * jnp.arange(half, dtype=F32) * (2.0 / ROT_DIM))
    ang = pos.astype(F32)[:, None] * inv[None, :]
    cos = jnp.cos(ang)[:, None, :]
    sin = jnp.sin(ang)[:, None, :]
    x1, x2 = x[..., :half], x[..., half:ROT_DIM]
    return jnp.concatenate([x1 * cos - x2 * sin, x2 * cos + x1 * sin, x[..., ROT_DIM:]], axis=-1)


def _dilated_step(q, k_all, v_all, window, dil, n_buf):
    Bd, Sd, H, Dh = q.shape
    nw = window // dil
    idx = n_buf + jnp.arange(Sd)[:, None] - dil * jnp.arange(nw + 1)[None, :]
    valid = idx >= 0
    idx = jnp.maximum(idx, 0)
    kg, vg = k_all[:, idx], v_all[:, idx]
    s = jnp.einsum('bqhd,bqjhd->bhqj', q, kg) * (Dh ** -0.5)
    p, lse = _masked_softmax(s, valid)
    o = jnp.einsum('bhqj,bqjhd->bqhd', p, vg)
    return o, lse.transpose(0, 2, 1)


def _compress(k_raw, v_raw, w_cmp1, w_cmp2, pe_cmp):
    B, T = k_raw.shape[:2]
    n_chunk = T // CMP_STRIDE
    kv = jnp.stack([k_raw, v_raw], axis=0)[:, :, :n_chunk * CMP_STRIDE]
    chunks = kv.reshape(2, B, n_chunk, CMP_STRIDE, B_KV_HEADS, HEAD_DIM)
    lo = jnp.einsum('zbclhd,zldf->zbchf', chunks, w_cmp1[:, :CMP_STRIDE])
    hi = jnp.einsum('zbclhd,zldf->zbchf', chunks, w_cmp1[:, CMP_STRIDE:])
    pe_term = jnp.einsum('zld,zldf->zf', pe_cmp, w_cmp1)
    hid = jax.nn.gelu(lo[:, :, :-1] + hi[:, :, 1:] + pe_term[:, None, None, None, :])
    out = jnp.einsum('zbnhf,zfd->zbnhd', hid, w_cmp2)
    end_pos = jnp.arange(n_chunk - 1, dtype=jnp.int32) * CMP_STRIDE + (CMP_LEN - 1)
    return _rope_apply(out[0], end_pos), out[1], end_pos


def _to_blocks(x):
    B, T = x.shape[:2]
    nsb = -(-T // SEL_BLOCK)
    x = jnp.pad(x, ((0, 0), (0, nsb * SEL_BLOCK - T), (0, 0), (0, 0)))
    return x.reshape(B, nsb, SEL_BLOCK, B_KV_HEADS, HEAD_DIM).transpose(0, 3, 1, 2, 4)


def _nsa_step(q, gate, qpos, kc, vc, cmp_end, ksb, vsb, kw, vw, wpos):
    B, Q = q.shape[:2]
    scale = HEAD_DIM ** -0.5
    qg = q.reshape(B, Q, B_KV_HEADS, B_GROUP, HEAD_DIM)
    s_c = jnp.einsum('bqhgd,bnhd->bhgqn', qg, kc) * scale
    p_c, _ = _masked_softmax(s_c, cmp_end[None, :] <= qpos[:, None])
    o_c = jnp.einsum('bhgqn,bnhd->bqhgd', p_c, vc)
    n_cmp, n_sb = kc.shape[1], ksb.shape[2]
    imp = jnp.sum(p_c, axis=2)
    cidx = (SEL_BLOCK // CMP_STRIDE) * jnp.arange(n_sb)[:, None] - 1 + jnp.arange(len(SEL_OVERLAP_W))[None, :]
    wts = jnp.where((cidx >= 0) & (cidx < n_cmp), jnp.asarray(SEL_OVERLAP_W, F32), 0.0)
    imp_sel = jnp.einsum('bhqjo,jo->bhqj', imp[..., jnp.clip(cidx, 0, n_cmp - 1)], wts)
    blk = jnp.arange(n_sb)[None, :]
    cur = (qpos // SEL_BLOCK)[:, None]
    forced = (blk == 0) | (blk == cur) | (blk == cur - 1)
    imp_sel = jnp.where(blk > cur, -jnp.inf, jnp.where(forced, jnp.inf, imp_sel))
    _, sel = lax.top_k(imp_sel, min(N_SEL, n_sb))
    n_k = sel.shape[-1]
    bi = jnp.arange(B)[:, None, None, None]
    hi = jnp.arange(B_KV_HEADS)[None, :, None, None]
    kg, vg = ksb[bi, hi, sel], vsb[bi, hi, sel]
    tok = sel[..., None] * SEL_BLOCK + jnp.arange(SEL_BLOCK)
    s_s = jnp.einsum('bqhgd,bhqkld->bhgqkl', qg, kg) * scale
    m_s = (tok <= qpos[None, None, :, None, None])[:, :, None]
    flat = (B, B_KV_HEADS, B_GROUP, Q, n_k * SEL_BLOCK)
    p_s, _ = _masked_softmax(s_s.reshape(flat), m_s.reshape(B, B_KV_HEADS, 1, Q, n_k * SEL_BLOCK))
    o_s = jnp.einsum('bhgqkl,bhqkld->bqhgd', p_s.reshape(s_s.shape), vg)
    s_w = jnp.einsum('bqhgd,bkhd->bhgqk', qg, kw) * scale
    rel = qpos[:, None] - wpos[None, :]
    p_w, _ = _masked_softmax(s_w, (rel >= 0) & (rel <= B_WINDOW) & (wpos >= 0)[None, :])
    o_w = jnp.einsum('bhgqk,bkhd->bqhgd', p_w, vw)
    g = gate.reshape(B, Q, B_KV_HEADS, B_GROUP, 3)
    o = g[..., 0:1] * o_c + g[..., 1:2] * o_s + g[..., 2:3] * o_w
    return o.reshape(B, Q, B_HEADS * HEAD_DIM)


def _selection_constants(n_cmp, n_cmp_pad, n_sb, n_keys):
    j = jnp.arange(n_sb)
    cidx = (SEL_BLOCK // CMP_STRIDE) * j[:, None] - 1 + jnp.arange(len(SEL_OVERLAP_W))[None, :]
    wts = jnp.where((cidx >= 0) & (cidx < n_cmp), jnp.asarray(SEL_OVERLAP_W, F32), 0.0)
    onehot = (jnp.clip(cidx, 0, n_cmp - 1)[:, :, None] == jnp.arange(n_cmp_pad)[None, None, :]).astype(F32)
    wsel = jnp.einsum('jo,jon->nj', wts, onehot).astype(BF16)
    expand = (jnp.arange(n_keys)[None, :] // SEL_BLOCK == j[:, None]).astype(BF16)
    return wsel, expand


def _pick(n, cands):
    for c in cands:
        if n % c == 0:
            return c
    return n


def kernel(x_prompt, x_sample, cache_a_g1, cache_a_g2, cache_a_g3, cache_b_kv, cache_b_win, page_table, g_mix, g_ffn, w_in_a, w_o_a, g_kv, w_kv_b, w_cmp1, w_cmp2, pe_cmp, w_qg_b, b_gate_b, w_o_b, w_peer_q, peer_subkeys, peer_u, peer_v, g_final):
    B, S, D = x_prompt.shape
    Bd, Sd, _ = x_sample.shape
    Tp, Ts = B * S, Bd * Sd
    T = Tp + Ts
    n_pages = page_table.shape[1]
    past = n_pages * cache_b_kv.shape[1]
    hw = A_HEADS * HEAD_DIM
    tm_big = _pick(T, (1280, 640, 256, 128, 64, 32, 16, 8))
    tm_mid = _pick(T, (640, 256, 128, 64, 32, 16, 8))
    tm_peer = _pick(T, (256, 128, 64, 32, 16, 8))

    x = jnp.concatenate([x_prompt.reshape(Tp, D), x_sample.reshape(Ts, D)], axis=0)
    pos_p = jnp.arange(S, dtype=jnp.int32)
    pos_s = past + jnp.arange(Sd, dtype=jnp.int32)
    pos = jnp.concatenate([jnp.tile(pos_p, B), jnp.tile(pos_s, Bd)])
    tab512 = _rope_tables(pos, 512)
    tab256 = tuple(t[:, :256] for t in tab512)

    n_grp = len(A_GROUPS)
    proj = rms_matmul(x, g_mix[0], w_in_a[0].astype(BF16), tm=tm_big, tn=hw, epilogue="rope",
                      flags=jnp.asarray([1, 1, 0] * n_grp, jnp.int32), tables=tab512)
    proj_p, proj_s = proj[:Tp], proj[Tp:].reshape(Bd, Sd, n_grp, 3, A_HEADS, HEAD_DIM)
    caches_a = (cache_a_g1, cache_a_g2, cache_a_g3)
    o_list, l_list, a_prompt, a_sample = [], [], [], []
    for g, (win, dil) in enumerate(A_GROUPS):
        o_p, l_p = band_attention(proj, g, win, dil, B, S)
        keep = min(win, S)
        kv_cols = proj_p.reshape(B, S, n_grp, 3 * hw)[:, S - keep:, g, hw:]
        a_prompt.append(kv_cols.reshape(1, B, keep, 2, A_HEADS, HEAD_DIM))
        buf = caches_a[g][0]
        nbuf = buf.shape[1]
        k_all = jnp.concatenate([buf[:, :, 0], proj_s[:, :, g, 1]], axis=1)
        v_all = jnp.concatenate([buf[:, :, 1], proj_s[:, :, g, 2]], axis=1)
        o_s, lse_s = _dilated_step(proj_s[:, :, g, 0], k_all, v_all, win, dil, nbuf)
        a_sample.append(jnp.stack([k_all, v_all], axis=2)[None, :, Sd:])
        l_s = jnp.broadcast_to(lse_s[..., None], (Bd, Sd, A_HEADS, HEAD_DIM))
        o_list.append(jnp.concatenate([o_p, o_s.reshape(Ts, hw)], axis=0))
        l_list.append(jnp.concatenate([l_p, l_s.reshape(Ts, hw)], axis=0))
    x = merge_matmul_residual(o_list, l_list, w_o_a[0].astype(BF16), x, tm=tm_mid)
    x = peer_layer(x, g_ffn[0], w_peer_q[0], peer_subkeys[0], peer_u[0], peer_v[0],
                   tm_route=tm_peer, tm_exp=tm_peer, chunk=2048)

    kvw = B_KV_HEADS * HEAD_DIM
    kv = rms_matmul(x, g_kv, w_kv_b.astype(BF16), tm=tm_big, tn=kvw, epilogue="rope",
                    flags=jnp.asarray([0, 0, 1, 0, 1, 0], jnp.int32), tables=tab256)
    kv_p = kv[:Tp].reshape(B, S, 6, B_KV_HEADS, HEAD_DIM)
    kv_s = kv[Tp:].reshape(Bd, Sd, 6, B_KV_HEADS, HEAD_DIM)
    bkv_prompt, bkv_sample = kv_p[:, :, :4], kv_s[:, :, :4]
    keep_w = min(B_WINDOW, S)
    bwin_prompt = kv_p[:, S - keep_w:, 4:]
    win_all_s = jnp.concatenate([cache_b_win, kv_s[:, :, 4:]], axis=1)
    bwin_sample = win_all_s[:, Sd:]

    kc_p, vc_p, _ = _compress(kv_p[:, :, 0], kv_p[:, :, 1], w_cmp1, w_cmp2, pe_cmp)
    past_rows = cache_b_kv[page_table].reshape(Bd, past, 4, B_KV_HEADS, HEAD_DIM)
    rows_s = jnp.concatenate([past_rows, bkv_sample], axis=1)
    kc_s, vc_s, cmp_end_s = _compress(rows_s[:, :, 0], rows_s[:, :, 1], w_cmp1, w_cmp2, pe_cmp)

    nq = B_HEADS * HEAD_DIM
    w_qg = w_qg_b[0]
    q = rms_matmul(x, g_mix[1], w_qg[:, :nq].astype(BF16), tm=tm_big, tn=512, epilogue="rope",
                   flags=jnp.ones((nq // 512,), jnp.int32), tables=tab512)
    ngate = 3 * B_HEADS
    w_gate = jnp.pad(w_qg[:, nq:], ((0, 0), (0, 128 - ngate))).astype(BF16)
    b_gate = jnp.pad(b_gate_b[0], (0, 128 - ngate))
    gate = rms_matmul(x, g_mix[1], w_gate, tm=tm_big, tn=128, epilogue="sigmoid", bias=b_gate)[:, :ngate]

    n_cmp = S // CMP_STRIDE - 1
    n_cmp_pad = -(-n_cmp // 128) * 128
    n_sb = S // SEL_BLOCK
    wsel, expand = _selection_constants(n_cmp, n_cmp_pad, n_sb, S)
    padc = ((0, 0), (0, n_cmp_pad - n_cmp), (0, 0), (0, 0))
    kc_in = jnp.pad(kc_p, padc).transpose(0, 2, 1, 3)
    vc_in = jnp.pad(vc_p, padc).transpose(0, 2, 1, 3)
    heads_first = lambda a: a.transpose(0, 2, 1, 3).astype(BF16)
    ks_in, vs_in = heads_first(kv_p[:, :, 2]), heads_first(kv_p[:, :, 3])
    padw = ((0, 0), (0, 0), (B_WINDOW, 0), (0, 0))
    kw_in, vw_in = jnp.pad(heads_first(kv_p[:, :, 4]), padw), jnp.pad(heads_first(kv_p[:, :, 5]), padw)
    gate_p = gate[:Tp].reshape(B, S, B_KV_HEADS, 3 * B_GROUP).transpose(0, 2, 1, 3)
    o_b_p = nsa_prompt(q[:Tp].reshape(B, S, nq), gate_p, kc_in, vc_in, ks_in, vs_in, kw_in, vw_in, wsel, expand,
                       n_cmp=n_cmp, kt_size=min(512, S))
    wpos_s = (past - cache_b_win.shape[1]) + jnp.arange(win_all_s.shape[1], dtype=jnp.int32)
    o_b_s = _nsa_step(q[Tp:].reshape(Bd, Sd, B_HEADS, HEAD_DIM), gate[Tp:].reshape(Bd, Sd, B_HEADS, 3), pos_s,
                      kc_s, vc_s, cmp_end_s, _to_blocks(rows_s[:, :, 2]), _to_blocks(rows_s[:, :, 3]),
                      win_all_s[:, :, 0], win_all_s[:, :, 1], wpos_s)
    o_b = jnp.concatenate([o_b_p.reshape(Tp, nq), o_b_s.reshape(Ts, nq)], axis=0)
    x = matmul_residual(o_b, w_o_b[0].astype(BF16), x, tm=tm_mid)
    x = peer_layer(x, g_ffn[1], w_peer_q[1], peer_subkeys[1], peer_u[1], peer_v[1],
                   tm_route=tm_peer, tm_exp=tm_peer, chunk=2048)
    y = rmsnorm_rows(x, g_final, tm=tm_mid)

    return (y[:Tp].reshape(B, S, D), y[Tp:].reshape(Bd, Sd, D),
            a_prompt[0], a_sample[0], a_prompt[1], a_sample[1], a_prompt[2], a_sample[2],
            bkv_prompt, bkv_sample, bwin_prompt, bwin_sample)
```

```python
import functools

import jax
import jax.numpy as jnp
from jax import lax
from jax.experimental import pallas as pl
from jax.experimental.pallas import tpu as pltpu

F32 = jnp.float32
BF16 = jnp.bfloat16

HEAD_DIM = 64
ROT_DIM = HEAD_DIM // 4
ROPE_THETA = 500000.0
NORM_EPS = 1e-6
A_GROUPS = ((128, 1), (512, 4), (2048, 16))
A_HEADS = 8
A_BLOCK = 128
B_HEADS = 16
B_KV_HEADS = 4
B_GROUP = B_HEADS // B_KV_HEADS
CMP_LEN = 32
CMP_STRIDE = 16
SEL_BLOCK = 64
N_SEL = 16
SEL_OVERLAP_W = (1.0, 2.0, 2.0, 2.0, 1.0)
B_WINDOW = 512
Q_BLOCK = 128
PEER_HEADS = 8
PEER_KEYS = 128
PEER_QDIM = 256
PEER_TOPK = 16

LANES = 128
NEG = -1e30
VMEM_LIMIT = 50 * 1024 * 1024
G_PITCH = 136
KV_W = B_KV_HEADS * HEAD_DIM
SCALE = HEAD_DIM ** -0.5

_NT = (((1,), (1,)), ((), ()))


def _cparams(sem):
    return pltpu.CompilerParams(dimension_semantics=sem, vmem_limit_bytes=VMEM_LIMIT)


def _split2(a):
    hi = a.astype(BF16)
    lo = (a - hi.astype(F32)).astype(BF16)
    return hi, lo


def _dot(a, b):
    return jnp.dot(a, b, preferred_element_type=F32)


def _dot_nt(a, b):
    return lax.dot_general(a, b, _NT, preferred_element_type=F32)


def _dot_nt3(a, b):
    ah, al = _split2(a)
    bh, bl = _split2(b)
    return _dot_nt(ah, bh) + _dot_nt(ah, bl) + _dot_nt(al, bh)


def _softmax_rows(s, valid):
    s = jnp.where(valid, s, NEG)
    m = jnp.max(s, axis=-1, keepdims=True)
    e = jnp.where(valid, jnp.exp(s - m), 0.0)
    den = jnp.sum(e, axis=-1, keepdims=True)
    return e / jnp.where(den > 0, den, 1.0)


def _rope_lanes(y, cos, sa, sb):
    n = y.shape[-1]
    return y * cos + pltpu.roll(y, 8, 1) * sb + pltpu.roll(y, n - 8, 1) * sa


def _rms_norm_rows(x, g):
    ms = jnp.mean(x * x, axis=-1, keepdims=True)
    return x * lax.rsqrt(ms + NORM_EPS) * g


def _rms_mm_kernel(flags_ref, x_ref, g_ref, w_ref, *rest, epilogue, emit_bf16):
    n_out = 2 if emit_bf16 else 1
    extra, outs, xn_ref = rest[:len(rest) - n_out - 1], rest[len(rest) - n_out - 1:-1], rest[-1]
    j = pl.program_id(1)

    @pl.when(j == 0)
    def _():
        xn_ref[...] = _rms_norm_rows(x_ref[...], g_ref[...]).astype(BF16)

    def store(val):
        outs[0][...] = val
        if emit_bf16:
            outs[1][...] = val.astype(BF16)

    y = _dot(xn_ref[...], w_ref[...])
    if epilogue == "rope":
        cos_ref, sa_ref, sb_ref = extra
        f = flags_ref[j]

        @pl.when(f == 0)
        def _():
            store(y)

        @pl.when(f != 0)
        def _():
            store(_rope_lanes(y, cos_ref[...], sa_ref[...], sb_ref[...]))
    elif epilogue == "sigmoid":
        store(jax.nn.sigmoid(y + extra[0][...]))
    else:
        store(y)


def rms_matmul(x, g, w, *, tm, tn, epilogue="none", flags=None, tables=None, bias=None, emit_bf16=False):
    M, D = x.shape
    N = w.shape[1]
    assert M % tm == 0 and N % tn == 0
    nj = N // tn
    if flags is None:
        flags = jnp.zeros((nj,), jnp.int32)
    in_specs = [
        pl.BlockSpec((tm, D), lambda i, j, f: (i, 0)),
        pl.BlockSpec((1, D), lambda i, j, f: (0, 0)),
        pl.BlockSpec((D, tn), lambda i, j, f: (0, j)),
    ]
    args = [x, g.reshape(1, D), w]
    if epilogue == "rope":
        for t in tables:
            in_specs.append(pl.BlockSpec((tm, tn), lambda i, j, f: (i, 0)))
            args.append(t)
    elif epilogue == "sigmoid":
        in_specs.append(pl.BlockSpec((1, tn), lambda i, j, f: (0, j)))
        args.append(bias.reshape(1, N))
    out_spec = pl.BlockSpec((tm, tn), lambda i, j, f: (i, j))
    out_shape = [jax.ShapeDtypeStruct((M, N), F32)]
    if emit_bf16:
        out_shape.append(jax.ShapeDtypeStruct((M, N), BF16))
    res = pl.pallas_call(
        functools.partial(_rms_mm_kernel, epilogue=epilogue, emit_bf16=emit_bf16),
        out_shape=out_shape,
        grid_spec=pltpu.PrefetchScalarGridSpec(
            num_scalar_prefetch=1,
            grid=(M // tm, nj),
            in_specs=in_specs,
            out_specs=[out_spec] * len(out_shape),
            scratch_shapes=[pltpu.VMEM((tm, D), BF16)],
        ),
        compiler_params=_cparams(("parallel", "arbitrary")),
        name="rms_matmul_" + epilogue,
    )(flags, *args)
    return res if emit_bf16 else res[0]


def _mm_res_kernel(a_ref, w_ref, r_ref, o_ref):
    o_ref[...] = r_ref[...] + _dot(a_ref[...].astype(BF16), w_ref[...])


def matmul_residual(a, w, res, *, tm):
    M, K = a.shape
    N = w.shape[1]
    return pl.pallas_call(
        _mm_res_kernel,
        out_shape=jax.ShapeDtypeStruct((M, N), F32),
        grid=(M // tm,),
        in_specs=[pl.BlockSpec((tm, K), lambda i: (i, 0)),
                  pl.BlockSpec((K, N), lambda i: (0, 0)),
                  pl.BlockSpec((tm, N), lambda i: (i, 0))],
        out_specs=pl.BlockSpec((tm, N), lambda i: (i, 0)),
        compiler_params=_cparams(("parallel",)),
        name="matmul_residual",
    )(a, w, res)


def _merge_mm_res_kernel(o1, o2, o3, l1, l2, l3, w_ref, r_ref, out_ref):
    a1, a2, a3 = l1[...], l2[...], l3[...]
    m = jnp.maximum(jnp.maximum(a1, a2), a3)
    e1, e2, e3 = jnp.exp(a1 - m), jnp.exp(a2 - m), jnp.exp(a3 - m)
    den = e1 + e2 + e3
    a = (e1 / den) * o1[...] + (e2 / den) * o2[...] + (e3 / den) * o3[...]
    out_ref[...] = r_ref[...] + _dot(a.astype(BF16), w_ref[...])


def merge_matmul_residual(os_, ls_, w, res, *, tm):
    M, K = os_[0].shape
    N = w.shape[1]
    row = pl.BlockSpec((tm, K), lambda i: (i, 0))
    return pl.pallas_call(
        _merge_mm_res_kernel,
        out_shape=jax.ShapeDtypeStruct((M, N), F32),
        grid=(M // tm,),
        in_specs=[row] * 6 + [pl.BlockSpec((K, N), lambda i: (0, 0)),
                              pl.BlockSpec((tm, N), lambda i: (i, 0))],
        out_specs=pl.BlockSpec((tm, N), lambda i: (i, 0)),
        compiler_params=_cparams(("parallel",)),
        name="merge_matmul_residual",
    )(*os_, *ls_, w, res)


def _rmsnorm_kernel(x_ref, g_ref, o_ref):
    o_ref[...] = _rms_norm_rows(x_ref[...], g_ref[...])


def rmsnorm_rows(x, g, *, tm):
    M, D = x.shape
    return pl.pallas_call(
        _rmsnorm_kernel,
        out_shape=jax.ShapeDtypeStruct((M, D), F32),
        grid=(M // tm,),
        in_specs=[pl.BlockSpec((tm, D), lambda i: (i, 0)), pl.BlockSpec((1, D), lambda i: (0, 0))],
        out_specs=pl.BlockSpec((tm, D), lambda i: (i, 0)),
        compiler_params=_cparams(("parallel",)),
        name="rmsnorm",
    )(x, g.reshape(1, D))


def _online_update(carry, s, mask, v):
    m_i, l_i, acc = carry
    s = jnp.where(mask, s, NEG)
    m_new = jnp.maximum(m_i, jnp.max(s, axis=-1, keepdims=True))
    alpha = jnp.exp(m_i - m_new)
    p = jnp.where(mask, jnp.exp(s - m_new), 0.0)
    return (m_new, alpha * l_i + jnp.sum(p, axis=-1, keepdims=True), alpha * acc + _dot(p.astype(BF16), v))


def _online_init(rows, width):
    return (jnp.full((rows, 1), NEG, F32), jnp.zeros((rows, 1), F32), jnp.zeros((rows, width), F32))


def _attend_heads(q, segments, n_heads):
    nq = q.shape[0]
    outs, lses = [], []
    for h in range(n_heads):
        qh = q[:, h * HEAD_DIM:(h + 1) * HEAD_DIM].astype(BF16)
        carry = _online_init(nq, HEAD_DIM)
        for k_rows, v_rows, mask in segments:
            carry = _online_update(carry, _dot_nt(qh, k_rows(h)) * SCALE, mask, v_rows(h))
        m, den, acc = carry
        outs.append(acc / den)
        lses.append(jnp.broadcast_to(m + jnp.log(den), (nq, HEAD_DIM)))
    return jnp.concatenate(outs, axis=1), jnp.concatenate(lses, axis=1)


def _pad_rows(x, n):
    return jnp.concatenate([x, jnp.zeros((n - x.shape[0], x.shape[1]), x.dtype)], axis=0)


def _band_attn_kernel(q_ref, kp_ref, kc_ref, vp_ref, vc_ref, o_ref, l_ref, *, nw):
    n = pl.program_id(2)
    kp, kc, vp, vc = kp_ref[...], kc_ref[...], vp_ref[...], vc_ref[...]
    i = lax.broadcasted_iota(jnp.int32, (A_BLOCK, 2 * A_BLOCK), 0)
    j = lax.broadcasted_iota(jnp.int32, (A_BLOCK, 2 * A_BLOCK), 1)
    rel = A_BLOCK + i - j
    mask = (rel >= 0) & (rel <= nw) & ((j >= A_BLOCK) | (n > 0))

    def rows(prev, cur):
        return lambda h: jnp.concatenate([prev[:, h * HEAD_DIM:(h + 1) * HEAD_DIM],
                                          cur[:, h * HEAD_DIM:(h + 1) * HEAD_DIM]], axis=0).astype(BF16)

    o, l = _attend_heads(q_ref[...], [(rows(kp, kc), rows(vp, vc), mask)], A_HEADS)
    o_ref[...] = o
    l_ref[...] = l


def band_attention(proj, g, window, dil, B, S):
    T, ncol = proj.shape
    L = S // dil
    nb = L // A_BLOCK
    hw = A_HEADS * HEAD_DIM
    per_res = ncol // hw
    view = proj.reshape(T // dil, dil * ncol)

    def spec(c, prev):
        def imap(b, r, n):
            nn = jnp.maximum(n - 1, 0) if prev else n
            return (b * nb + nn, r * per_res + g * 3 + c)
        return pl.BlockSpec((A_BLOCK, hw), imap)

    out_spec = pl.BlockSpec((A_BLOCK, hw), lambda b, r, n: (b * nb + n, r))
    o, l = pl.pallas_call(
        functools.partial(_band_attn_kernel, nw=window // dil),
        out_shape=[jax.ShapeDtypeStruct((T // dil, dil * hw), F32)] * 2,
        grid=(B, dil, nb),
        in_specs=[spec(0, False), spec(1, True), spec(1, False), spec(2, True), spec(2, False)],
        out_specs=[out_spec, out_spec],
        compiler_params=_cparams(("parallel", "parallel", "arbitrary")),
        name=f"band_attention_d{dil}",
    )(view, view, view, view, view)
    return o.reshape(T, hw), l.reshape(T, hw)


def _a_step_kernel(q_ref, k_ref, v_ref, c_ref, o_any, l_any, o_ref, l_ref, s_ref, *, win, dil):
    del o_any, l_any
    W = c_ref.shape[1]
    Sd = q_ref.shape[0]
    hw = A_HEADS * HEAD_DIM
    knew, vnew = k_ref[...], v_ref[...]
    s_ref[0, pl.ds(0, W - Sd), :] = c_ref[0, pl.ds(Sd, W - Sd), :]
    s_ref[0, pl.ds(W - Sd, Sd), pl.ds(0, hw)] = knew
    s_ref[0, pl.ds(W - Sd, Sd), pl.ds(hw, hw)] = vnew

    def on_stride(dist):
        return (dist >= 0) & (dist <= win) & (jnp.bitwise_and(dist, dil - 1) == 0)

    dist_buf = (W + lax.broadcasted_iota(jnp.int32, (Sd, W), 0)) - lax.broadcasted_iota(jnp.int32, (Sd, W), 1)
    ncol = lax.broadcasted_iota(jnp.int32, (Sd, LANES), 1)
    dist_new = lax.broadcasted_iota(jnp.int32, (Sd, LANES), 0) - ncol
    knew_p, vnew_p = _pad_rows(knew, LANES), _pad_rows(vnew, LANES)
    buf_rows = lambda off: lambda h: c_ref[0, :, pl.ds(off + h * HEAD_DIM, HEAD_DIM)].astype(BF16)
    new_rows = lambda x: lambda h: x[:, h * HEAD_DIM:(h + 1) * HEAD_DIM].astype(BF16)
    segments = [(buf_rows(0), buf_rows(hw), on_stride(dist_buf)),
                (new_rows(knew_p), new_rows(vnew_p), on_stride(dist_new) & (ncol < Sd))]
    o, l = _attend_heads(q_ref[...], segments, A_HEADS)
    o_ref[...] = o
    l_ref[...] = l


def a_step(proj, g, window, dil, cache, o_full, l_full, Tp, Bd, Sd):
    assert dil & (dil - 1) == 0
    T = proj.shape[0]
    W = cache.shape[1]
    hw = A_HEADS * HEAD_DIM
    r0 = Tp // Sd
    col = lambda c: pl.BlockSpec((Sd, hw), lambda b: (r0 + b, g * 3 + c))
    any_spec = pl.BlockSpec(memory_space=pl.ANY)
    row_out = pl.BlockSpec((Sd, hw), lambda b: (r0 + b, 0))
    return pl.pallas_call(
        functools.partial(_a_step_kernel, win=window, dil=dil),
        out_shape=[jax.ShapeDtypeStruct((T, hw), F32), jax.ShapeDtypeStruct((T, hw), F32),
                   jax.ShapeDtypeStruct(cache.shape, F32)],
        grid=(Bd,),
        in_specs=[col(0), col(1), col(2), pl.BlockSpec((1, W, 2 * hw), lambda b: (b, 0, 0)), any_spec, any_spec],
        out_specs=[row_out, row_out, pl.BlockSpec((1, W, 2 * hw), lambda b: (b, 0, 0))],
        input_output_aliases={4: 0, 5: 1},
        compiler_params=_cparams(("parallel",)),
        name=f"a_step_d{dil}",
    )(proj, proj, proj, cache, o_full, l_full)


def _topk_cols(s, k, payload=None):
    N = s.shape[0]
    iota = lax.broadcasted_iota(jnp.int32, s.shape, 0).astype(F32)
    vals, picks = [], []
    for _ in range(k):
        m = jnp.max(s, axis=0, keepdims=True)
        idx = jnp.min(jnp.where(s == m, iota, float(N)), axis=0, keepdims=True)
        sel = iota == idx
        vals.append(m)
        if payload is None:
            picks.append(idx)
        else:
            picks.append(jnp.max(jnp.where(sel, payload, -1.0), axis=0, keepdims=True))
        s = jnp.where(sel, -jnp.inf, s)
    return vals, picks


def _topk_mask_cols(s, k):
    N = s.shape[0]
    iota = lax.broadcasted_iota(jnp.int32, s.shape, 0).astype(F32)
    chosen = jnp.zeros(s.shape, F32)
    for _ in range(k):
        m = jnp.max(s, axis=0, keepdims=True)
        idx = jnp.min(jnp.where(s == m, iota, float(N)), axis=0, keepdims=True)
        sel = iota == idx
        chosen = jnp.where(sel, 1.0, chosen)
        s = jnp.where(sel, -jnp.inf, s)
    return chosen


def _peer_route_kernel(x_ref, g_ref, wqh_ref, wql_ref, skh_ref, skl_ref,
                       hn_ref, e_ref, w_ref, xh_ref, xl_ref, et_ref, wt_ref):
    h = pl.program_id(1)

    @pl.when(h == 0)
    def _():
        xn = _rms_norm_rows(x_ref[...], g_ref[...])
        hi, lo = _split2(xn)
        xh_ref[...] = hi
        xl_ref[...] = lo
        hn_ref[...] = hi

    xh, xl = xh_ref[...], xl_ref[...]
    wqh, wql = wqh_ref[...], wql_ref[...]
    q = _dot(xh, wqh) + _dot(xh, wql) + _dot(xl, wqh)
    tops = []
    for z in range(2):
        qh, ql = _split2(q[:, z * PEER_KEYS:(z + 1) * PEER_KEYS])
        skh, skl = skh_ref[z], skl_ref[z]
        st = _dot_nt(skh, qh) + _dot_nt(skl, qh) + _dot_nt(skh, ql)
        tops.append(_topk_cols(st, PEER_TOPK))
    (v1, i1), (v2, i2) = tops
    v2m = jnp.concatenate(v2, axis=0)
    i2m = jnp.concatenate(i2, axis=0)
    cand = jnp.concatenate([v1[a] + v2m for a in range(PEER_TOPK)], axis=0)
    cand_e = jnp.concatenate([i1[a] * float(PEER_KEYS) + i2m for a in range(PEER_TOPK)], axis=0)
    best, best_e = _topk_cols(cand, PEER_TOPK, payload=cand_e)
    bs = jnp.concatenate(best, axis=0)
    be = jnp.concatenate(best_e, axis=0)
    ex = jnp.exp(bs - best[0])
    gw = ex / jnp.sum(ex, axis=0, keepdims=True)
    row = pl.multiple_of(h * PEER_TOPK, PEER_TOPK)
    et_ref[pl.ds(row, PEER_TOPK), :] = be
    wt_ref[pl.ds(row, PEER_TOPK), :] = gw

    @pl.when(h == PEER_HEADS - 1)
    def _():
        e_ref[...] = et_ref[...].T.astype(jnp.int32)
        w_ref[...] = wt_ref[...].T


def peer_route(x, g, wq_hi, wq_lo, sk_hi, sk_lo, *, tm):
    T, D = x.shape
    nslot = PEER_HEADS * PEER_TOPK
    return pl.pallas_call(
        _peer_route_kernel,
        out_shape=[jax.ShapeDtypeStruct((T, D), BF16),
                   jax.ShapeDtypeStruct((T, nslot), jnp.int32),
                   jax.ShapeDtypeStruct((T, nslot), F32)],
        grid=(T // tm, PEER_HEADS),
        in_specs=[pl.BlockSpec((tm, D), lambda i, h: (i, 0)),
                  pl.BlockSpec((1, D), lambda i, h: (0, 0)),
                  pl.BlockSpec((D, PEER_QDIM), lambda i, h: (0, h)),
                  pl.BlockSpec((D, PEER_QDIM), lambda i, h: (0, h)),
                  pl.BlockSpec((2, PEER_KEYS, PEER_QDIM // 2), lambda i, h: (0, 0, 0)),
                  pl.BlockSpec((2, PEER_KEYS, PEER_QDIM // 2), lambda i, h: (0, 0, 0))],
        out_specs=[pl.BlockSpec((tm, D), lambda i, h: (i, 0)),
                   pl.BlockSpec((tm, nslot), lambda i, h: (i, 0)),
                   pl.BlockSpec((tm, nslot), lambda i, h: (i, 0))],
        scratch_shapes=[pltpu.VMEM((tm, D), BF16), pltpu.VMEM((tm, D), BF16),
                        pltpu.VMEM((nslot, tm), F32), pltpu.VMEM((nslot, tm), F32)],
        compiler_params=_cparams(("parallel", "arbitrary")),
        name="peer_route",
    )(x, g.reshape(1, D), wq_hi, wq_lo, sk_hi, sk_lo)


def _peer_expert_kernel(hn_ref, e_ref, w_ref, x_ref, u_ref, v_ref, o_ref, gs_ref, acc_ref, *, tm, rows_per_chunk):
    c = pl.program_id(1)
    nk = PEER_KEYS

    @pl.when(c == 0)
    def _():
        acc_ref[...] = jnp.zeros_like(acc_ref)
        sub = lax.broadcasted_iota(jnp.int32, (nk, nk), 0)

        def body(t, carry):
            e = e_ref[pl.ds(t, 1), :]
            w = w_ref[pl.ds(t, 1), :]
            i1 = jnp.right_shift(e, 7)
            i2 = jnp.bitwise_and(e, nk - 1)
            pt = jnp.where(sub == i1, w, 0.0).astype(BF16)
            qt = jnp.where(sub == i2, 1.0, 0.0).astype(BF16)
            gs_ref[pl.ds(pl.multiple_of(t * G_PITCH, 8), nk), :] = _dot_nt(pt, qt)
            return carry

        lax.fori_loop(0, tm, body, 0)

    act = jax.nn.gelu(_dot_nt(hn_ref[...], u_ref[...]))
    parts = []
    for r in range(rows_per_chunk):
        gr = gs_ref[pl.ds(c * rows_per_chunk + r, tm, stride=G_PITCH), :]
        parts.append((act[:, r * nk:(r + 1) * nk] * gr).astype(BF16))
    acc_ref[...] += _dot(jnp.concatenate(parts, axis=1), v_ref[...])

    @pl.when(c == pl.num_programs(1) - 1)
    def _():
        o_ref[...] = x_ref[...] + acc_ref[...]


def peer_experts(hn, e, w, x, u, v, *, tm, chunk):
    T, D = x.shape
    n_exp = u.shape[0]
    nslot = e.shape[1]
    rows_per_chunk = chunk // PEER_KEYS
    return pl.pallas_call(
        functools.partial(_peer_expert_kernel, tm=tm, rows_per_chunk=rows_per_chunk),
        out_shape=jax.ShapeDtypeStruct((T, D), F32),
        grid=(T // tm, n_exp // chunk),
        in_specs=[pl.BlockSpec((tm, D), lambda i, c: (i, 0)),
                  pl.BlockSpec((tm, nslot), lambda i, c: (i, 0)),
                  pl.BlockSpec((tm, nslot), lambda i, c: (i, 0)),
                  pl.BlockSpec((tm, D), lambda i, c: (i, 0)),
                  pl.BlockSpec((chunk, D), lambda i, c: (c, 0)),
                  pl.BlockSpec((chunk, D), lambda i, c: (c, 0))],
        out_specs=pl.BlockSpec((tm, D), lambda i, c: (i, 0)),
        scratch_shapes=[pltpu.VMEM((tm * G_PITCH, PEER_KEYS), F32), pltpu.VMEM((tm, D), F32)],
        compiler_params=_cparams(("parallel", "arbitrary")),
        name="peer_experts",
    )(hn, e, w, x, u, v)


def peer_layer(x, g, w_q, subkeys, u, v, *, tm_route, tm_exp, chunk):
    wq_hi, wq_lo = _split2(w_q)
    sk_hi, sk_lo = _split2(subkeys)
    hn, e, w = peer_route(x, g, wq_hi, wq_lo, sk_hi, sk_lo, tm=tm_route)
    return peer_experts(hn, e, w, x, u.astype(BF16), v.astype(BF16), tm=tm_exp, chunk=chunk)


def _compress_core(get_x, w1k_ref, w1v_ref, pe_ref, w1f_ref, w2k_ref, w2v_ref, cos_ref, sa_ref, sb_ref,
                   kc_ref, vc_ref, acc_ref, n_cmp):
    nch = acc_ref.shape[0]
    for l in range(CMP_STRIDE):
        x = get_x(l)
        yk = _dot(x[:, :KV_W].astype(BF16), w1k_ref[l])
        yv = _dot(x[:, KV_W:].astype(BF16), w1v_ref[l])
        if l == 0:
            acc_ref[:, pl.ds(0, 2 * KV_W)] = yk
            acc_ref[:, pl.ds(2 * KV_W, 2 * KV_W)] = yv
        else:
            acc_ref[:, pl.ds(0, 2 * KV_W)] += yk
            acc_ref[:, pl.ds(2 * KV_W, 2 * KV_W)] += yv
    keep = lax.broadcasted_iota(jnp.int32, (nch, KV_W), 0) < n_cmp
    for z, (w2_ref, out_ref) in enumerate(((w2k_ref, kc_ref), (w2v_ref, vc_ref))):
        ph, pl_ = _split2(pe_ref[z])
        wh, wl = _split2(w1f_ref[z])
        pe_t = _dot(ph, wh) + _dot(ph, wl) + _dot(pl_, wh)
        pe_t = jnp.concatenate([pe_t] * B_KV_HEADS, axis=1)
        lo = acc_ref[:, pl.ds(z * 2 * KV_W, KV_W)]
        hi = acc_ref[:, pl.ds(z * 2 * KV_W + KV_W, KV_W)]
        hid = jax.nn.gelu(lo + pltpu.roll(hi, nch - 1, 0) + pe_t)
        out = _dot(hid.astype(BF16), w2_ref[...])
        if z == 0:
            out = _rope_lanes(out, cos_ref[...], sa_ref[...], sb_ref[...])
        out_ref[0] = jnp.where(keep, out, 0.0)


def _compress_prompt_kernel(x0, x1, x2, x3, *rest, n_cmp):
    slabs = (x0, x1, x2, x3)
    nch = rest[-1].shape[0]

    def get_x(l):
        return jnp.concatenate([s[pl.ds(l, nch, stride=CMP_STRIDE), :] for s in slabs], axis=1)

    _compress_core(get_x, *rest, n_cmp=n_cmp)


def _compress_weights(w_cmp1, w_cmp2, pe_cmp, n_chunk):
    eye = jnp.eye(B_KV_HEADS, dtype=F32)
    kron = lambda w: jnp.kron(eye, w)
    w1 = []
    for z in range(2):
        lo = jnp.stack([kron(w_cmp1[z, l]) for l in range(CMP_STRIDE)])
        hi = jnp.stack([kron(w_cmp1[z, CMP_STRIDE + l]) for l in range(CMP_STRIDE)])
        w1.append(jnp.concatenate([lo, hi], axis=2).astype(BF16))
    w2 = [kron(w_cmp2[z]).astype(BF16) for z in range(2)]
    pe = pe_cmp.reshape(2, 1, CMP_LEN * HEAD_DIM)
    w1f = w_cmp1.reshape(2, CMP_LEN * HEAD_DIM, -1)
    end_pos = jnp.arange(n_chunk, dtype=jnp.int32) * CMP_STRIDE + (CMP_LEN - 1)
    tabs = _rope_tables(end_pos, KV_W)
    return (w1[0], w1[1], pe, w1f, w2[0], w2[1]) + tabs


def _const_specs(arrs):
    return [pl.BlockSpec(a.shape, (lambda *i, nd=a.ndim: (0,) * nd), pipeline_mode=pl.Buffered(1)) for a in arrs]


def compress_prompt(kv, cw, B, S):
    nch = S // CMP_STRIDE
    slab = lambda s: pl.BlockSpec((S, LANES), lambda b: (b, s), pipeline_mode=pl.Buffered(1))
    out = pl.BlockSpec((1, nch, KV_W), lambda b: (b, 0, 0))
    return pl.pallas_call(
        functools.partial(_compress_prompt_kernel, n_cmp=nch - 1),
        out_shape=[jax.ShapeDtypeStruct((B, nch, KV_W), F32)] * 2,
        grid=(B,),
        in_specs=[slab(s) for s in range(4)] + _const_specs(cw),
        out_specs=[out, out],
        scratch_shapes=[pltpu.VMEM((nch, 4 * KV_W), F32)],
        compiler_params=_cparams(("parallel",)),
        name="compress_prompt",
    )(kv, kv, kv, kv, *cw)


def _compress_sample_kernel(pt_ref, *refs, n_cmp, pages_per_step):
    n_pg = 4 * pages_per_step
    pages, rest, x_ref = refs[:n_pg], refs[n_pg:-1], refs[-1]
    j = pl.program_id(1)
    ch_per_page = pages[0].shape[1] // CMP_STRIDE
    for p in range(pages_per_step):
        r0 = pl.multiple_of((j * pages_per_step + p) * ch_per_page, ch_per_page)
        for s in range(4):
            for l in range(CMP_STRIDE):
                x_ref[l, pl.ds(r0, ch_per_page), pl.ds(s * LANES, LANES)] = (
                    pages[p * 4 + s][0, pl.ds(l, ch_per_page, stride=CMP_STRIDE), :])

    @pl.when(j == pl.num_programs(1) - 1)
    def _():
        _compress_core(lambda l: x_ref[l], *rest, n_cmp=n_cmp)


def compress_sample(cache, page_table, cw, *, pages_per_step):
    Bd, n_pages = page_table.shape
    page = cache.shape[1]
    nch = n_pages * page // CMP_STRIDE
    P = pages_per_step

    def pspec(p, s):
        return pl.BlockSpec((1, page, LANES), lambda b, j, pt: (pt[b, j * P + p], 0, s))

    out = pl.BlockSpec((1, nch, KV_W), lambda b, j, pt: (b, 0, 0))
    return pl.pallas_call(
        functools.partial(_compress_sample_kernel, n_cmp=nch - 1, pages_per_step=P),
        out_shape=[jax.ShapeDtypeStruct((Bd, nch, KV_W), F32)] * 2,
        grid_spec=pltpu.PrefetchScalarGridSpec(
            num_scalar_prefetch=1,
            grid=(Bd, n_pages // P),
            in_specs=[pspec(p, s) for p in range(P) for s in range(4)] + _const_specs(cw),
            out_specs=[out, out],
            scratch_shapes=[pltpu.VMEM((nch, 4 * KV_W), F32),
                            pltpu.VMEM((CMP_STRIDE, nch, 2 * KV_W), F32)],
        ),
        compiler_params=_cparams(("parallel", "arbitrary")),
        name="compress_sample",
    )(page_table, *([cache] * (4 * P)), *cw)


def _select_blocks(imp, wsel, blk, cur):
    i_hi = imp.astype(BF16)
    r1 = imp - i_hi.astype(F32)
    i_mid = r1.astype(BF16)
    i_lo = (r1 - i_mid.astype(F32)).astype(BF16)
    imp_sel = _dot(i_hi, wsel) + _dot(i_mid, wsel) + _dot(i_lo, wsel)
    forced = (blk == 0) | (blk == cur) | (blk == cur - 1)
    imp_sel = jnp.where(blk > cur, -jnp.inf, jnp.where(forced, jnp.inf, imp_sel))
    return _topk_mask_cols(imp_sel.T, N_SEL).T


def _tile_block_mask(chosen_q, k0, kt_size):
    n_sb = chosen_q.shape[1]
    blk = lax.broadcasted_iota(jnp.int32, (n_sb, kt_size), 0)
    key_blk = jnp.right_shift(k0 + lax.broadcasted_iota(jnp.int32, (n_sb, kt_size), 1), 6)
    expand = jnp.where(blk == key_blk, 1.0, 0.0).astype(BF16)
    return _dot(chosen_q, expand) > 0.5


def _nsa_prompt_kernel(q_ref, gate_ref, kc_ref, vc_ref, ks_ref, vs_ref, kw_ref, vw_ref, wsel_ref,
                       o_ref, *, n_cmp, kt_size):
    n = pl.program_id(1)
    t0 = n * Q_BLOCK
    G, TQ = B_GROUP, Q_BLOCK
    R = G * TQ
    NC = kc_ref.shape[1]
    NSB = wsel_ref.shape[1]
    wsel = wsel_ref[...]
    gt = gate_ref[...]
    row_c = lax.broadcasted_iota(jnp.int32, (R, NC), 0)
    col_c = lax.broadcasted_iota(jnp.int32, (R, NC), 1)
    valid_c = (col_c * CMP_STRIDE + (CMP_LEN - 1) <= t0 + jnp.bitwise_and(row_c, TQ - 1)) & (col_c < n_cmp)
    blk = lax.broadcasted_iota(jnp.int32, (TQ, NSB), 1)
    cur = jnp.right_shift(t0 + lax.broadcasted_iota(jnp.int32, (TQ, NSB), 0), 6)
    qpos_k = t0 + lax.broadcasted_iota(jnp.int32, (TQ, kt_size), 0)
    kcol = lax.broadcasted_iota(jnp.int32, (TQ, kt_size), 1)
    WK = B_WINDOW + TQ
    w0 = pl.multiple_of(jnp.maximum(t0 - B_WINDOW, 0), TQ)
    wpos = w0 + lax.broadcasted_iota(jnp.int32, (R, WK), 1)
    rel = t0 + jnp.bitwise_and(lax.broadcasted_iota(jnp.int32, (R, WK), 0), TQ - 1) - wpos
    valid_w = (rel >= 0) & (rel <= B_WINDOW)
    n_kt = (t0 + TQ + kt_size - 1) // kt_size

    outs = []
    for h in range(B_KV_HEADS):
        hs = pl.ds(h * HEAD_DIM, HEAD_DIM)
        qs = jnp.concatenate([q_ref[:, pl.ds((h * G + g) * HEAD_DIM, HEAD_DIM)] for g in range(G)], axis=0)
        qb = qs.astype(BF16)

        p_c = _softmax_rows(_dot_nt3(qs, kc_ref[0, :, hs]) * SCALE, valid_c)
        o_c = _dot(p_c.astype(BF16), vc_ref[0, :, hs].astype(BF16))
        imp = p_c[0:TQ]
        for g in range(1, G):
            imp = imp + p_c[g * TQ:(g + 1) * TQ]
        chosen_q = _select_blocks(imp, wsel, blk, cur).astype(BF16)

        def sel_body(kt, carry):
            k0 = pl.multiple_of(kt * kt_size, kt_size)
            s = _dot_nt(qb, ks_ref[pl.ds(k0, kt_size), hs]) * SCALE
            mk = _tile_block_mask(chosen_q, k0, kt_size) & (k0 + kcol <= qpos_k)
            return _online_update(carry, s, jnp.concatenate([mk] * G, axis=0), vs_ref[pl.ds(k0, kt_size), hs])

        _, l_s, acc_s = lax.fori_loop(0, n_kt, sel_body, _online_init(R, HEAD_DIM))
        o_s = acc_s / jnp.where(l_s > 0, l_s, 1.0)

        p_w = _softmax_rows(_dot_nt(qb, kw_ref[pl.ds(w0, WK), hs]) * SCALE, valid_w)
        o_w = _dot(p_w.astype(BF16), vw_ref[pl.ds(w0, WK), hs])

        for g in range(G):
            rs = slice(g * TQ, (g + 1) * TQ)
            c = 3 * (h * G + g)
            outs.append(gt[:, c:c + 1] * o_c[rs] + gt[:, c + 1:c + 2] * o_s[rs] + gt[:, c + 2:c + 3] * o_w[rs])
    o_ref[...] = jnp.concatenate(outs, axis=1)


def nsa_prompt(q, gate, kc, vc, kvb, wsel, B, S, *, kt_size):
    T, nq = q.shape
    nqb = S // Q_BLOCK
    NC = kc.shape[1]
    comp = lambda c: pl.BlockSpec((S, KV_W), lambda b, n: (b, c))
    cmp_spec = pl.BlockSpec((1, NC, KV_W), lambda b, n: (b, 0, 0))
    return pl.pallas_call(
        functools.partial(_nsa_prompt_kernel, n_cmp=S // CMP_STRIDE - 1, kt_size=kt_size),
        out_shape=jax.ShapeDtypeStruct((T, nq), F32),
        grid=(B, nqb),
        in_specs=[pl.BlockSpec((Q_BLOCK, nq), lambda b, n: (b * nqb + n, 0)),
                  pl.BlockSpec((Q_BLOCK, LANES), lambda b, n: (b * nqb + n, 0)),
                  cmp_spec, cmp_spec, comp(2), comp(3), comp(4), comp(5),
                  pl.BlockSpec(wsel.shape, lambda b, n: (0, 0))],
        out_specs=pl.BlockSpec((Q_BLOCK, nq), lambda b, n: (b * nqb + n, 0)),
        compiler_params=_cparams(("parallel", "arbitrary")),
        name="nsa_prompt",
    )(q, gate, kc, vc, kvb, kvb, kvb, kvb, wsel)


def _nsa_step_kernel(pt_ref, q_ref, gate_ref, kc_ref, vc_ref, new_ref, win_ref, wsel_ref, *rest,
                     past, n_cmp, pages_per_step):
    P = pages_per_step
    pages = rest[:P]
    o_any, o_ref, st_ref, a_ref, ch_ref, m_ref, l_ref, acc_ref, oc_ref = rest[P:]
    del o_any, pt_ref
    j = pl.program_id(1)
    G, KVH = B_GROUP, B_KV_HEADS
    Sd = q_ref.shape[0]
    R = B_HEADS * Sd
    page = pages[0].shape[1]
    kt = P * page
    row_head = lambda shape: lax.broadcasted_iota(jnp.int32, shape, 0) // (G * Sd)
    own = lambda width: row_head((R, width)) == lax.broadcasted_iota(jnp.int32, (R, width), 1) // HEAD_DIM

    def diag(x):
        rows = G * Sd
        return jnp.concatenate([x[h * rows:(h + 1) * rows, h * HEAD_DIM:(h + 1) * HEAD_DIM]
                                for h in range(KVH)], axis=0)

    @pl.when(j == 0)
    def _():
        q = q_ref[...]
        a = jnp.concatenate([q[:, hd * HEAD_DIM:(hd + 1) * HEAD_DIM] for hd in range(B_HEADS)], axis=0)
        a = jnp.where(own(KV_W), jnp.concatenate([a] * KVH, axis=1), 0.0)
        a_ref[...] = a.astype(BF16)
        NC = kc_ref.shape[1]
        valid_c = lax.broadcasted_iota(jnp.int32, (R, NC), 1) < n_cmp
        p_c = _softmax_rows(_dot_nt3(a, kc_ref[0]) * SCALE, valid_c)
        oc_ref[...] = diag(_dot(p_c.astype(BF16), vc_ref[0].astype(BF16)))
        pooled = jnp.sum(p_c.reshape(KVH, G, Sd, NC), axis=1, keepdims=True)
        imp = jnp.broadcast_to(pooled, (KVH, G, Sd, NC)).reshape(R, NC)
        NSB = wsel_ref.shape[1]
        blk = lax.broadcasted_iota(jnp.int32, (R, NSB), 1)
        srow = jnp.bitwise_and(lax.broadcasted_iota(jnp.int32, (R, NSB), 0), Sd - 1)
        cur = jnp.right_shift(past + srow, 6)
        ch_ref[...] = _select_blocks(imp, wsel_ref[...], blk, cur).astype(BF16)
        m_ref[...] = jnp.full(m_ref.shape, NEG, F32)
        l_ref[...] = jnp.zeros(l_ref.shape, F32)
        acc_ref[...] = jnp.zeros(acc_ref.shape, F32)

    a = a_ref[...]
    ks = jnp.concatenate([pg[0, :, pl.ds(0, KV_W)] for pg in pages], axis=0).astype(BF16)
    vs = jnp.concatenate([pg[0, :, pl.ds(KV_W, KV_W)] for pg in pages], axis=0).astype(BF16)
    mk = _tile_block_mask(ch_ref[...], j * kt, kt)
    m_n, l_n, acc_n = _online_update((m_ref[...], l_ref[...], acc_ref[...]), _dot_nt(a, ks) * SCALE, mk, vs)
    m_ref[...] = m_n
    l_ref[...] = l_n
    acc_ref[...] = acc_n

    @pl.when(j == pl.num_programs(1) - 1)
    def _():
        new = _pad_rows(new_ref[...], LANES)
        comp = lambda c: new[:, c * KV_W:(c + 1) * KV_W].astype(BF16)
        srow = jnp.bitwise_and(lax.broadcasted_iota(jnp.int32, (R, LANES), 0), Sd - 1)
        scol = lax.broadcasted_iota(jnp.int32, (R, LANES), 1)
        causal_new = (scol <= srow) & (scol < Sd)
        nblk = past // SEL_BLOCK
        in_new = ch_ref[:, nblk:nblk + 1].astype(F32) > 0.5
        _, l_s, acc_s = _online_update((m_n, l_n, acc_n), _dot_nt(a, comp(2)) * SCALE, in_new & causal_new, comp(3))
        o_s = diag(acc_s) / jnp.where(l_s > 0, l_s, 1.0)

        W = win_ref.shape[1]
        wrow = jnp.bitwise_and(lax.broadcasted_iota(jnp.int32, (R, W), 0), Sd - 1)
        rel = W + wrow - lax.broadcasted_iota(jnp.int32, (R, W), 1)
        carry = _online_update(_online_init(R, KV_W), _dot_nt(a, win_ref[0, :, pl.ds(0, KV_W)].astype(BF16)) * SCALE,
                               (rel >= 0) & (rel <= B_WINDOW), win_ref[0, :, pl.ds(KV_W, KV_W)].astype(BF16))
        _, l_w, acc_w = _online_update(carry, _dot_nt(a, comp(4)) * SCALE, causal_new, comp(5))
        o_w = diag(acc_w) / jnp.where(l_w > 0, l_w, 1.0)
        o_c = oc_ref[...]
        gt = gate_ref[...]
        outs = []
        for hd in range(B_HEADS):
            rs = slice(hd * Sd, (hd + 1) * Sd)
            c = 3 * hd
            outs.append(gt[:, c:c + 1] * o_c[rs] + gt[:, c + 1:c + 2] * o_s[rs] + gt[:, c + 2:c + 3] * o_w[rs])
        o_ref[...] = jnp.concatenate(outs, axis=1)
        st_ref[0, pl.ds(0, W - Sd), :] = win_ref[0, pl.ds(Sd, W - Sd), :]
        st_ref[0, pl.ds(W - Sd, Sd), :] = new_ref[:, pl.ds(4 * KV_W, 2 * KV_W)]


def nsa_step(q, gate, kc, vc, kv, win, cache, page_table, wsel, o_full, Tp, Bd, Sd, *, pages_per_step):
    T, nq = q.shape
    n_pages = page_table.shape[1]
    page = cache.shape[1]
    past = n_pages * page
    W = win.shape[1]
    NC = kc.shape[1]
    P = pages_per_step
    r0 = Tp // Sd
    R = B_HEADS * Sd
    rows = lambda w: pl.BlockSpec((Sd, w), lambda b, j, pt: (r0 + b, 0))
    per_b = lambda s1, s2: pl.BlockSpec((1, s1, s2), lambda b, j, pt: (b, 0, 0))
    pspec = lambda p: pl.BlockSpec((1, page, 2 * KV_W), lambda b, j, pt: (pt[b, j * P + p], 0, 1))
    return pl.pallas_call(
        functools.partial(_nsa_step_kernel, past=past, n_cmp=past // CMP_STRIDE - 1, pages_per_step=P),
        out_shape=[jax.ShapeDtypeStruct((T, nq), F32), jax.ShapeDtypeStruct(win.shape, F32)],
        grid_spec=pltpu.PrefetchScalarGridSpec(
            num_scalar_prefetch=1,
            grid=(Bd, n_pages // P),
            in_specs=[rows(nq), rows(LANES), per_b(NC, KV_W), per_b(NC, KV_W), rows(kv.shape[1]),
                      per_b(W, 2 * KV_W), pl.BlockSpec(wsel.shape, lambda b, j, pt: (0, 0))]
                     + [pspec(p) for p in range(P)] + [pl.BlockSpec(memory_space=pl.ANY)],
            out_specs=[rows(nq), per_b(W, 2 * KV_W)],
            scratch_shapes=[pltpu.VMEM((R, KV_W), BF16), pltpu.VMEM((R, wsel.shape[1]), BF16),
                            pltpu.VMEM((R, 1), F32), pltpu.VMEM((R, 1), F32), pltpu.VMEM((R, KV_W), F32),
                            pltpu.VMEM((R, HEAD_DIM), F32)],
        ),
        input_output_aliases={8 + P: 0},
        compiler_params=_cparams(("parallel", "arbitrary")),
        name="nsa_step",
    )(page_table, q, gate, kc, vc, kv, win, wsel, *([cache] * P), o_full)


def _rope_tables(pos, width):
    half = ROT_DIM // 2
    inv = jnp.exp(-jnp.log(jnp.float32(ROPE_THETA)) * jnp.arange(half, dtype=F32) * (2.0 / ROT_DIM))
    ang = pos.astype(F32)[:, None] * inv[None, :]
    cos, sin = jnp.cos(ang), jnp.sin(ang)
    T = pos.shape[0]
    z8 = jnp.zeros((T, half), F32)
    rest = HEAD_DIM - ROT_DIM
    c = jnp.concatenate([cos, cos, jnp.ones((T, rest), F32)], axis=1)
    sa = jnp.concatenate([-sin, z8, jnp.zeros((T, rest), F32)], axis=1)
    sb = jnp.concatenate([z8, sin, jnp.zeros((T, rest), F32)], axis=1)
    reps = width // HEAD_DIM
    return tuple(jnp.tile(t, (1, reps)) for t in (c, sa, sb))


def _overlap_weights(n_cmp, n_cmp_pad, n_sb, n_sb_pad):
    j = jnp.arange(n_sb)
    cidx = (SEL_BLOCK // CMP_STRIDE) * j[:, None] - 1 + jnp.arange(len(SEL_OVERLAP_W))[None, :]
    wts = jnp.where((cidx >= 0) & (cidx < n_cmp), jnp.asarray(SEL_OVERLAP_W, F32), 0.0)
    onehot = (jnp.clip(cidx, 0, n_cmp - 1)[:, :, None] == jnp.arange(n_cmp_pad)[None, None, :]).astype(F32)
    wsel = jnp.einsum('jo,jon->nj', wts, onehot)
    return jnp.pad(wsel, ((0, 0), (0, n_sb_pad - n_sb))).astype(BF16)


def _pick(n, cands):
    for c in cands:
        if n % c == 0:
            return c
    return n


def _pad_to(n, m):
    return -(-n // m) * m


def kernel(x_prompt, x_sample, cache_a_g1, cache_a_g2, cache_a_g3, cache_b_kv, cache_b_win, page_table, g_mix, g_ffn, w_in_a, w_o_a, g_kv, w_kv_b, w_cmp1, w_cmp2, pe_cmp, w_qg_b, b_gate_b, w_o_b, w_peer_q, peer_subkeys, peer_u, peer_v, g_final):
    B, S, D = x_prompt.shape
    Bd, Sd, _ = x_sample.shape
    Tp, Ts = B * S, Bd * Sd
    T = Tp + Ts
    n_phys, page = cache_b_kv.shape[:2]
    n_pages = page_table.shape[1]
    past = n_pages * page
    hw = A_HEADS * HEAD_DIM
    tm_big = _pick(T, (1280, 640, 256, 128, 64, 32, 16, 8))
    tm_mid = _pick(T, (640, 256, 128, 64, 32, 16, 8))
    tm_peer = _pick(T, (256, 128, 64, 32, 16, 8))

    x = jnp.concatenate([x_prompt.reshape(Tp, D), x_sample.reshape(Ts, D)], axis=0)
    pos_p = jnp.arange(S, dtype=jnp.int32)
    pos_s = past + jnp.arange(Sd, dtype=jnp.int32)
    pos = jnp.concatenate([jnp.tile(pos_p, B), jnp.tile(pos_s, Bd)])
    tab512 = _rope_tables(pos, 512)
    tab256 = tuple(t[:, :KV_W] for t in tab512)

    n_grp = len(A_GROUPS)
    proj = rms_matmul(x, g_mix[0], w_in_a[0].astype(BF16), tm=tm_big, tn=hw, epilogue="rope",
                      flags=jnp.asarray([1, 1, 0] * n_grp, jnp.int32), tables=tab512)
    caches_a = (cache_a_g1, cache_a_g2, cache_a_g3)
    o_list, l_list, a_prompt, a_sample = [], [], [], []
    for g, (win, dil) in enumerate(A_GROUPS):
        o_g, l_g = band_attention(proj, g, win, dil, B, S)
        cache = caches_a[g]
        o_g, l_g, state = a_step(proj, g, win, dil, cache[0].reshape(Bd, cache.shape[2], 2 * hw), o_g, l_g,
                                 Tp, Bd, Sd)
        a_sample.append(state.reshape(cache.shape))
        keep = min(win, S)
        kv_cols = proj[:Tp].reshape(B, S, n_grp, 3 * hw)[:, S - keep:, g, hw:]
        a_prompt.append(kv_cols.reshape(1, B, keep, 2, A_HEADS, HEAD_DIM))
        o_list.append(o_g)
        l_list.append(l_g)
    x = merge_matmul_residual(o_list, l_list, w_o_a[0].astype(BF16), x, tm=tm_mid)
    x = peer_layer(x, g_ffn[0], w_peer_q[0], peer_subkeys[0], peer_u[0], peer_v[0],
                   tm_route=tm_peer, tm_exp=tm_peer, chunk=2048)

    kv, kvb = rms_matmul(x, g_kv, w_kv_b.astype(BF16), tm=tm_big, tn=KV_W, epilogue="rope",
                         flags=jnp.asarray([0, 0, 1, 0, 1, 0], jnp.int32), tables=tab256, emit_bf16=True)
    kv_p = kv[:Tp].reshape(B, S, 6, B_KV_HEADS, HEAD_DIM)
    bkv_prompt = kv_p[:, :, :4]
    bkv_sample = kv[Tp:, :4 * KV_W].reshape(Bd, Sd, 4, B_KV_HEADS, HEAD_DIM)
    keep_w = min(B_WINDOW, S)
    bwin_prompt = kv_p[:, S - keep_w:, 4:]
    cache2d = cache_b_kv.reshape(n_phys, page, 4 * KV_W)
    kc_p, vc_p = compress_prompt(kv, _compress_weights(w_cmp1, w_cmp2, pe_cmp, S // CMP_STRIDE), B, S)
    kc_s, vc_s = compress_sample(cache2d, page_table, _compress_weights(w_cmp1, w_cmp2, pe_cmp, past // CMP_STRIDE),
                                 pages_per_step=4)

    nq = B_HEADS * HEAD_DIM
    w_qg = w_qg_b[0]
    q = rms_matmul(x, g_mix[1], w_qg[:, :nq].astype(BF16), tm=tm_big, tn=512, epilogue="rope",
                   flags=jnp.ones((nq // 512,), jnp.int32), tables=tab512)
    ngate = 3 * B_HEADS
    w_gate = jnp.pad(w_qg[:, nq:], ((0, 0), (0, LANES - ngate))).astype(BF16)
    b_gate = jnp.pad(b_gate_b[0], (0, LANES - ngate))
    gate = rms_matmul(x, g_mix[1], w_gate, tm=tm_big, tn=LANES, epilogue="sigmoid", bias=b_gate)

    n_cmp_p = S // CMP_STRIDE - 1
    wsel_p = _overlap_weights(n_cmp_p, S // CMP_STRIDE, S // SEL_BLOCK, _pad_to(S // SEL_BLOCK, LANES))
    o_b = nsa_prompt(q, gate, kc_p, vc_p, kvb, wsel_p, B, S, kt_size=min(512, S))
    n_cmp_s = past // CMP_STRIDE - 1
    n_sb_s = -(-(past + Sd) // SEL_BLOCK)
    wsel_s = _overlap_weights(n_cmp_s, past // CMP_STRIDE, n_sb_s, _pad_to(n_sb_s, LANES))
    Wb = cache_b_win.shape[1]
    o_b, bwin_sample = nsa_step(q, gate, kc_s, vc_s, kv, cache_b_win.reshape(Bd, Wb, 2 * KV_W), cache2d, page_table,
                                wsel_s, o_b, Tp, Bd, Sd, pages_per_step=4)
    x = matmul_residual(o_b, w_o_b[0].astype(BF16), x, tm=tm_mid)
    x = peer_layer(x, g_ffn[1], w_peer_q[1], peer_subkeys[1], peer_u[1], peer_v[1],
                   tm_route=tm_peer, tm_exp=tm_peer, chunk=2048)
    y = rmsnorm_rows(x, g_final, tm=tm_mid)

    return (y[:Tp].reshape(B, S, D), y[Tp:].reshape(Bd, Sd, D),
            a_prompt[0], a_sample[0], a_prompt[1], a_sample[1], a_prompt[2], a_sample[2],
            bkv_prompt, bkv_sample, bwin_prompt, bwin_sample.reshape(cache_b_win.shape))
```

```python
import functools

import jax
import jax.numpy as jnp
from jax import lax
from jax.experimental import pallas as pl
from jax.experimental.pallas import tpu as pltpu

F32 = jnp.float32
BF16 = jnp.bfloat16

HEAD_DIM = 64
ROT_DIM = HEAD_DIM // 4
ROPE_THETA = 500000.0
NORM_EPS = 1e-6
A_GROUPS = ((128, 1), (512, 4), (2048, 16))
A_HEADS = 8
A_BLOCK = 128
B_HEADS = 16
B_KV_HEADS = 4
B_GROUP = B_HEADS // B_KV_HEADS
CMP_LEN = 32
CMP_STRIDE = 16
SEL_BLOCK = 64
N_SEL = 16
SEL_OVERLAP_W = (1.0, 2.0, 2.0, 2.0, 1.0)
B_WINDOW = 512
Q_BLOCK = 128
PEER_HEADS = 8
PEER_KEYS = 128
PEER_QDIM = 256
PEER_TOPK = 16

LANES = 128
NEG = -1e30
VMEM_LIMIT = 50 * 1024 * 1024
G_PITCH = 136
KV_W = B_KV_HEADS * HEAD_DIM
SCALE = HEAD_DIM ** -0.5

_NT = (((1,), (1,)), ((), ()))


def _cparams(sem):
    return pltpu.CompilerParams(dimension_semantics=sem, vmem_limit_bytes=VMEM_LIMIT)


def _split2(a):
    hi = a.astype(BF16)
    lo = (a - hi.astype(F32)).astype(BF16)
    return hi, lo


def _dot(a, b):
    return jnp.dot(a, b, preferred_element_type=F32)


def _dot_nt(a, b):
    return lax.dot_general(a, b, _NT, preferred_element_type=F32)


def _dot_nt3(a, b):
    ah, al = _split2(a)
    bh, bl = _split2(b)
    return _dot_nt(ah, bh) + _dot_nt(ah, bl) + _dot_nt(al, bh)


def _softmax_rows(s, valid):
    s = jnp.where(valid, s, NEG)
    m = jnp.max(s, axis=-1, keepdims=True)
    e = jnp.where(valid, jnp.exp(s - m), 0.0)
    den = jnp.sum(e, axis=-1, keepdims=True)
    return e / jnp.where(den > 0, den, 1.0)


def _rope_lanes(y, cos, sa, sb):
    n = y.shape[-1]
    return y * cos + pltpu.roll(y, 8, 1) * sb + pltpu.roll(y, n - 8, 1) * sa


def _rms_norm_rows(x, g):
    ms = jnp.mean(x * x, axis=-1, keepdims=True)
    return x * lax.rsqrt(ms + NORM_EPS) * g


def _rms_mm_kernel(flags_ref, x_ref, g_ref, w_ref, *rest, epilogue, emit_bf16):
    n_out = 2 if emit_bf16 else 1
    extra, outs, xn_ref = rest[:len(rest) - n_out - 1], rest[len(rest) - n_out - 1:-1], rest[-1]
    j = pl.program_id(1)

    @pl.when(j == 0)
    def _():
        xn_ref[...] = _rms_norm_rows(x_ref[...], g_ref[...]).astype(BF16)

    def store(val):
        outs[0][...] = val
        if emit_bf16:
            outs[1][...] = val.astype(BF16)

    y = _dot(xn_ref[...], w_ref[...])
    if epilogue == "rope":
        cos_ref, sa_ref, sb_ref = extra
        f = flags_ref[j]

        @pl.when(f == 0)
        def _():
            store(y)

        @pl.when(f != 0)
        def _():
            store(_rope_lanes(y, cos_ref[...], sa_ref[...], sb_ref[...]))
    elif epilogue == "sigmoid":
        store(jax.nn.sigmoid(y + extra[0][...]))
    else:
        store(y)


def rms_matmul(x, g, w, *, tm, tn, epilogue="none", flags=None, tables=None, bias=None, emit_bf16=False):
    M, D = x.shape
    N = w.shape[1]
    assert M % tm == 0 and N % tn == 0
    nj = N // tn
    if flags is None:
        flags = jnp.zeros((nj,), jnp.int32)
    in_specs = [
        pl.BlockSpec((tm, D), lambda i, j, f: (i, 0)),
        pl.BlockSpec((1, D), lambda i, j, f: (0, 0)),
        pl.BlockSpec((D, tn), lambda i, j, f: (0, j)),
    ]
    args = [x, g.reshape(1, D), w]
    if epilogue == "rope":
        for t in tables:
            in_specs.append(pl.BlockSpec((tm, tn), lambda i, j, f: (i, 0)))
            args.append(t)
    elif epilogue == "sigmoid":
        in_specs.append(pl.BlockSpec((1, tn), lambda i, j, f: (0, j)))
        args.append(bias.reshape(1, N))
    out_spec = pl.BlockSpec((tm, tn), lambda i, j, f: (i, j))
    out_shape = [jax.ShapeDtypeStruct((M, N), F32)]
    if emit_bf16:
        out_shape.append(jax.ShapeDtypeStruct((M, N), BF16))
    res = pl.pallas_call(
        functools.partial(_rms_mm_kernel, epilogue=epilogue, emit_bf16=emit_bf16),
        out_shape=out_shape,
        grid_spec=pltpu.PrefetchScalarGridSpec(
            num_scalar_prefetch=1,
            grid=(M // tm, nj),
            in_specs=in_specs,
            out_specs=[out_spec] * len(out_shape),
            scratch_shapes=[pltpu.VMEM((tm, D), BF16)],
        ),
        compiler_params=_cparams(("parallel", "arbitrary")),
        name="rms_matmul_" + epilogue,
    )(flags, *args)
    return res if emit_bf16 else res[0]


def _mm_res_kernel(a_ref, w_ref, r_ref, o_ref):
    o_ref[...] = r_ref[...] + _dot(a_ref[...].astype(BF16), w_ref[...])


def matmul_residual(a, w, res, *, tm):
    M, K = a.shape
    N = w.shape[1]
    return pl.pallas_call(
        _mm_res_kernel,
        out_shape=jax.ShapeDtypeStruct((M, N), F32),
        grid=(M // tm,),
        in_specs=[pl.BlockSpec((tm, K), lambda i: (i, 0)),
                  pl.BlockSpec((K, N), lambda i: (0, 0)),
                  pl.BlockSpec((tm, N), lambda i: (i, 0))],
        out_specs=pl.BlockSpec((tm, N), lambda i: (i, 0)),
        compiler_params=_cparams(("parallel",)),
        name="matmul_residual",
    )(a, w, res)


def _merge_mm_res_kernel(o1, o2, o3, l1, l2, l3, w_ref, r_ref, out_ref):
    a1, a2, a3 = l1[...], l2[...], l3[...]
    m = jnp.maximum(jnp.maximum(a1, a2), a3)
    e1, e2, e3 = jnp.exp(a1 - m), jnp.exp(a2 - m), jnp.exp(a3 - m)
    den = e1 + e2 + e3
    a = (e1 / den) * o1[...] + (e2 / den) * o2[...] + (e3 / den) * o3[...]
    out_ref[...] = r_ref[...] + _dot(a.astype(BF16), w_ref[...])


def merge_matmul_residual(os_, ls_, w, res, *, tm):
    M, K = os_[0].shape
    N = w.shape[1]
    row = pl.BlockSpec((tm, K), lambda i: (i, 0))
    return pl.pallas_call(
        _merge_mm_res_kernel,
        out_shape=jax.ShapeDtypeStruct((M, N), F32),
        grid=(M // tm,),
        in_specs=[row] * 6 + [pl.BlockSpec((K, N), lambda i: (0, 0)),
                              pl.BlockSpec((tm, N), lambda i: (i, 0))],
        out_specs=pl.BlockSpec((tm, N), lambda i: (i, 0)),
        compiler_params=_cparams(("parallel",)),
        name="merge_matmul_residual",
    )(*os_, *ls_, w, res)


def _rmsnorm_kernel(x_ref, g_ref, o_ref):
    o_ref[...] = _rms_norm_rows(x_ref[...], g_ref[...])


def rmsnorm_rows(x, g, *, tm):
    M, D = x.shape
    return pl.pallas_call(
        _rmsnorm_kernel,
        out_shape=jax.ShapeDtypeStruct((M, D), F32),
        grid=(M // tm,),
        in_specs=[pl.BlockSpec((tm, D), lambda i: (i, 0)), pl.BlockSpec((1, D), lambda i: (0, 0))],
        out_specs=pl.BlockSpec((tm, D), lambda i: (i, 0)),
        compiler_params=_cparams(("parallel",)),
        name="rmsnorm",
    )(x, g.reshape(1, D))


def _online_update(carry, s, mask, v):
    m_i, l_i, acc = carry
    s = jnp.where(mask, s, NEG)
    m_new = jnp.maximum(m_i, jnp.max(s, axis=-1, keepdims=True))
    alpha = jnp.exp(m_i - m_new)
    p = jnp.where(mask, jnp.exp(s - m_new), 0.0)
    return (m_new, alpha * l_i + jnp.sum(p, axis=-1, keepdims=True), alpha * acc + _dot(p.astype(BF16), v))


def _online_init(rows, width):
    return (jnp.full((rows, 1), NEG, F32), jnp.zeros((rows, 1), F32), jnp.zeros((rows, width), F32))


def _attend_heads(q, segments, n_heads):
    nq = q.shape[0]
    outs, lses = [], []
    for h in range(n_heads):
        qh = q[:, h * HEAD_DIM:(h + 1) * HEAD_DIM].astype(BF16)
        carry = _online_init(nq, HEAD_DIM)
        for k_rows, v_rows, mask in segments:
            carry = _online_update(carry, _dot_nt(qh, k_rows(h)) * SCALE, mask, v_rows(h))
        m, den, acc = carry
        outs.append(acc / den)
        lses.append(jnp.broadcast_to(m + jnp.log(den), (nq, HEAD_DIM)))
    return jnp.concatenate(outs, axis=1), jnp.concatenate(lses, axis=1)


def _pad_rows(x, n):
    return jnp.concatenate([x, jnp.zeros((n - x.shape[0], x.shape[1]), x.dtype)], axis=0)


def _band_attn_kernel(q_ref, kp_ref, kc_ref, vp_ref, vc_ref, o_ref, l_ref, *, nw):
    n = pl.program_id(2)
    kp, kc, vp, vc = kp_ref[...], kc_ref[...], vp_ref[...], vc_ref[...]
    i = lax.broadcasted_iota(jnp.int32, (A_BLOCK, 2 * A_BLOCK), 0)
    j = lax.broadcasted_iota(jnp.int32, (A_BLOCK, 2 * A_BLOCK), 1)
    rel = A_BLOCK + i - j
    mask = (rel >= 0) & (rel <= nw) & ((j >= A_BLOCK) | (n > 0))

    def rows(prev, cur):
        return lambda h: jnp.concatenate([prev[:, h * HEAD_DIM:(h + 1) * HEAD_DIM],
                                          cur[:, h * HEAD_DIM:(h + 1) * HEAD_DIM]], axis=0).astype(BF16)

    o, l = _attend_heads(q_ref[...], [(rows(kp, kc), rows(vp, vc), mask)], A_HEADS)
    o_ref[...] = o
    l_ref[...] = l


def band_attention(proj, g, window, dil, B, S):
    T, ncol = proj.shape
    L = S // dil
    nb = L // A_BLOCK
    hw = A_HEADS * HEAD_DIM
    per_res = ncol // hw
    view = proj.reshape(T // dil, dil * ncol)

    def spec(c, prev):
        def imap(b, r, n):
            nn = jnp.maximum(n - 1, 0) if prev else n
            return (b * nb + nn, r * per_res + g * 3 + c)
        return pl.BlockSpec((A_BLOCK, hw), imap)

    out_spec = pl.BlockSpec((A_BLOCK, hw), lambda b, r, n: (b * nb + n, r))
    o, l = pl.pallas_call(
        functools.partial(_band_attn_kernel, nw=window // dil),
        out_shape=[jax.ShapeDtypeStruct((T // dil, dil * hw), F32)] * 2,
        grid=(B, dil, nb),
        in_specs=[spec(0, False), spec(1, True), spec(1, False), spec(2, True), spec(2, False)],
        out_specs=[out_spec, out_spec],
        compiler_params=_cparams(("parallel", "parallel", "arbitrary")),
        name=f"band_attention_d{dil}",
    )(view, view, view, view, view)
    return o.reshape(T, hw), l.reshape(T, hw)


def _a_step_kernel(q_ref, k_ref, v_ref, c_ref, o_any, l_any, o_ref, l_ref, s_ref, *, win, dil):
    del o_any, l_any
    W = c_ref.shape[1]
    Sd = q_ref.shape[0]
    hw = A_HEADS * HEAD_DIM
    knew, vnew = k_ref[...], v_ref[...]
    s_ref[0, pl.ds(0, W - Sd), :] = c_ref[0, pl.ds(Sd, W - Sd), :]
    s_ref[0, pl.ds(W - Sd, Sd), pl.ds(0, hw)] = knew
    s_ref[0, pl.ds(W - Sd, Sd), pl.ds(hw, hw)] = vnew

    def on_stride(dist):
        return (dist >= 0) & (dist <= win) & (jnp.bitwise_and(dist, dil - 1) == 0)

    dist_buf = (W + lax.broadcasted_iota(jnp.int32, (Sd, W), 0)) - lax.broadcasted_iota(jnp.int32, (Sd, W), 1)
    ncol = lax.broadcasted_iota(jnp.int32, (Sd, LANES), 1)
    dist_new = lax.broadcasted_iota(jnp.int32, (Sd, LANES), 0) - ncol
    knew_p, vnew_p = _pad_rows(knew, LANES), _pad_rows(vnew, LANES)
    buf_rows = lambda off: lambda h: c_ref[0, :, pl.ds(off + h * HEAD_DIM, HEAD_DIM)].astype(BF16)
    new_rows = lambda x: lambda h: x[:, h * HEAD_DIM:(h + 1) * HEAD_DIM].astype(BF16)
    segments = [(buf_rows(0), buf_rows(hw), on_stride(dist_buf)),
                (new_rows(knew_p), new_rows(vnew_p), on_stride(dist_new) & (ncol < Sd))]
    o, l = _attend_heads(q_ref[...], segments, A_HEADS)
    o_ref[...] = o
    l_ref[...] = l


def a_step(proj, g, window, dil, cache, o_full, l_full, Tp, Bd, Sd):
    assert dil & (dil - 1) == 0
    T = proj.shape[0]
    W = cache.shape[1]
    hw = A_HEADS * HEAD_DIM
    r0 = Tp // Sd
    col = lambda c: pl.BlockSpec((Sd, hw), lambda b: (r0 + b, g * 3 + c))
    any_spec = pl.BlockSpec(memory_space=pl.ANY)
    row_out = pl.BlockSpec((Sd, hw), lambda b: (r0 + b, 0))
    return pl.pallas_call(
        functools.partial(_a_step_kernel, win=window, dil=dil),
        out_shape=[jax.ShapeDtypeStruct((T, hw), F32), jax.ShapeDtypeStruct((T, hw), F32),
                   jax.ShapeDtypeStruct(cache.shape, F32)],
        grid=(Bd,),
        in_specs=[col(0), col(1), col(2), pl.BlockSpec((1, W, 2 * hw), lambda b: (b, 0, 0)), any_spec, any_spec],
        out_specs=[row_out, row_out, pl.BlockSpec((1, W, 2 * hw), lambda b: (b, 0, 0))],
        input_output_aliases={4: 0, 5: 1},
        compiler_params=_cparams(("parallel",)),
        name=f"a_step_d{dil}",
    )(proj, proj, proj, cache, o_full, l_full)


def _topk_cols(s, k, payload=None):
    N = s.shape[0]
    iota = lax.broadcasted_iota(jnp.int32, s.shape, 0).astype(F32)
    vals, picks = [], []
    for _ in range(k):
        m = jnp.max(s, axis=0, keepdims=True)
        idx = jnp.min(jnp.where(s == m, iota, float(N)), axis=0, keepdims=True)
        sel = iota == idx
        vals.append(m)
        if payload is None:
            picks.append(idx)
        else:
            picks.append(jnp.max(jnp.where(sel, payload, -1.0), axis=0, keepdims=True))
        s = jnp.where(sel, -jnp.inf, s)
    return vals, picks


def _topk_mask_cols(s, k):
    N = s.shape[0]
    iota = lax.broadcasted_iota(jnp.int32, s.shape, 0).astype(F32)
    chosen = jnp.zeros(s.shape, F32)
    for _ in range(k):
        m = jnp.max(s, axis=0, keepdims=True)
        idx = jnp.min(jnp.where(s == m, iota, float(N)), axis=0, keepdims=True)
        sel = iota == idx
        chosen = jnp.where(sel, 1.0, chosen)
        s = jnp.where(sel, -jnp.inf, s)
    return chosen


def _peer_route_kernel(x_ref, g_ref, wqh_ref, wql_ref, skh_ref, skl_ref,
                       hn_ref, e_ref, w_ref, xh_ref, xl_ref, et_ref, wt_ref):
    h = pl.program_id(1)

    @pl.when(h == 0)
    def _():
        xn = _rms_norm_rows(x_ref[...], g_ref[...])
        hi, lo = _split2(xn)
        xh_ref[...] = hi
        xl_ref[...] = lo
        hn_ref[...] = hi

    xh, xl = xh_ref[...], xl_ref[...]
    wqh, wql = wqh_ref[...], wql_ref[...]
    q = _dot(xh, wqh) + _dot(xh, wql) + _dot(xl, wqh)
    tops = []
    for z in range(2):
        qh, ql = _split2(q[:, z * PEER_KEYS:(z + 1) * PEER_KEYS])
        skh, skl = skh_ref[z], skl_ref[z]
        st = _dot_nt(skh, qh) + _dot_nt(skl, qh) + _dot_nt(skh, ql)
        tops.append(_topk_cols(st, PEER_TOPK))
    (v1, i1), (v2, i2) = tops
    K = PEER_TOPK
    tm = v1[0].shape[1]
    v2_lo, v2_hi = jnp.concatenate(v2[:8], axis=0), jnp.concatenate(v2[8:], axis=0)
    i2_lo, i2_hi = jnp.concatenate(i2[:8], axis=0), jnp.concatenate(i2[8:], axis=0)
    b_iota = lax.broadcasted_iota(jnp.int32, (8, tm), 0)
    e_of = lambda a, i2rows: i1[a] * float(PEER_KEYS) + i2rows
    cand = [v1[0] + v2_lo, v1[0] + v2_hi, v1[1] + v2_lo]
    cand_e = [e_of(0, i2_lo), e_of(0, i2_hi), e_of(1, i2_lo)]
    for a in range(2, 8):
        cand.append(jnp.where(b_iota < K // (a + 1), v1[a] + v2_lo, -jnp.inf))
        cand_e.append(e_of(a, i2_lo))
    cand.append(jnp.concatenate(v1[8:], axis=0) + v2[0])
    cand_e.append(jnp.concatenate(i1[8:], axis=0) * float(PEER_KEYS) + i2[0])
    best, best_e = _topk_cols(jnp.concatenate(cand, axis=0), K, payload=jnp.concatenate(cand_e, axis=0))
    bs = jnp.concatenate(best, axis=0)
    be = jnp.concatenate(best_e, axis=0)
    ex = jnp.exp(bs - best[0])
    gw = ex / jnp.sum(ex, axis=0, keepdims=True)
    row = pl.multiple_of(h * PEER_TOPK, PEER_TOPK)
    et_ref[pl.ds(row, PEER_TOPK), :] = be
    wt_ref[pl.ds(row, PEER_TOPK), :] = gw

    @pl.when(h == PEER_HEADS - 1)
    def _():
        e_ref[...] = et_ref[...].T.astype(jnp.int32)
        w_ref[...] = wt_ref[...].T


def peer_route(x, g, wq_hi, wq_lo, sk_hi, sk_lo, *, tm):
    T, D = x.shape
    nslot = PEER_HEADS * PEER_TOPK
    return pl.pallas_call(
        _peer_route_kernel,
        out_shape=[jax.ShapeDtypeStruct((T, D), BF16),
                   jax.ShapeDtypeStruct((T, nslot), jnp.int32),
                   jax.ShapeDtypeStruct((T, nslot), F32)],
        grid=(T // tm, PEER_HEADS),
        in_specs=[pl.BlockSpec((tm, D), lambda i, h: (i, 0)),
                  pl.BlockSpec((1, D), lambda i, h: (0, 0)),
                  pl.BlockSpec((D, PEER_QDIM), lambda i, h: (0, h)),
                  pl.BlockSpec((D, PEER_QDIM), lambda i, h: (0, h)),
                  pl.BlockSpec((2, PEER_KEYS, PEER_QDIM // 2), lambda i, h: (0, 0, 0)),
                  pl.BlockSpec((2, PEER_KEYS, PEER_QDIM // 2), lambda i, h: (0, 0, 0))],
        out_specs=[pl.BlockSpec((tm, D), lambda i, h: (i, 0)),
                   pl.BlockSpec((tm, nslot), lambda i, h: (i, 0)),
                   pl.BlockSpec((tm, nslot), lambda i, h: (i, 0))],
        scratch_shapes=[pltpu.VMEM((tm, D), BF16), pltpu.VMEM((tm, D), BF16),
                        pltpu.VMEM((nslot, tm), F32), pltpu.VMEM((nslot, tm), F32)],
        compiler_params=_cparams(("parallel", "arbitrary")),
        name="peer_route",
    )(x, g.reshape(1, D), wq_hi, wq_lo, sk_hi, sk_lo)


def _peer_expert_kernel(hn_ref, e_ref, w_ref, x_ref, u_ref, v_ref, o_ref, gs_ref, acc_ref, *, tm, rows_per_chunk):
    c = pl.program_id(1)
    nk = PEER_KEYS

    @pl.when(c == 0)
    def _():
        acc_ref[...] = jnp.zeros_like(acc_ref)
        sub = lax.broadcasted_iota(jnp.int32, (nk, nk), 0)

        def body(t8, carry):
            r0 = pl.multiple_of(t8 * 8, 8)
            e8 = e_ref[pl.ds(r0, 8), :]
            w8 = w_ref[pl.ds(r0, 8), :]
            for r in range(8):
                e, w = e8[r:r + 1], w8[r:r + 1]
                i1 = jnp.right_shift(e, 7)
                i2 = jnp.bitwise_and(e, nk - 1)
                pt = jnp.where(sub == i1, w, 0.0).astype(BF16)
                qt = jnp.where(sub == i2, 1.0, 0.0).astype(BF16)
                row = pl.multiple_of((r0 + r) * G_PITCH, 8)
                gs_ref[pl.ds(row, nk), :] = _dot_nt(pt, qt)
            return carry

        lax.fori_loop(0, tm // 8, body, 0)

    act = jax.nn.gelu(_dot_nt(hn_ref[...], u_ref[...]))
    parts = []
    for r in range(rows_per_chunk):
        gr = gs_ref[pl.ds(c * rows_per_chunk + r, tm, stride=G_PITCH), :]
        parts.append((act[:, r * nk:(r + 1) * nk] * gr).astype(BF16))
    acc_ref[...] += _dot(jnp.concatenate(parts, axis=1), v_ref[...])

    @pl.when(c == pl.num_programs(1) - 1)
    def _():
        o_ref[...] = x_ref[...] + acc_ref[...]


def peer_experts(hn, e, w, x, u, v, *, tm, chunk):
    T, D = x.shape
    n_exp = u.shape[0]
    nslot = e.shape[1]
    rows_per_chunk = chunk // PEER_KEYS
    return pl.pallas_call(
        functools.partial(_peer_expert_kernel, tm=tm, rows_per_chunk=rows_per_chunk),
        out_shape=jax.ShapeDtypeStruct((T, D), F32),
        grid=(T // tm, n_exp // chunk),
        in_specs=[pl.BlockSpec((tm, D), lambda i, c: (i, 0)),
                  pl.BlockSpec((tm, nslot), lambda i, c: (i, 0)),
                  pl.BlockSpec((tm, nslot), lambda i, c: (i, 0)),
                  pl.BlockSpec((tm, D), lambda i, c: (i, 0)),
                  pl.BlockSpec((chunk, D), lambda i, c: (c, 0)),
                  pl.BlockSpec((chunk, D), lambda i, c: (c, 0))],
        out_specs=pl.BlockSpec((tm, D), lambda i, c: (i, 0)),
        scratch_shapes=[pltpu.VMEM((tm * G_PITCH, PEER_KEYS), F32), pltpu.VMEM((tm, D), F32)],
        compiler_params=_cparams(("parallel", "arbitrary")),
        name="peer_experts",
    )(hn, e, w, x, u, v)


def peer_layer(x, g, w_q, subkeys, u, v, *, tm_route, tm_exp, chunk):
    wq_hi, wq_lo = _split2(w_q)
    sk_hi, sk_lo = _split2(subkeys)
    hn, e, w = peer_route(x, g, wq_hi, wq_lo, sk_hi, sk_lo, tm=tm_route)
    return peer_experts(hn, e, w, x, u.astype(BF16), v.astype(BF16), tm=tm_exp, chunk=chunk)


def _compress_core(get_x, w1k_ref, w1v_ref, pe_ref, w1f_ref, w2k_ref, w2v_ref, cos_ref, sa_ref, sb_ref,
                   kc_ref, vc_ref, acc_ref, n_cmp):
    nch = acc_ref.shape[0]
    for l in range(CMP_STRIDE):
        x = get_x(l)
        yk = _dot(x[:, :KV_W].astype(BF16), w1k_ref[l])
        yv = _dot(x[:, KV_W:].astype(BF16), w1v_ref[l])
        if l == 0:
            acc_ref[:, pl.ds(0, 2 * KV_W)] = yk
            acc_ref[:, pl.ds(2 * KV_W, 2 * KV_W)] = yv
        else:
            acc_ref[:, pl.ds(0, 2 * KV_W)] += yk
            acc_ref[:, pl.ds(2 * KV_W, 2 * KV_W)] += yv
    keep = lax.broadcasted_iota(jnp.int32, (nch, KV_W), 0) < n_cmp
    for z, (w2_ref, out_ref) in enumerate(((w2k_ref, kc_ref), (w2v_ref, vc_ref))):
        ph, pl_ = _split2(pe_ref[z])
        wh, wl = _split2(w1f_ref[z])
        pe_t = _dot(ph, wh) + _dot(ph, wl) + _dot(pl_, wh)
        pe_t = jnp.concatenate([pe_t] * B_KV_HEADS, axis=1)
        lo = acc_ref[:, pl.ds(z * 2 * KV_W, KV_W)]
        hi = acc_ref[:, pl.ds(z * 2 * KV_W + KV_W, KV_W)]
        hid = jax.nn.gelu(lo + pltpu.roll(hi, nch - 1, 0) + pe_t)
        out = _dot(hid.astype(BF16), w2_ref[...])
        if z == 0:
            out = _rope_lanes(out, cos_ref[...], sa_ref[...], sb_ref[...])
        out_ref[0] = jnp.where(keep, out, 0.0)


def _compress_prompt_kernel(x0, x1, x2, x3, *rest, n_cmp):
    slabs = (x0, x1, x2, x3)
    nch = rest[-1].shape[0]

    def get_x(l):
        return jnp.concatenate([s[pl.ds(l, nch, stride=CMP_STRIDE), :] for s in slabs], axis=1)

    _compress_core(get_x, *rest, n_cmp=n_cmp)


def _compress_weights(w_cmp1, w_cmp2, pe_cmp, n_chunk):
    eye = jnp.eye(B_KV_HEADS, dtype=F32)
    kron = lambda w: jnp.kron(eye, w)
    w1 = []
    for z in range(2):
        lo = jnp.stack([kron(w_cmp1[z, l]) for l in range(CMP_STRIDE)])
        hi = jnp.stack([kron(w_cmp1[z, CMP_STRIDE + l]) for l in range(CMP_STRIDE)])
        w1.append(jnp.concatenate([lo, hi], axis=2).astype(BF16))
    w2 = [kron(w_cmp2[z]).astype(BF16) for z in range(2)]
    pe = pe_cmp.reshape(2, 1, CMP_LEN * HEAD_DIM)
    w1f = w_cmp1.reshape(2, CMP_LEN * HEAD_DIM, -1)
    end_pos = jnp.arange(n_chunk, dtype=jnp.int32) * CMP_STRIDE + (CMP_LEN - 1)
    tabs = _rope_tables(end_pos, KV_W)
    return (w1[0], w1[1], pe, w1f, w2[0], w2[1]) + tabs


def _const_specs(arrs):
    return [pl.BlockSpec(a.shape, (lambda *i, nd=a.ndim: (0,) * nd), pipeline_mode=pl.Buffered(1)) for a in arrs]


def compress_prompt(kv, cw, B, S):
    nch = S // CMP_STRIDE
    slab = lambda s: pl.BlockSpec((S, LANES), lambda b: (b, s), pipeline_mode=pl.Buffered(1))
    out = pl.BlockSpec((1, nch, KV_W), lambda b: (b, 0, 0))
    return pl.pallas_call(
        functools.partial(_compress_prompt_kernel, n_cmp=nch - 1),
        out_shape=[jax.ShapeDtypeStruct((B, nch, KV_W), F32)] * 2,
        grid=(B,),
        in_specs=[slab(s) for s in range(4)] + _const_specs(cw),
        out_specs=[out, out],
        scratch_shapes=[pltpu.VMEM((nch, 4 * KV_W), F32)],
        compiler_params=_cparams(("parallel",)),
        name="compress_prompt",
    )(kv, kv, kv, kv, *cw)


def _compress_sample_kernel(pt_ref, *refs, n_cmp, pages_per_step):
    n_pg = 4 * pages_per_step
    pages, rest, x_ref = refs[:n_pg], refs[n_pg:-1], refs[-1]
    j = pl.program_id(1)
    ch_per_page = pages[0].shape[1] // CMP_STRIDE
    for p in range(pages_per_step):
        r0 = pl.multiple_of((j * pages_per_step + p) * ch_per_page, ch_per_page)
        for s in range(4):
            for l in range(CMP_STRIDE):
                x_ref[l, pl.ds(r0, ch_per_page), pl.ds(s * LANES, LANES)] = (
                    pages[p * 4 + s][0, pl.ds(l, ch_per_page, stride=CMP_STRIDE), :])

    @pl.when(j == pl.num_programs(1) - 1)
    def _():
        _compress_core(lambda l: x_ref[l], *rest, n_cmp=n_cmp)


def compress_sample(cache, page_table, cw, *, pages_per_step):
    Bd, n_pages = page_table.shape
    page = cache.shape[1]
    nch = n_pages * page // CMP_STRIDE
    P = pages_per_step

    def pspec(p, s):
        return pl.BlockSpec((1, page, LANES), lambda b, j, pt: (pt[b, j * P + p], 0, s))

    out = pl.BlockSpec((1, nch, KV_W), lambda b, j, pt: (b, 0, 0))
    return pl.pallas_call(
        functools.partial(_compress_sample_kernel, n_cmp=nch - 1, pages_per_step=P),
        out_shape=[jax.ShapeDtypeStruct((Bd, nch, KV_W), F32)] * 2,
        grid_spec=pltpu.PrefetchScalarGridSpec(
            num_scalar_prefetch=1,
            grid=(Bd, n_pages // P),
            in_specs=[pspec(p, s) for p in range(P) for s in range(4)] + _const_specs(cw),
            out_specs=[out, out],
            scratch_shapes=[pltpu.VMEM((nch, 4 * KV_W), F32),
                            pltpu.VMEM((CMP_STRIDE, nch, 2 * KV_W), F32)],
        ),
        compiler_params=_cparams(("parallel", "arbitrary")),
        name="compress_sample",
    )(page_table, *([cache] * (4 * P)), *cw)


def _select_blocks(imp, wsel, blk, cur):
    i_hi = imp.astype(BF16)
    r1 = imp - i_hi.astype(F32)
    i_mid = r1.astype(BF16)
    i_lo = (r1 - i_mid.astype(F32)).astype(BF16)
    imp_sel = _dot(i_hi, wsel) + _dot(i_mid, wsel) + _dot(i_lo, wsel)
    forced = (blk == 0) | (blk == cur) | (blk == cur - 1)
    imp_sel = jnp.where(blk > cur, -jnp.inf, jnp.where(forced, jnp.inf, imp_sel))
    return _topk_mask_cols(imp_sel.T, N_SEL).T


def _tile_block_hits(chosen_q, k0, kt_size):
    n_sb = chosen_q.shape[1]
    blk = lax.broadcasted_iota(jnp.int32, (n_sb, kt_size), 0)
    key_blk = jnp.right_shift(k0 + lax.broadcasted_iota(jnp.int32, (n_sb, kt_size), 1), 6)
    expand = jnp.where(blk == key_blk, 1.0, 0.0).astype(BF16)
    return _dot(chosen_q, expand)


def _tile_block_mask(chosen_q, k0, kt_size):
    return _tile_block_hits(chosen_q, k0, kt_size) > 0.5


def _nsa_prompt_kernel(q_ref, gate_ref, kc_ref, vc_ref, ks_ref, vs_ref, kw_ref, vw_ref, wsel_ref,
                       o_ref, *, n_cmp, kt_size):
    n = pl.program_id(1)
    t0 = n * Q_BLOCK
    G, TQ = B_GROUP, Q_BLOCK
    R = G * TQ
    NC = kc_ref.shape[1]
    NSB = wsel_ref.shape[1]
    wsel = wsel_ref[...]
    gt = gate_ref[...]
    row_c = lax.broadcasted_iota(jnp.int32, (R, NC), 0)
    col_c = lax.broadcasted_iota(jnp.int32, (R, NC), 1)
    valid_c = (col_c * CMP_STRIDE + (CMP_LEN - 1) <= t0 + jnp.bitwise_and(row_c, TQ - 1)) & (col_c < n_cmp)
    bias_c = jnp.where(valid_c, 0.0, NEG)
    sees_cmp = t0 + jnp.bitwise_and(lax.broadcasted_iota(jnp.int32, (R, 1), 0), TQ - 1) >= CMP_LEN - 1
    blk = lax.broadcasted_iota(jnp.int32, (TQ, NSB), 1)
    cur = jnp.right_shift(t0 + lax.broadcasted_iota(jnp.int32, (TQ, NSB), 0), 6)
    WK = B_WINDOW + TQ
    w0 = pl.multiple_of(jnp.maximum(t0 - B_WINDOW, 0), TQ)
    wpos = w0 + lax.broadcasted_iota(jnp.int32, (R, WK), 1)
    rel = t0 + jnp.bitwise_and(lax.broadcasted_iota(jnp.int32, (R, WK), 0), TQ - 1) - wpos
    bias_w = jnp.where((rel >= 0) & (rel <= B_WINDOW), 0.0, NEG)
    n_kt = (t0 + TQ + kt_size - 1) // kt_size
    k_last = pl.multiple_of((n_kt - 1) * kt_size, kt_size)
    causal_last = jnp.where(k_last + lax.broadcasted_iota(jnp.int32, (TQ, kt_size), 1)
                            <= t0 + lax.broadcasted_iota(jnp.int32, (TQ, kt_size), 0), 0.0, NEG)
    ones_col = jnp.where(lax.broadcasted_iota(jnp.int32, (kt_size, HEAD_DIM), 1) == 0, 1.0, 0.0).astype(BF16)

    outs = []
    for h in range(B_KV_HEADS):
        hs = pl.ds(h * HEAD_DIM, HEAD_DIM)
        qs = jnp.concatenate([q_ref[:, pl.ds((h * G + g) * HEAD_DIM, HEAD_DIM)] for g in range(G)], axis=0)
        qs = qs * SCALE
        qb = qs.astype(BF16)

        s_c = _dot_nt3(qs, kc_ref[0, :, hs]) + bias_c
        e_c = jnp.exp(s_c - jnp.max(s_c, axis=-1, keepdims=True))
        p_c = e_c * jnp.where(sees_cmp, 1.0 / jnp.sum(e_c, axis=-1, keepdims=True), 0.0)
        o_c = _dot(p_c.astype(BF16), vc_ref[0, :, hs].astype(BF16))
        imp = p_c[0:TQ]
        for g in range(1, G):
            imp = imp + p_c[g * TQ:(g + 1) * TQ]
        chosen_q = _select_blocks(imp, wsel, blk, cur).astype(BF16)

        def sel_tile(k0, carry, extra_bias):
            m_i, acc = carry
            bias = (_tile_block_hits(chosen_q, k0, kt_size) - 1.0) * (-NEG)
            if extra_bias is not None:
                bias = bias + extra_bias
            s = _dot_nt(qb, ks_ref[pl.ds(k0, kt_size), hs]).reshape(G, TQ, kt_size) + bias[None]
            s = s.reshape(R, kt_size)
            m_new = jnp.maximum(m_i, jnp.max(s, axis=-1, keepdims=True))
            p = jnp.exp(s - m_new).astype(BF16)
            v1 = jnp.concatenate([vs_ref[pl.ds(k0, kt_size), hs], ones_col], axis=1)
            return m_new, jnp.exp(m_i - m_new) * acc + _dot(p, v1)

        def sel_body(kt, carry):
            return sel_tile(pl.multiple_of(kt * kt_size, kt_size), carry, None)

        carry = lax.fori_loop(0, n_kt - 1, sel_body, (jnp.full((R, 1), NEG, F32), jnp.zeros((R, 2 * HEAD_DIM), F32)))
        _, acc_s = sel_tile(k_last, carry, causal_last)
        o_s = acc_s[:, :HEAD_DIM] / acc_s[:, HEAD_DIM:HEAD_DIM + 1]

        s_w = _dot_nt(qb, kw_ref[pl.ds(w0, WK), hs]) + bias_w
        e_w = jnp.exp(s_w - jnp.max(s_w, axis=-1, keepdims=True))
        p_w = e_w * (1.0 / jnp.sum(e_w, axis=-1, keepdims=True))
        o_w = _dot(p_w.astype(BF16), vw_ref[pl.ds(w0, WK), hs])

        for g in range(G):
            rs = slice(g * TQ, (g + 1) * TQ)
            c = 3 * (h * G + g)
            outs.append(gt[:, c:c + 1] * o_c[rs] + gt[:, c + 1:c + 2] * o_s[rs] + gt[:, c + 2:c + 3] * o_w[rs])
    o_ref[...] = jnp.concatenate(outs, axis=1)


def nsa_prompt(q, gate, kc, vc, kvb, wsel, B, S, *, kt_size):
    T, nq = q.shape
    nqb = S // Q_BLOCK
    NC = kc.shape[1]
    comp = lambda c: pl.BlockSpec((S, KV_W), lambda b, n: (b, c))
    cmp_spec = pl.BlockSpec((1, NC, KV_W), lambda b, n: (b, 0, 0))
    return pl.pallas_call(
        functools.partial(_nsa_prompt_kernel, n_cmp=S // CMP_STRIDE - 1, kt_size=kt_size),
        out_shape=jax.ShapeDtypeStruct((T, nq), F32),
        grid=(B, nqb),
        in_specs=[pl.BlockSpec((Q_BLOCK, nq), lambda b, n: (b * nqb + n, 0)),
                  pl.BlockSpec((Q_BLOCK, LANES), lambda b, n: (b * nqb + n, 0)),
                  cmp_spec, cmp_spec, comp(2), comp(3), comp(4), comp(5),
                  pl.BlockSpec(wsel.shape, lambda b, n: (0, 0))],
        out_specs=pl.BlockSpec((Q_BLOCK, nq), lambda b, n: (b * nqb + n, 0)),
        compiler_params=_cparams(("parallel", "arbitrary")),
        name="nsa_prompt",
    )(q, gate, kc, vc, kvb, kvb, kvb, kvb, wsel)


def _nsa_step_kernel(pt_ref, q_ref, gate_ref, kc_ref, vc_ref, new_ref, win_ref, wsel_ref, *rest,
                     past, n_cmp, pages_per_step):
    P = pages_per_step
    pages = rest[:P]
    o_any, o_ref, st_ref, a_ref, ch_ref, m_ref, l_ref, acc_ref, oc_ref = rest[P:]
    del o_any, pt_ref
    j = pl.program_id(1)
    G, KVH = B_GROUP, B_KV_HEADS
    Sd = q_ref.shape[0]
    R = B_HEADS * Sd
    page = pages[0].shape[1]
    kt = P * page
    row_head = lambda shape: lax.broadcasted_iota(jnp.int32, shape, 0) // (G * Sd)
    own = lambda width: row_head((R, width)) == lax.broadcasted_iota(jnp.int32, (R, width), 1) // HEAD_DIM

    def diag(x):
        rows = G * Sd
        return jnp.concatenate([x[h * rows:(h + 1) * rows, h * HEAD_DIM:(h + 1) * HEAD_DIM]
                                for h in range(KVH)], axis=0)

    @pl.when(j == 0)
    def _():
        q = q_ref[...]
        a = jnp.concatenate([q[:, hd * HEAD_DIM:(hd + 1) * HEAD_DIM] for hd in range(B_HEADS)], axis=0)
        a = jnp.where(own(KV_W), jnp.concatenate([a] * KVH, axis=1), 0.0)
        a_ref[...] = a.astype(BF16)
        NC = kc_ref.shape[1]
        valid_c = lax.broadcasted_iota(jnp.int32, (R, NC), 1) < n_cmp
        p_c = _softmax_rows(_dot_nt3(a, kc_ref[0]) * SCALE, valid_c)
        oc_ref[...] = diag(_dot(p_c.astype(BF16), vc_ref[0].astype(BF16)))
        pooled = jnp.sum(p_c.reshape(KVH, G, Sd, NC), axis=1, keepdims=True)
        imp = jnp.broadcast_to(pooled, (KVH, G, Sd, NC)).reshape(R, NC)
        NSB = wsel_ref.shape[1]
        blk = lax.broadcasted_iota(jnp.int32, (R, NSB), 1)
        srow = jnp.bitwise_and(lax.broadcasted_iota(jnp.int32, (R, NSB), 0), Sd - 1)
        cur = jnp.right_shift(past + srow, 6)
        ch_ref[...] = _select_blocks(imp, wsel_ref[...], blk, cur).astype(BF16)
        m_ref[...] = jnp.full(m_ref.shape, NEG, F32)
        l_ref[...] = jnp.zeros(l_ref.shape, F32)
        acc_ref[...] = jnp.zeros(acc_ref.shape, F32)

    a = a_ref[...]
    ks = jnp.concatenate([pg[0, :, pl.ds(0, KV_W)] for pg in pages], axis=0).astype(BF16)
    vs = jnp.concatenate([pg[0, :, pl.ds(KV_W, KV_W)] for pg in pages], axis=0).astype(BF16)
    mk = _tile_block_mask(ch_ref[...], j * kt, kt)
    m_n, l_n, acc_n = _online_update((m_ref[...], l_ref[...], acc_ref[...]), _dot_nt(a, ks) * SCALE, mk, vs)
    m_ref[...] = m_n
    l_ref[...] = l_n
    acc_ref[...] = acc_n

    @pl.when(j == pl.num_programs(1) - 1)
    def _():
        new = _pad_rows(new_ref[...], LANES)
        comp = lambda c: new[:, c * KV_W:(c + 1) * KV_W].astype(BF16)
        srow = jnp.bitwise_and(lax.broadcasted_iota(jnp.int32, (R, LANES), 0), Sd - 1)
        scol = lax.broadcasted_iota(jnp.int32, (R, LANES), 1)
        causal_new = (scol <= srow) & (scol < Sd)
        nblk = past // SEL_BLOCK
        in_new = ch_ref[:, nblk:nblk + 1].astype(F32) > 0.5
        _, l_s, acc_s = _online_update((m_n, l_n, acc_n), _dot_nt(a, comp(2)) * SCALE, in_new & causal_new, comp(3))
        o_s = diag(acc_s) / jnp.where(l_s > 0, l_s, 1.0)

        W = win_ref.shape[1]
        wrow = jnp.bitwise_and(lax.broadcasted_iota(jnp.int32, (R, W), 0), Sd - 1)
        rel = W + wrow - lax.broadcasted_iota(jnp.int32, (R, W), 1)
        carry = _online_update(_online_init(R, KV_W), _dot_nt(a, win_ref[0, :, pl.ds(0, KV_W)].astype(BF16)) * SCALE,
                               (rel >= 0) & (rel <= B_WINDOW), win_ref[0, :, pl.ds(KV_W, KV_W)].astype(BF16))
        _, l_w, acc_w = _online_update(carry, _dot_nt(a, comp(4)) * SCALE, causal_new, comp(5))
        o_w = diag(acc_w) / jnp.where(l_w > 0, l_w, 1.0)
        o_c = oc_ref[...]
        gt = gate_ref[...]
        outs = []
        for hd in range(B_HEADS):
            rs = slice(hd * Sd, (hd + 1) * Sd)
            c = 3 * hd
            outs.append(gt[:, c:c + 1] * o_c[rs] + gt[:, c + 1:c + 2] * o_s[rs] + gt[:, c + 2:c + 3] * o_w[rs])
        o_ref[...] = jnp.concatenate(outs, axis=1)
        st_ref[0, pl.ds(0, W - Sd), :] = win_ref[0, pl.ds(Sd, W - Sd), :]
        st_ref[0, pl.ds(W - Sd, Sd), :] = new_ref[:, pl.ds(4 * KV_W, 2 * KV_W)]


def nsa_step(q, gate, kc, vc, kv, win, cache, page_table, wsel, o_full, Tp, Bd, Sd, *, pages_per_step):
    T, nq = q.shape
    n_pages = page_table.shape[1]
    page = cache.shape[1]
    past = n_pages * page
    W = win.shape[1]
    NC = kc.shape[1]
    P = pages_per_step
    r0 = Tp // Sd
    R = B_HEADS * Sd
    rows = lambda w: pl.BlockSpec((Sd, w), lambda b, j, pt: (r0 + b, 0))
    per_b = lambda s1, s2: pl.BlockSpec((1, s1, s2), lambda b, j, pt: (b, 0, 0))
    pspec = lambda p: pl.BlockSpec((1, page, 2 * KV_W), lambda b, j, pt: (pt[b, j * P + p], 0, 1))
    return pl.pallas_call(
        functools.partial(_nsa_step_kernel, past=past, n_cmp=past // CMP_STRIDE - 1, pages_per_step=P),
        out_shape=[jax.ShapeDtypeStruct((T, nq), F32), jax.ShapeDtypeStruct(win.shape, F32)],
        grid_spec=pltpu.PrefetchScalarGridSpec(
            num_scalar_prefetch=1,
            grid=(Bd, n_pages // P),
            in_specs=[rows(nq), rows(LANES), per_b(NC, KV_W), per_b(NC, KV_W), rows(kv.shape[1]),
                      per_b(W, 2 * KV_W), pl.BlockSpec(wsel.shape, lambda b, j, pt: (0, 0))]
                     + [pspec(p) for p in range(P)] + [pl.BlockSpec(memory_space=pl.ANY)],
            out_specs=[rows(nq), per_b(W, 2 * KV_W)],
            scratch_shapes=[pltpu.VMEM((R, KV_W), BF16), pltpu.VMEM((R, wsel.shape[1]), BF16),
                            pltpu.VMEM((R, 1), F32), pltpu.VMEM((R, 1), F32), pltpu.VMEM((R, KV_W), F32),
                            pltpu.VMEM((R, HEAD_DIM), F32)],
        ),
        input_output_aliases={8 + P: 0},
        compiler_params=_cparams(("parallel", "arbitrary")),
        name="nsa_step",
    )(page_table, q, gate, kc, vc, kv, win, wsel, *([cache] * P), o_full)


def _rope_tables(pos, width):
    half = ROT_DIM // 2
    inv = jnp.exp(-jnp.log(jnp.float32(ROPE_THETA)) * jnp.arange(half, dtype=F32) * (2.0 / ROT_DIM))
    ang = pos.astype(F32)[:, None] * inv[None, :]
    cos, sin = jnp.cos(ang), jnp.sin(ang)
    T = pos.shape[0]
    z8 = jnp.zeros((T, half), F32)
    rest = HEAD_DIM - ROT_DIM
    c = jnp.concatenate([cos, cos, jnp.ones((T, rest), F32)], axis=1)
    sa = jnp.concatenate([-sin, z8, jnp.zeros((T, rest), F32)], axis=1)
    sb = jnp.concatenate([z8, sin, jnp.zeros((T, rest), F32)], axis=1)
    reps = width // HEAD_DIM
    return tuple(jnp.tile(t, (1, reps)) for t in (c, sa, sb))


def _overlap_weights(n_cmp, n_cmp_pad, n_sb, n_sb_pad):
    j = jnp.arange(n_sb)
    cidx = (SEL_BLOCK // CMP_STRIDE) * j[:, None] - 1 + jnp.arange(len(SEL_OVERLAP_W))[None, :]
    wts = jnp.where((cidx >= 0) & (cidx < n_cmp), jnp.asarray(SEL_OVERLAP_W, F32), 0.0)
    onehot = (jnp.clip(cidx, 0, n_cmp - 1)[:, :, None] == jnp.arange(n_cmp_pad)[None, None, :]).astype(F32)
    wsel = jnp.einsum('jo,jon->nj', wts, onehot)
    return jnp.pad(wsel, ((0, 0), (0, n_sb_pad - n_sb))).astype(BF16)


def _pick(n, cands):
    for c in cands:
        if n % c == 0:
            return c
    return n


def _pad_to(n, m):
    return -(-n // m) * m


def kernel(x_prompt, x_sample, cache_a_g1, cache_a_g2, cache_a_g3, cache_b_kv, cache_b_win, page_table, g_mix, g_ffn, w_in_a, w_o_a, g_kv, w_kv_b, w_cmp1, w_cmp2, pe_cmp, w_qg_b, b_gate_b, w_o_b, w_peer_q, peer_subkeys, peer_u, peer_v, g_final):
    B, S, D = x_prompt.shape
    Bd, Sd, _ = x_sample.shape
    Tp, Ts = B * S, Bd * Sd
    T = Tp + Ts
    n_phys, page = cache_b_kv.shape[:2]
    n_pages = page_table.shape[1]
    past = n_pages * page
    hw = A_HEADS * HEAD_DIM
    tm_big = _pick(T, (1280, 640, 256, 128, 64, 32, 16, 8))
    tm_mid = _pick(T, (640, 256, 128, 64, 32, 16, 8))
    tm_peer = _pick(T, (256, 128, 64, 32, 16, 8))

    x = jnp.concatenate([x_prompt.reshape(Tp, D), x_sample.reshape(Ts, D)], axis=0)
    pos_p = jnp.arange(S, dtype=jnp.int32)
    pos_s = past + jnp.arange(Sd, dtype=jnp.int32)
    pos = jnp.concatenate([jnp.tile(pos_p, B), jnp.tile(pos_s, Bd)])
    tab512 = _rope_tables(pos, 512)
    tab256 = tuple(t[:, :KV_W] for t in tab512)

    n_grp = len(A_GROUPS)
    proj = rms_matmul(x, g_mix[0], w_in_a[0].astype(BF16), tm=tm_big, tn=hw, epilogue="rope",
                      flags=jnp.asarray([1, 1, 0] * n_grp, jnp.int32), tables=tab512)
    caches_a = (cache_a_g1, cache_a_g2, cache_a_g3)
    o_list, l_list, a_prompt, a_sample = [], [], [], []
    for g, (win, dil) in enumerate(A_GROUPS):
        o_g, l_g = band_attention(proj, g, win, dil, B, S)
        cache = caches_a[g]
        o_g, l_g, state = a_step(proj, g, win, dil, cache[0].reshape(Bd, cache.shape[2], 2 * hw), o_g, l_g,
                                 Tp, Bd, Sd)
        a_sample.append(state.reshape(cache.shape))
        keep = min(win, S)
        kv_cols = proj[:Tp].reshape(B, S, n_grp, 3 * hw)[:, S - keep:, g, hw:]
        a_prompt.append(kv_cols.reshape(1, B, keep, 2, A_HEADS, HEAD_DIM))
        o_list.append(o_g)
        l_list.append(l_g)
    x = merge_matmul_residual(o_list, l_list, w_o_a[0].astype(BF16), x, tm=tm_mid)
    x = peer_layer(x, g_ffn[0], w_peer_q[0], peer_subkeys[0], peer_u[0], peer_v[0],
                   tm_route=tm_peer, tm_exp=tm_peer, chunk=2048)

    kv, kvb = rms_matmul(x, g_kv, w_kv_b.astype(BF16), tm=tm_big, tn=KV_W, epilogue="rope",
                         flags=jnp.asarray([0, 0, 1, 0, 1, 0], jnp.int32), tables=tab256, emit_bf16=True)
    kv_p = kv[:Tp].reshape(B, S, 6, B_KV_HEADS, HEAD_DIM)
    bkv_prompt = kv_p[:, :, :4]
    bkv_sample = kv[Tp:, :4 * KV_W].reshape(Bd, Sd, 4, B_KV_HEADS, HEAD_DIM)
    keep_w = min(B_WINDOW, S)
    bwin_prompt = kv_p[:, S - keep_w:, 4:]
    cache2d = cache_b_kv.reshape(n_phys, page, 4 * KV_W)
    kc_p, vc_p = compress_prompt(kv, _compress_weights(w_cmp1, w_cmp2, pe_cmp, S // CMP_STRIDE), B, S)
    kc_s, vc_s = compress_sample(cache2d, page_table, _compress_weights(w_cmp1, w_cmp2, pe_cmp, past // CMP_STRIDE),
                                 pages_per_step=4)

    nq = B_HEADS * HEAD_DIM
    w_qg = w_qg_b[0]
    q = rms_matmul(x, g_mix[1], w_qg[:, :nq].astype(BF16), tm=tm_big, tn=512, epilogue="rope",
                   flags=jnp.ones((nq // 512,), jnp.int32), tables=tab512)
    ngate = 3 * B_HEADS
    w_gate = jnp.pad(w_qg[:, nq:], ((0, 0), (0, LANES - ngate))).astype(BF16)
    b_gate = jnp.pad(b_gate_b[0], (0, LANES - ngate))
    gate = rms_matmul(x, g_mix[1], w_gate, tm=tm_big, tn=LANES, epilogue="sigmoid", bias=b_gate)

    n_cmp_p = S // CMP_STRIDE - 1
    wsel_p = _overlap_weights(n_cmp_p, S // CMP_STRIDE, S // SEL_BLOCK, _pad_to(S // SEL_BLOCK, LANES))
    o_b = nsa_prompt(q, gate, kc_p, vc_p, kvb, wsel_p, B, S, kt_size=min(512, S))
    n_cmp_s = past // CMP_STRIDE - 1
    n_sb_s = -(-(past + Sd) // SEL_BLOCK)
    wsel_s = _overlap_weights(n_cmp_s, past // CMP_STRIDE, n_sb_s, _pad_to(n_sb_s, LANES))
    Wb = cache_b_win.shape[1]
    o_b, bwin_sample = nsa_step(q, gate, kc_s, vc_s, kv, cache_b_win.reshape(Bd, Wb, 2 * KV_W), cache2d, page_table,
                                wsel_s, o_b, Tp, Bd, Sd, pages_per_step=4)
    x = matmul_residual(o_b, w_o_b[0].astype(BF16), x, tm=tm_mid)
    x = peer_layer(x, g_ffn[1], w_peer_q[1], peer_subkeys[1], peer_u[1], peer_v[1],
                   tm_route=tm_peer, tm_exp=tm_peer, chunk=2048)
    y = rmsnorm_rows(x, g_final, tm=tm_mid)

    return (y[:Tp].reshape(B, S, D), y[Tp:].reshape(Bd, Sd, D),
            a_prompt[0], a_sample[0], a_prompt[1], a_sample[1], a_prompt[2], a_sample[2],
            bkv_prompt, bkv_sample, bwin_prompt, bwin_sample.reshape(cache_b_win.shape))
```

```python
import functools
import math

import jax
import jax.numpy as jnp
from jax import lax
from jax.experimental import pallas as pl
from jax.experimental.pallas import tpu as pltpu

F32 = jnp.float32
BF16 = jnp.bfloat16

HEAD_DIM = 64
ROT_DIM = HEAD_DIM // 4
ROPE_THETA = 500000.0
NORM_EPS = 1e-6
A_GROUPS = ((128, 1), (512, 4), (2048, 16))
A_HEADS = 8
A_BLOCK = 128
B_HEADS = 16
B_KV_HEADS = 4
B_GROUP = B_HEADS // B_KV_HEADS
CMP_LEN = 32
CMP_STRIDE = 16
SEL_BLOCK = 64
N_SEL = 16
SEL_OVERLAP_W = (1.0, 2.0, 2.0, 2.0, 1.0)
B_WINDOW = 512
Q_BLOCK = 128
PEER_HEADS = 8
PEER_KEYS = 128
PEER_QDIM = 256
PEER_TOPK = 16

LANES = 128
NEG = -1e30
VMEM_LIMIT = 50 * 1024 * 1024
G_PITCH = 72
KV_W = B_KV_HEADS * HEAD_DIM
SCALE = HEAD_DIM ** -0.5

_NT = (((1,), (1,)), ((), ()))


def _cparams(sem):
    return pltpu.CompilerParams(dimension_semantics=sem, vmem_limit_bytes=VMEM_LIMIT)


def _split2(a):
    hi = a.astype(BF16)
    lo = (a - hi.astype(F32)).astype(BF16)
    return hi, lo


def _dot(a, b):
    return jnp.dot(a, b, preferred_element_type=F32)


def _dot_nt(a, b):
    return lax.dot_general(a, b, _NT, preferred_element_type=F32)


def _dot_nt3(a, b):
    ah, al = _split2(a)
    bh, bl = _split2(b)
    return _dot_nt(ah, bh) + _dot_nt(ah, bl) + _dot_nt(al, bh)


def _softmax_rows(s, valid):
    s = jnp.where(valid, s, NEG)
    m = jnp.max(s, axis=-1, keepdims=True)
    e = jnp.where(valid, jnp.exp(s - m), 0.0)
    den = jnp.sum(e, axis=-1, keepdims=True)
    return e / jnp.where(den > 0, den, 1.0)


def _rope_lanes(y, cos, sa, sb):
    n = y.shape[-1]
    return y * cos + pltpu.roll(y, 8, 1) * sb + pltpu.roll(y, n - 8, 1) * sa


def _rms_norm_rows(x, g):
    ms = jnp.mean(x * x, axis=-1, keepdims=True)
    return x * lax.rsqrt(ms + NORM_EPS) * g


def _rms_mm_kernel(flags_ref, x_ref, g_ref, w_ref, *rest, epilogue, emit_bf16):
    n_out = 2 if emit_bf16 else 1
    extra, outs, xn_ref = rest[:len(rest) - n_out - 1], rest[len(rest) - n_out - 1:-1], rest[-1]
    j = pl.program_id(1)

    @pl.when(j == 0)
    def _():
        xn_ref[...] = _rms_norm_rows(x_ref[...], g_ref[...]).astype(BF16)

    def store(val):
        outs[0][...] = val
        if emit_bf16:
            outs[1][...] = val.astype(BF16)

    y = _dot(xn_ref[...], w_ref[...])
    if epilogue == "rope":
        cos_ref, sa_ref, sb_ref = extra
        f = flags_ref[j]

        @pl.when(f == 0)
        def _():
            store(y)

        @pl.when(f != 0)
        def _():
            store(_rope_lanes(y, cos_ref[...], sa_ref[...], sb_ref[...]))
    elif epilogue == "sigmoid":
        store(jax.nn.sigmoid(y + extra[0][...]))
    else:
        store(y)


def rms_matmul(x, g, w, *, tm, tn, epilogue="none", flags=None, tables=None, bias=None, emit_bf16=False):
    M, D = x.shape
    N = w.shape[1]
    assert M % tm == 0 and N % tn == 0
    nj = N // tn
    if flags is None:
        flags = jnp.zeros((nj,), jnp.int32)
    in_specs = [
        pl.BlockSpec((tm, D), lambda i, j, f: (i, 0)),
        pl.BlockSpec((1, D), lambda i, j, f: (0, 0)),
        pl.BlockSpec((D, tn), lambda i, j, f: (0, j)),
    ]
    args = [x, g.reshape(1, D), w]
    if epilogue == "rope":
        for t in tables:
            in_specs.append(pl.BlockSpec((tm, tn), lambda i, j, f: (i, 0)))
            args.append(t)
    elif epilogue == "sigmoid":
        in_specs.append(pl.BlockSpec((1, tn), lambda i, j, f: (0, j)))
        args.append(bias.reshape(1, N))
    out_spec = pl.BlockSpec((tm, tn), lambda i, j, f: (i, j))
    out_shape = [jax.ShapeDtypeStruct((M, N), F32)]
    if emit_bf16:
        out_shape.append(jax.ShapeDtypeStruct((M, N), BF16))
    res = pl.pallas_call(
        functools.partial(_rms_mm_kernel, epilogue=epilogue, emit_bf16=emit_bf16),
        out_shape=out_shape,
        grid_spec=pltpu.PrefetchScalarGridSpec(
            num_scalar_prefetch=1,
            grid=(M // tm, nj),
            in_specs=in_specs,
            out_specs=[out_spec] * len(out_shape),
            scratch_shapes=[pltpu.VMEM((tm, D), BF16)],
        ),
        compiler_params=_cparams(("parallel", "arbitrary")),
        name="rms_matmul_" + epilogue,
    )(flags, *args)
    return res if emit_bf16 else res[0]


def _two_source_specs(n_arr, tm, K, n_p):
    p_spec = pl.BlockSpec((tm, K), lambda i: (jnp.minimum(i, n_p - 1), 0))
    s_spec = pl.BlockSpec((tm, K), lambda i: (jnp.maximum(i - n_p, 0), 0))
    return [p_spec] * n_arr + [s_spec] * n_arr


def _mm_res_kernel(ap_ref, as_ref, w_ref, r_ref, o_ref, *, n_p):
    a = jnp.where(pl.program_id(0) >= n_p, as_ref[...], ap_ref[...])
    o_ref[...] = r_ref[...] + _dot(a.astype(BF16), w_ref[...])


def matmul_residual(a_p, a_s, w, res, *, tm):
    M, N = res.shape
    K = a_p.shape[1]
    n_p = a_p.shape[0] // tm
    assert a_p.shape[0] % tm == 0 and a_s.shape[0] % tm == 0
    return pl.pallas_call(
        functools.partial(_mm_res_kernel, n_p=n_p),
        out_shape=jax.ShapeDtypeStruct((M, N), F32),
        grid=(M // tm,),
        in_specs=_two_source_specs(1, tm, K, n_p) + [pl.BlockSpec((K, N), lambda i: (0, 0)),
                                                    pl.BlockSpec((tm, N), lambda i: (i, 0))],
        out_specs=pl.BlockSpec((tm, N), lambda i: (i, 0)),
        compiler_params=_cparams(("parallel",)),
        name="matmul_residual",
    )(a_p, a_s, w, res)


def _merge_mm_res_kernel(*refs, n_p):
    prompt, sample, (w_ref, r_ref, out_ref) = refs[:6], refs[6:12], refs[12:]
    is_s = pl.program_id(0) >= n_p
    o1, o2, o3, a1, a2, a3 = [jnp.where(is_s, s[...], p[...]) for p, s in zip(prompt, sample)]
    m = jnp.maximum(jnp.maximum(a1, a2), a3)
    e1, e2, e3 = jnp.exp(a1 - m), jnp.exp(a2 - m), jnp.exp(a3 - m)
    den = e1 + e2 + e3
    a = (e1 / den) * o1 + (e2 / den) * o2 + (e3 / den) * o3
    out_ref[...] = r_ref[...] + _dot(a.astype(BF16), w_ref[...])


def merge_matmul_residual(ol_p, ol_s, w, res, *, tm):
    M, N = res.shape
    K = ol_p[0].shape[1]
    n_p = ol_p[0].shape[0] // tm
    assert ol_p[0].shape[0] % tm == 0 and ol_s[0].shape[0] % tm == 0
    return pl.pallas_call(
        functools.partial(_merge_mm_res_kernel, n_p=n_p),
        out_shape=jax.ShapeDtypeStruct((M, N), F32),
        grid=(M // tm,),
        in_specs=_two_source_specs(6, tm, K, n_p) + [pl.BlockSpec((K, N), lambda i: (0, 0)),
                                                    pl.BlockSpec((tm, N), lambda i: (i, 0))],
        out_specs=pl.BlockSpec((tm, N), lambda i: (i, 0)),
        compiler_params=_cparams(("parallel",)),
        name="merge_matmul_residual",
    )(*ol_p, *ol_s, w, res)


def _rmsnorm_kernel(x_ref, g_ref, o_ref):
    o_ref[...] = _rms_norm_rows(x_ref[...], g_ref[...])


def rmsnorm_rows(x, g, *, tm):
    M, D = x.shape
    return pl.pallas_call(
        _rmsnorm_kernel,
        out_shape=jax.ShapeDtypeStruct((M, D), F32),
        grid=(M // tm,),
        in_specs=[pl.BlockSpec((tm, D), lambda i: (i, 0)), pl.BlockSpec((1, D), lambda i: (0, 0))],
        out_specs=pl.BlockSpec((tm, D), lambda i: (i, 0)),
        compiler_params=_cparams(("parallel",)),
        name="rmsnorm",
    )(x, g.reshape(1, D))


def _online_update(carry, s, mask, v):
    m_i, l_i, acc = carry
    s = jnp.where(mask, s, NEG)
    m_new = jnp.maximum(m_i, jnp.max(s, axis=-1, keepdims=True))
    alpha = jnp.exp(m_i - m_new)
    p = jnp.where(mask, jnp.exp(s - m_new), 0.0)
    return (m_new, alpha * l_i + jnp.sum(p, axis=-1, keepdims=True), alpha * acc + _dot(p.astype(BF16), v))


def _online_init(rows, width):
    return (jnp.full((rows, 1), NEG, F32), jnp.zeros((rows, 1), F32), jnp.zeros((rows, width), F32))


def _attend_heads(q, segments, n_heads):
    nq = q.shape[0]
    outs, lses = [], []
    for h in range(n_heads):
        qh = q[:, h * HEAD_DIM:(h + 1) * HEAD_DIM].astype(BF16)
        carry = _online_init(nq, HEAD_DIM)
        for k_rows, v_rows, mask in segments:
            carry = _online_update(carry, _dot_nt(qh, k_rows(h)) * SCALE, mask, v_rows(h))
        m, den, acc = carry
        outs.append(acc / den)
        lses.append(jnp.broadcast_to(m + jnp.log(den), (nq, HEAD_DIM)))
    return jnp.concatenate(outs, axis=1), jnp.concatenate(lses, axis=1)


def _pad_rows(x, n):
    return jnp.concatenate([x, jnp.zeros((n - x.shape[0], x.shape[1]), x.dtype)], axis=0)


def _band_attn_kernel(q_ref, kp_ref, kc_ref, vp_ref, vc_ref, o_ref, l_ref, *, nw):
    n = pl.program_id(2)
    kp, kc, vp, vc = kp_ref[...], kc_ref[...], vp_ref[...], vc_ref[...]
    i = lax.broadcasted_iota(jnp.int32, (A_BLOCK, 2 * A_BLOCK), 0)
    j = lax.broadcasted_iota(jnp.int32, (A_BLOCK, 2 * A_BLOCK), 1)
    rel = A_BLOCK + i - j
    mask = (rel >= 0) & (rel <= nw) & ((j >= A_BLOCK) | (n > 0))

    def rows(prev, cur):
        return lambda h: jnp.concatenate([prev[:, h * HEAD_DIM:(h + 1) * HEAD_DIM],
                                          cur[:, h * HEAD_DIM:(h + 1) * HEAD_DIM]], axis=0).astype(BF16)

    o, l = _attend_heads(q_ref[...], [(rows(kp, kc), rows(vp, vc), mask)], A_HEADS)
    o_ref[...] = o
    l_ref[...] = l


def band_attention(proj, g, window, dil, B, S):
    T, ncol = proj.shape
    L = S // dil
    nb = L // A_BLOCK
    hw = A_HEADS * HEAD_DIM
    per_res = ncol // hw
    view = proj.reshape(T // dil, dil * ncol)

    def spec(c, prev):
        def imap(b, r, n):
            nn = jnp.maximum(n - 1, 0) if prev else n
            return (b * nb + nn, r * per_res + g * 3 + c)
        return pl.BlockSpec((A_BLOCK, hw), imap)

    out_spec = pl.BlockSpec((A_BLOCK, hw), lambda b, r, n: (b * nb + n, r))
    o, l = pl.pallas_call(
        functools.partial(_band_attn_kernel, nw=window // dil),
        out_shape=[jax.ShapeDtypeStruct((B * L, dil * hw), F32)] * 2,
        grid=(B, dil, nb),
        in_specs=[spec(0, False), spec(1, True), spec(1, False), spec(2, True), spec(2, False)],
        out_specs=[out_spec, out_spec],
        compiler_params=_cparams(("parallel", "parallel", "arbitrary")),
        name=f"band_attention_d{dil}",
    )(view, view, view, view, view)
    return o.reshape(B * S, hw), l.reshape(B * S, hw)


def _a_step_kernel(q_ref, k_ref, v_ref, c_ref, o_ref, l_ref, s_ref, *, win, dil):
    W = c_ref.shape[1]
    Sd = q_ref.shape[0]
    hw = A_HEADS * HEAD_DIM
    knew, vnew = k_ref[...], v_ref[...]
    s_ref[0, pl.ds(0, W - Sd), :] = c_ref[0, pl.ds(Sd, W - Sd), :]
    s_ref[0, pl.ds(W - Sd, Sd), pl.ds(0, hw)] = knew
    s_ref[0, pl.ds(W - Sd, Sd), pl.ds(hw, hw)] = vnew

    def on_stride(dist):
        return (dist >= 0) & (dist <= win) & (jnp.bitwise_and(dist, dil - 1) == 0)

    dist_buf = (W + lax.broadcasted_iota(jnp.int32, (Sd, W), 0)) - lax.broadcasted_iota(jnp.int32, (Sd, W), 1)
    ncol = lax.broadcasted_iota(jnp.int32, (Sd, LANES), 1)
    dist_new = lax.broadcasted_iota(jnp.int32, (Sd, LANES), 0) - ncol
    knew_p, vnew_p = _pad_rows(knew, LANES), _pad_rows(vnew, LANES)
    buf_rows = lambda off: lambda h: c_ref[0, :, pl.ds(off + h * HEAD_DIM, HEAD_DIM)].astype(BF16)
    new_rows = lambda x: lambda h: x[:, h * HEAD_DIM:(h + 1) * HEAD_DIM].astype(BF16)
    segments = [(buf_rows(0), buf_rows(hw), on_stride(dist_buf)),
                (new_rows(knew_p), new_rows(vnew_p), on_stride(dist_new) & (ncol < Sd))]
    o, l = _attend_heads(q_ref[...], segments, A_HEADS)
    o_ref[...] = o
    l_ref[...] = l


def a_step(proj, g, window, dil, cache, Tp, Bd, Sd):
    assert dil & (dil - 1) == 0
    W = cache.shape[1]
    hw = A_HEADS * HEAD_DIM
    r0 = Tp // Sd
    col = lambda c: pl.BlockSpec((Sd, hw), lambda b: (r0 + b, g * 3 + c))
    row_out = pl.BlockSpec((Sd, hw), lambda b: (b, 0))
    return pl.pallas_call(
        functools.partial(_a_step_kernel, win=window, dil=dil),
        out_shape=[jax.ShapeDtypeStruct((Bd * Sd, hw), F32), jax.ShapeDtypeStruct((Bd * Sd, hw), F32),
                   jax.ShapeDtypeStruct(cache.shape, F32)],
        grid=(Bd,),
        in_specs=[col(0), col(1), col(2), pl.BlockSpec((1, W, 2 * hw), lambda b: (b, 0, 0))],
        out_specs=[row_out, row_out, pl.BlockSpec((1, W, 2 * hw), lambda b: (b, 0, 0))],
        compiler_params=_cparams(("parallel",)),
        name=f"a_step_d{dil}",
    )(proj, proj, proj, cache)


def _topk_cols(s, k, payload=None):
    N = s.shape[0]
    iota = lax.broadcasted_iota(jnp.int32, s.shape, 0).astype(F32)
    vals, picks = [], []
    for _ in range(k):
        m = jnp.max(s, axis=0, keepdims=True)
        idx = jnp.min(jnp.where(s == m, iota, float(N)), axis=0, keepdims=True)
        sel = iota == idx
        vals.append(m)
        if payload is None:
            picks.append(idx)
        else:
            picks.append(jnp.max(jnp.where(sel, payload, -1.0), axis=0, keepdims=True))
        s = jnp.where(sel, -jnp.inf, s)
    return vals, picks


def _topk_mask_cols(s, k):
    N = s.shape[0]
    iota = lax.broadcasted_iota(jnp.int32, s.shape, 0).astype(F32)
    chosen = jnp.zeros(s.shape, F32)
    for _ in range(k):
        m = jnp.max(s, axis=0, keepdims=True)
        idx = jnp.min(jnp.where(s == m, iota, float(N)), axis=0, keepdims=True)
        sel = iota == idx
        chosen = jnp.where(sel, 1.0, chosen)
        s = jnp.where(sel, -jnp.inf, s)
    return chosen


def _peer_route_kernel(x_ref, g_ref, wqh_ref, wql_ref, skh_ref, skl_ref,
                       hn_ref, e_ref, w_ref, xh_ref, xl_ref, et_ref, wt_ref):
    h = pl.program_id(1)

    @pl.when(h == 0)
    def _():
        xn = _rms_norm_rows(x_ref[...], g_ref[...])
        hi, lo = _split2(xn)
        xh_ref[...] = hi
        xl_ref[...] = lo
        hn_ref[...] = hi

    xh, xl = xh_ref[...], xl_ref[...]
    wqh, wql = wqh_ref[...], wql_ref[...]
    q = _dot(xh, wqh) + _dot(xh, wql) + _dot(xl, wqh)
    tops = []
    for z in range(2):
        qh, ql = _split2(q[:, z * PEER_KEYS:(z + 1) * PEER_KEYS])
        skh, skl = skh_ref[z], skl_ref[z]
        st = _dot_nt(skh, qh) + _dot_nt(skl, qh) + _dot_nt(skh, ql)
        tops.append(_topk_cols(st, PEER_TOPK))
    (v1, i1), (v2, i2) = tops
    K = PEER_TOPK
    tm = v1[0].shape[1]
    v2_lo, v2_hi = jnp.concatenate(v2[:8], axis=0), jnp.concatenate(v2[8:], axis=0)
    i2_lo, i2_hi = jnp.concatenate(i2[:8], axis=0), jnp.concatenate(i2[8:], axis=0)
    b_iota = lax.broadcasted_iota(jnp.int32, (8, tm), 0)
    e_of = lambda a, i2rows: i1[a] * float(PEER_KEYS) + i2rows
    cand = [v1[0] + v2_lo, v1[0] + v2_hi, v1[1] + v2_lo]
    cand_e = [e_of(0, i2_lo), e_of(0, i2_hi), e_of(1, i2_lo)]
    for a in range(2, 8):
        cand.append(jnp.where(b_iota < K // (a + 1), v1[a] + v2_lo, -jnp.inf))
        cand_e.append(e_of(a, i2_lo))
    cand.append(jnp.concatenate(v1[8:], axis=0) + v2[0])
    cand_e.append(jnp.concatenate(i1[8:], axis=0) * float(PEER_KEYS) + i2[0])
    best, best_e = _topk_cols(jnp.concatenate(cand, axis=0), K, payload=jnp.concatenate(cand_e, axis=0))
    bs = jnp.concatenate(best, axis=0)
    be = jnp.concatenate(best_e, axis=0)
    ex = jnp.exp(bs - best[0])
    gw = ex / jnp.sum(ex, axis=0, keepdims=True)
    row = pl.multiple_of(h * PEER_TOPK, PEER_TOPK)
    et_ref[pl.ds(row, PEER_TOPK), :] = be
    wt_ref[pl.ds(row, PEER_TOPK), :] = gw

    @pl.when(h == PEER_HEADS - 1)
    def _():
        e_ref[...] = et_ref[...].T.astype(jnp.int32)
        w_ref[...] = wt_ref[...].T


def peer_route(x, g, wq_hi, wq_lo, sk_hi, sk_lo, *, tm):
    T, D = x.shape
    nslot = PEER_HEADS * PEER_TOPK
    return pl.pallas_call(
        _peer_route_kernel,
        out_shape=[jax.ShapeDtypeStruct((T, D), BF16),
                   jax.ShapeDtypeStruct((T, nslot), jnp.int32),
                   jax.ShapeDtypeStruct((T, nslot), F32)],
        grid=(T // tm, PEER_HEADS),
        in_specs=[pl.BlockSpec((tm, D), lambda i, h: (i, 0)),
                  pl.BlockSpec((1, D), lambda i, h: (0, 0)),
                  pl.BlockSpec((D, PEER_QDIM), lambda i, h: (0, h)),
                  pl.BlockSpec((D, PEER_QDIM), lambda i, h: (0, h)),
                  pl.BlockSpec((2, PEER_KEYS, PEER_QDIM // 2), lambda i, h: (0, 0, 0)),
                  pl.BlockSpec((2, PEER_KEYS, PEER_QDIM // 2), lambda i, h: (0, 0, 0))],
        out_specs=[pl.BlockSpec((tm, D), lambda i, h: (i, 0)),
                   pl.BlockSpec((tm, nslot), lambda i, h: (i, 0)),
                   pl.BlockSpec((tm, nslot), lambda i, h: (i, 0))],
        scratch_shapes=[pltpu.VMEM((tm, D), BF16), pltpu.VMEM((tm, D), BF16),
                        pltpu.VMEM((nslot, tm), F32), pltpu.VMEM((nslot, tm), F32)],
        compiler_params=_cparams(("parallel", "arbitrary")),
        name="peer_route",
    )(x, g.reshape(1, D), wq_hi, wq_lo, sk_hi, sk_lo)


HI16 = -65536


def _peer_expert_kernel(hn_ref, e_ref, w_ref, x_ref, ua_ref, ub_ref, va_ref, vb_ref, o_ref, gs_ref, acc_ref,
                        *, tm, rows_per_chunk):
    c = pl.program_id(1)
    nk = PEER_KEYS
    half = nk // 2

    @pl.when(c == 0)
    def _():
        acc_ref[...] = jnp.zeros_like(acc_ref)
        sub = lax.broadcasted_iota(jnp.int32, (nk, nk), 0)

        def body(t8, carry):
            r0 = pl.multiple_of(t8 * 8, 8)
            e8 = e_ref[pl.ds(r0, 8), :]
            w8 = w_ref[pl.ds(r0, 8), :]
            for r in range(8):
                e, w = e8[r:r + 1], w8[r:r + 1]
                i1 = jnp.right_shift(e, 7)
                i2 = jnp.bitwise_and(e, nk - 1)
                pt = jnp.where(sub == i1, w, 0.0).astype(BF16)
                qt = jnp.where(sub == i2, 1.0, 0.0).astype(BF16)
                gt = _dot_nt(pt, qt).astype(BF16).astype(F32)
                bits = lax.bitcast_convert_type(gt, jnp.int32)
                word = jnp.bitwise_or(lax.shift_right_logical(bits[:half], 16), bits[half:])
                gs_ref[pl.ds(pl.multiple_of((r0 + r) * G_PITCH, 8), half), :] = word
            return carry

        lax.fori_loop(0, tm // 8, body, 0)

    hn = hn_ref[...]
    act_a = jax.nn.gelu(_dot_nt(hn, ua_ref[...]))
    act_b = jax.nn.gelu(_dot_nt(hn, ub_ref[...]))
    pa, pb = [], []
    for r in range(rows_per_chunk):
        word = gs_ref[pl.ds(c * rows_per_chunk + r, tm, stride=G_PITCH), :]
        g_lo = lax.bitcast_convert_type(jnp.left_shift(word, 16), F32)
        g_hi = lax.bitcast_convert_type(jnp.bitwise_and(word, HI16), F32)
        pa.append((act_a[:, r * nk:(r + 1) * nk] * g_lo).astype(BF16))
        pb.append((act_b[:, r * nk:(r + 1) * nk] * g_hi).astype(BF16))
    acc_ref[...] += (_dot(jnp.concatenate(pa, axis=1), va_ref[...])
                     + _dot(jnp.concatenate(pb, axis=1), vb_ref[...]))

    @pl.when(c == pl.num_programs(1) - 1)
    def _():
        o_ref[...] = x_ref[...] + acc_ref[...]


def peer_experts(hn, e, w, x, u, v, *, tm, chunk):
    T, D = x.shape
    n_exp = u.shape[0]
    nslot = e.shape[1]
    blk = chunk // 2
    rows_per_chunk = blk // PEER_KEYS
    n_chunks = n_exp // chunk
    lo = pl.BlockSpec((blk, D), lambda i, c: (c, 0))
    hi = pl.BlockSpec((blk, D), lambda i, c: (c + n_chunks, 0))
    return pl.pallas_call(
        functools.partial(_peer_expert_kernel, tm=tm, rows_per_chunk=rows_per_chunk),
        out_shape=jax.ShapeDtypeStruct((T, D), F32),
        grid=(T // tm, n_chunks),
        in_specs=[pl.BlockSpec((tm, D), lambda i, c: (i, 0)),
                  pl.BlockSpec((tm, nslot), lambda i, c: (i, 0)),
                  pl.BlockSpec((tm, nslot), lambda i, c: (i, 0)),
                  pl.BlockSpec((tm, D), lambda i, c: (i, 0)),
                  lo, hi, lo, hi],
        out_specs=pl.BlockSpec((tm, D), lambda i, c: (i, 0)),
        scratch_shapes=[pltpu.VMEM((tm * G_PITCH, PEER_KEYS), jnp.int32), pltpu.VMEM((tm, D), F32)],
        compiler_params=_cparams(("parallel", "arbitrary")),
        name="peer_experts",
    )(hn, e, w, x, u, u, v, v)


def peer_layer(x, g, w_q, subkeys, u, v, *, tm_route, tm_exp, chunk):
    wq_hi, wq_lo = _split2(w_q)
    sk_hi, sk_lo = _split2(subkeys)
    hn, e, w = peer_route(x, g, wq_hi, wq_lo, sk_hi, sk_lo, tm=tm_route)
    return peer_experts(hn, e, w, x, u.astype(BF16), v.astype(BF16), tm=tm_exp, chunk=chunk)


def _compress_core(get_x, w1k_ref, w1v_ref, pe_ref, w1f_ref, w2k_ref, w2v_ref, cos_ref, sa_ref, sb_ref,
                   kc_ref, vc_ref, acc_ref, n_cmp):
    nch = acc_ref.shape[0]
    for l in range(CMP_STRIDE):
        x = get_x(l)
        yk = _dot(x[:, :KV_W].astype(BF16), w1k_ref[l])
        yv = _dot(x[:, KV_W:].astype(BF16), w1v_ref[l])
        if l == 0:
            acc_ref[:, pl.ds(0, 2 * KV_W)] = yk
            acc_ref[:, pl.ds(2 * KV_W, 2 * KV_W)] = yv
        else:
            acc_ref[:, pl.ds(0, 2 * KV_W)] += yk
            acc_ref[:, pl.ds(2 * KV_W, 2 * KV_W)] += yv
    keep = lax.broadcasted_iota(jnp.int32, (nch, KV_W), 0) < n_cmp
    for z, (w2_ref, out_ref) in enumerate(((w2k_ref, kc_ref), (w2v_ref, vc_ref))):
        ph, pl_ = _split2(pe_ref[z])
        wh, wl = _split2(w1f_ref[z])
        pe_t = _dot(ph, wh) + _dot(ph, wl) + _dot(pl_, wh)
        pe_t = jnp.concatenate([pe_t] * B_KV_HEADS, axis=1)
        lo = acc_ref[:, pl.ds(z * 2 * KV_W, KV_W)]
        hi = acc_ref[:, pl.ds(z * 2 * KV_W + KV_W, KV_W)]
        hid = jax.nn.gelu(lo + pltpu.roll(hi, nch - 1, 0) + pe_t)
        out = _dot(hid.astype(BF16), w2_ref[...])
        if z == 0:
            out = _rope_lanes(out, cos_ref[...], sa_ref[...], sb_ref[...])
        out_ref[0] = jnp.where(keep, out, 0.0)


def _compress_prompt_kernel(x0, x1, x2, x3, *rest, n_cmp):
    slabs = (x0, x1, x2, x3)
    nch = rest[-1].shape[0]

    def get_x(l):
        return jnp.concatenate([s[pl.ds(l, nch, stride=CMP_STRIDE), :] for s in slabs], axis=1)

    _compress_core(get_x, *rest, n_cmp=n_cmp)


def _compress_weights(w_cmp1, w_cmp2, pe_cmp, n_chunk):
    eye = jnp.eye(B_KV_HEADS, dtype=F32)
    kron = lambda w: jnp.kron(eye, w)
    w1 = []
    for z in range(2):
        lo = jnp.stack([kron(w_cmp1[z, l]) for l in range(CMP_STRIDE)])
        hi = jnp.stack([kron(w_cmp1[z, CMP_STRIDE + l]) for l in range(CMP_STRIDE)])
        w1.append(jnp.concatenate([lo, hi], axis=2).astype(BF16))
    w2 = [kron(w_cmp2[z]).astype(BF16) for z in range(2)]
    pe = pe_cmp.reshape(2, 1, CMP_LEN * HEAD_DIM)
    w1f = w_cmp1.reshape(2, CMP_LEN * HEAD_DIM, -1)
    end_pos = jnp.arange(n_chunk, dtype=jnp.int32) * CMP_STRIDE + (CMP_LEN - 1)
    tabs = _rope_tables(end_pos, KV_W)
    return (w1[0], w1[1], pe, w1f, w2[0], w2[1]) + tabs


def _const_specs(arrs):
    return [pl.BlockSpec(a.shape, (lambda *i, nd=a.ndim: (0,) * nd), pipeline_mode=pl.Buffered(1)) for a in arrs]


def compress_prompt(kv, cw, B, S):
    nch = S // CMP_STRIDE
    slab = lambda s: pl.BlockSpec((S, LANES), lambda b: (b, s), pipeline_mode=pl.Buffered(1))
    out = pl.BlockSpec((1, nch, KV_W), lambda b: (b, 0, 0))
    return pl.pallas_call(
        functools.partial(_compress_prompt_kernel, n_cmp=nch - 1),
        out_shape=[jax.ShapeDtypeStruct((B, nch, KV_W), F32)] * 2,
        grid=(B,),
        in_specs=[slab(s) for s in range(4)] + _const_specs(cw),
        out_specs=[out, out],
        scratch_shapes=[pltpu.VMEM((nch, 4 * KV_W), F32)],
        compiler_params=_cparams(("parallel",)),
        name="compress_prompt",
    )(kv, kv, kv, kv, *cw)


def _compress_sample_kernel(pt_ref, *refs, n_cmp, pages_per_step):
    n_pg = 4 * pages_per_step
    pages, rest, x_ref = refs[:n_pg], refs[n_pg:-1], refs[-1]
    j = pl.program_id(1)
    ch_per_page = pages[0].shape[1] // CMP_STRIDE
    for p in range(pages_per_step):
        r0 = pl.multiple_of((j * pages_per_step + p) * ch_per_page, ch_per_page)
        for s in range(4):
            for l in range(CMP_STRIDE):
                x_ref[l, pl.ds(r0, ch_per_page), pl.ds(s * LANES, LANES)] = (
                    pages[p * 4 + s][0, pl.ds(l, ch_per_page, stride=CMP_STRIDE), :])

    @pl.when(j == pl.num_programs(1) - 1)
    def _():
        _compress_core(lambda l: x_ref[l], *rest, n_cmp=n_cmp)


def compress_sample(cache, page_table, cw, *, pages_per_step):
    Bd, n_pages = page_table.shape
    page = cache.shape[1]
    nch = n_pages * page // CMP_STRIDE
    P = pages_per_step

    def pspec(p, s):
        return pl.BlockSpec((1, page, LANES), lambda b, j, pt: (pt[b, j * P + p], 0, s))

    out = pl.BlockSpec((1, nch, KV_W), lambda b, j, pt: (b, 0, 0))
    return pl.pallas_call(
        functools.partial(_compress_sample_kernel, n_cmp=nch - 1, pages_per_step=P),
        out_shape=[jax.ShapeDtypeStruct((Bd, nch, KV_W), F32)] * 2,
        grid_spec=pltpu.PrefetchScalarGridSpec(
            num_scalar_prefetch=1,
            grid=(Bd, n_pages // P),
            in_specs=[pspec(p, s) for p in range(P) for s in range(4)] + _const_specs(cw),
            out_specs=[out, out],
            scratch_shapes=[pltpu.VMEM((nch, 4 * KV_W), F32),
                            pltpu.VMEM((CMP_STRIDE, nch, 2 * KV_W), F32)],
        ),
        compiler_params=_cparams(("parallel", "arbitrary")),
        name="compress_sample",
    )(page_table, *([cache] * (4 * P)), *cw)


def _select_blocks(imp, wsel, blk, cur):
    i_hi = imp.astype(BF16)
    r1 = imp - i_hi.astype(F32)
    i_mid = r1.astype(BF16)
    i_lo = (r1 - i_mid.astype(F32)).astype(BF16)
    imp_sel = _dot(i_hi, wsel) + _dot(i_mid, wsel) + _dot(i_lo, wsel)
    forced = (blk == 0) | (blk == cur) | (blk == cur - 1)
    imp_sel = jnp.where(blk > cur, -jnp.inf, jnp.where(forced, jnp.inf, imp_sel))
    return _topk_mask_cols(imp_sel.T, N_SEL).T


def _tile_block_hits(chosen_q, k0, kt_size):
    n_sb = chosen_q.shape[1]
    blk = lax.broadcasted_iota(jnp.int32, (n_sb, kt_size), 0)
    key_blk = jnp.right_shift(k0 + lax.broadcasted_iota(jnp.int32, (n_sb, kt_size), 1), 6)
    expand = jnp.where(blk == key_blk, 1.0, 0.0).astype(BF16)
    return _dot(chosen_q, expand)


def _tile_block_mask(chosen_q, k0, kt_size):
    return _tile_block_hits(chosen_q, k0, kt_size) > 0.5


def _nsa_prompt_kernel(q_ref, gate_ref, kc_ref, vc_ref, ks_ref, vs_ref, kw_ref, vw_ref, wsel_ref,
                       o_ref, *, n_cmp, kt_size):
    n = pl.program_id(1)
    t0 = n * Q_BLOCK
    G, TQ = B_GROUP, Q_BLOCK
    R = G * TQ
    NC = kc_ref.shape[1]
    NSB = wsel_ref.shape[1]
    wsel = wsel_ref[...]
    gt = gate_ref[...]
    row_c = lax.broadcasted_iota(jnp.int32, (R, NC), 0)
    col_c = lax.broadcasted_iota(jnp.int32, (R, NC), 1)
    valid_c = (col_c * CMP_STRIDE + (CMP_LEN - 1) <= t0 + jnp.bitwise_and(row_c, TQ - 1)) & (col_c < n_cmp)
    bias_c = jnp.where(valid_c, 0.0, NEG)
    sees_cmp = t0 + jnp.bitwise_and(lax.broadcasted_iota(jnp.int32, (R, 1), 0), TQ - 1) >= CMP_LEN - 1
    blk = lax.broadcasted_iota(jnp.int32, (TQ, NSB), 1)
    cur = jnp.right_shift(t0 + lax.broadcasted_iota(jnp.int32, (TQ, NSB), 0), 6)
    WK = B_WINDOW + TQ
    w0 = pl.multiple_of(jnp.maximum(t0 - B_WINDOW, 0), TQ)
    wpos = w0 + lax.broadcasted_iota(jnp.int32, (R, WK), 1)
    rel = t0 + jnp.bitwise_and(lax.broadcasted_iota(jnp.int32, (R, WK), 0), TQ - 1) - wpos
    bias_w = jnp.where((rel >= 0) & (rel <= B_WINDOW), 0.0, NEG)
    n_kt = (t0 + TQ + kt_size - 1) // kt_size
    k_last = pl.multiple_of((n_kt - 1) * kt_size, kt_size)
    causal_last = jnp.where(k_last + lax.broadcasted_iota(jnp.int32, (TQ, kt_size), 1)
                            <= t0 + lax.broadcasted_iota(jnp.int32, (TQ, kt_size), 0), 0.0, NEG)
    ones_col = jnp.where(lax.broadcasted_iota(jnp.int32, (kt_size, HEAD_DIM), 1) == 0, 1.0, 0.0).astype(BF16)

    outs = []
    for h in range(B_KV_HEADS):
        hs = pl.ds(h * HEAD_DIM, HEAD_DIM)
        qs = jnp.concatenate([q_ref[:, pl.ds((h * G + g) * HEAD_DIM, HEAD_DIM)] for g in range(G)], axis=0)
        qs = qs * SCALE
        qb = qs.astype(BF16)

        s_c = _dot_nt3(qs, kc_ref[0, :, hs]) + bias_c
        e_c = jnp.exp(s_c - jnp.max(s_c, axis=-1, keepdims=True))
        p_c = e_c * jnp.where(sees_cmp, 1.0 / jnp.sum(e_c, axis=-1, keepdims=True), 0.0)
        o_c = _dot(p_c.astype(BF16), vc_ref[0, :, hs].astype(BF16))
        imp = p_c[0:TQ]
        for g in range(1, G):
            imp = imp + p_c[g * TQ:(g + 1) * TQ]
        chosen_q = _select_blocks(imp, wsel, blk, cur).astype(BF16)

        def sel_tile(k0, carry, extra_bias):
            m_i, acc = carry
            bias = (_tile_block_hits(chosen_q, k0, kt_size) - 1.0) * (-NEG)
            if extra_bias is not None:
                bias = bias + extra_bias
            s = _dot_nt(qb, ks_ref[pl.ds(k0, kt_size), hs]).reshape(G, TQ, kt_size) + bias[None]
            s = s.reshape(R, kt_size)
            m_new = jnp.maximum(m_i, jnp.max(s, axis=-1, keepdims=True))
            p = jnp.exp(s - m_new).astype(BF16)
            v1 = jnp.concatenate([vs_ref[pl.ds(k0, kt_size), hs], ones_col], axis=1)
            return m_new, jnp.exp(m_i - m_new) * acc + _dot(p, v1)

        def sel_body(kt, carry):
            return sel_tile(pl.multiple_of(kt * kt_size, kt_size), carry, None)

        carry = lax.fori_loop(0, n_kt - 1, sel_body, (jnp.full((R, 1), NEG, F32), jnp.zeros((R, 2 * HEAD_DIM), F32)))
        _, acc_s = sel_tile(k_last, carry, causal_last)
        o_s = acc_s[:, :HEAD_DIM] / acc_s[:, HEAD_DIM:HEAD_DIM + 1]

        s_w = _dot_nt(qb, kw_ref[pl.ds(w0, WK), hs]) + bias_w
        e_w = jnp.exp(s_w - jnp.max(s_w, axis=-1, keepdims=True))
        p_w = e_w * (1.0 / jnp.sum(e_w, axis=-1, keepdims=True))
        o_w = _dot(p_w.astype(BF16), vw_ref[pl.ds(w0, WK), hs])

        for g in range(G):
            rs = slice(g * TQ, (g + 1) * TQ)
            c = 3 * (h * G + g)
            outs.append(gt[:, c:c + 1] * o_c[rs] + gt[:, c + 1:c + 2] * o_s[rs] + gt[:, c + 2:c + 3] * o_w[rs])
    o_ref[...] = jnp.concatenate(outs, axis=1)


def nsa_prompt(q, gate, kc, vc, kvb, wsel, B, S, *, kt_size):
    nq = q.shape[1]
    nqb = S // Q_BLOCK
    NC = kc.shape[1]
    comp = lambda c: pl.BlockSpec((S, KV_W), lambda b, n: (b, c))
    cmp_spec = pl.BlockSpec((1, NC, KV_W), lambda b, n: (b, 0, 0))
    return pl.pallas_call(
        functools.partial(_nsa_prompt_kernel, n_cmp=S // CMP_STRIDE - 1, kt_size=kt_size),
        out_shape=jax.ShapeDtypeStruct((B * S, nq), F32),
        grid=(B, nqb),
        in_specs=[pl.BlockSpec((Q_BLOCK, nq), lambda b, n: (b * nqb + n, 0)),
                  pl.BlockSpec((Q_BLOCK, LANES), lambda b, n: (b * nqb + n, 0)),
                  cmp_spec, cmp_spec, comp(2), comp(3), comp(4), comp(5),
                  pl.BlockSpec(wsel.shape, lambda b, n: (0, 0))],
        out_specs=pl.BlockSpec((Q_BLOCK, nq), lambda b, n: (b * nqb + n, 0)),
        compiler_params=_cparams(("parallel", "arbitrary")),
        name="nsa_prompt",
    )(q, gate, kc, vc, kvb, kvb, kvb, kvb, wsel)


def _nsa_step_kernel(pt_ref, q_ref, gate_ref, kc_ref, vc_ref, new_ref, win_ref, wsel_ref, *rest,
                     past, n_cmp, pages_per_step):
    P = pages_per_step
    pages = rest[:P]
    o_ref, st_ref, a_ref, ch_ref, m_ref, l_ref, acc_ref, oc_ref = rest[P:]
    del pt_ref
    j = pl.program_id(1)
    G, KVH = B_GROUP, B_KV_HEADS
    Sd = q_ref.shape[0]
    R = B_HEADS * Sd
    page = pages[0].shape[1]
    kt = P * page
    row_head = lambda shape: lax.broadcasted_iota(jnp.int32, shape, 0) // (G * Sd)
    own = lambda width: row_head((R, width)) == lax.broadcasted_iota(jnp.int32, (R, width), 1) // HEAD_DIM

    def diag(x):
        rows = G * Sd
        return jnp.concatenate([x[h * rows:(h + 1) * rows, h * HEAD_DIM:(h + 1) * HEAD_DIM]
                                for h in range(KVH)], axis=0)

    @pl.when(j == 0)
    def _():
        q = q_ref[...]
        a = jnp.concatenate([q[:, hd * HEAD_DIM:(hd + 1) * HEAD_DIM] for hd in range(B_HEADS)], axis=0)
        a = jnp.where(own(KV_W), jnp.concatenate([a] * KVH, axis=1), 0.0)
        a_ref[...] = a.astype(BF16)
        NC = kc_ref.shape[1]
        valid_c = lax.broadcasted_iota(jnp.int32, (R, NC), 1) < n_cmp
        p_c = _softmax_rows(_dot_nt3(a, kc_ref[0]) * SCALE, valid_c)
        oc_ref[...] = diag(_dot(p_c.astype(BF16), vc_ref[0].astype(BF16)))
        pooled = jnp.sum(p_c.reshape(KVH, G, Sd, NC), axis=1, keepdims=True)
        imp = jnp.broadcast_to(pooled, (KVH, G, Sd, NC)).reshape(R, NC)
        NSB = wsel_ref.shape[1]
        blk = lax.broadcasted_iota(jnp.int32, (R, NSB), 1)
        srow = jnp.bitwise_and(lax.broadcasted_iota(jnp.int32, (R, NSB), 0), Sd - 1)
        cur = jnp.right_shift(past + srow, 6)
        ch_ref[...] = _select_blocks(imp, wsel_ref[...], blk, cur).astype(BF16)
        m_ref[...] = jnp.full(m_ref.shape, NEG, F32)
        l_ref[...] = jnp.zeros(l_ref.shape, F32)
        acc_ref[...] = jnp.zeros(acc_ref.shape, F32)

    a = a_ref[...]
    ks = jnp.concatenate([pg[0, :, pl.ds(0, KV_W)] for pg in pages], axis=0).astype(BF16)
    vs = jnp.concatenate([pg[0, :, pl.ds(KV_W, KV_W)] for pg in pages], axis=0).astype(BF16)
    mk = _tile_block_mask(ch_ref[...], j * kt, kt)
    m_n, l_n, acc_n = _online_update((m_ref[...], l_ref[...], acc_ref[...]), _dot_nt(a, ks) * SCALE, mk, vs)
    m_ref[...] = m_n
    l_ref[...] = l_n
    acc_ref[...] = acc_n

    @pl.when(j == pl.num_programs(1) - 1)
    def _():
        new = _pad_rows(new_ref[...], LANES)
        comp = lambda c: new[:, c * KV_W:(c + 1) * KV_W].astype(BF16)
        srow = jnp.bitwise_and(lax.broadcasted_iota(jnp.int32, (R, LANES), 0), Sd - 1)
        scol = lax.broadcasted_iota(jnp.int32, (R, LANES), 1)
        causal_new = (scol <= srow) & (scol < Sd)
        nblk = past // SEL_BLOCK
        in_new = ch_ref[:, nblk:nblk + 1].astype(F32) > 0.5
        _, l_s, acc_s = _online_update((m_n, l_n, acc_n), _dot_nt(a, comp(2)) * SCALE, in_new & causal_new, comp(3))
        o_s = diag(acc_s) / jnp.where(l_s > 0, l_s, 1.0)

        W = win_ref.shape[1]
        wrow = jnp.bitwise_and(lax.broadcasted_iota(jnp.int32, (R, W), 0), Sd - 1)
        rel = W + wrow - lax.broadcasted_iota(jnp.int32, (R, W), 1)
        carry = _online_update(_online_init(R, KV_W), _dot_nt(a, win_ref[0, :, pl.ds(0, KV_W)].astype(BF16)) * SCALE,
                               (rel >= 0) & (rel <= B_WINDOW), win_ref[0, :, pl.ds(KV_W, KV_W)].astype(BF16))
        _, l_w, acc_w = _online_update(carry, _dot_nt(a, comp(4)) * SCALE, causal_new, comp(5))
        o_w = diag(acc_w) / jnp.where(l_w > 0, l_w, 1.0)
        o_c = oc_ref[...]
        gt = gate_ref[...]
        outs = []
        for hd in range(B_HEADS):
            rs = slice(hd * Sd, (hd + 1) * Sd)
            c = 3 * hd
            outs.append(gt[:, c:c + 1] * o_c[rs] + gt[:, c + 1:c + 2] * o_s[rs] + gt[:, c + 2:c + 3] * o_w[rs])
        o_ref[...] = jnp.concatenate(outs, axis=1)
        st_ref[0, pl.ds(0, W - Sd), :] = win_ref[0, pl.ds(Sd, W - Sd), :]
        st_ref[0, pl.ds(W - Sd, Sd), :] = new_ref[:, pl.ds(4 * KV_W, 2 * KV_W)]


def nsa_step(q, gate, kc, vc, kv, win, cache, page_table, wsel, Tp, Bd, Sd, *, pages_per_step):
    nq = q.shape[1]
    n_pages = page_table.shape[1]
    page = cache.shape[1]
    past = n_pages * page
    W = win.shape[1]
    NC = kc.shape[1]
    P = pages_per_step
    r0 = Tp // Sd
    R = B_HEADS * Sd
    rows = lambda w: pl.BlockSpec((Sd, w), lambda b, j, pt: (r0 + b, 0))
    per_b = lambda s1, s2: pl.BlockSpec((1, s1, s2), lambda b, j, pt: (b, 0, 0))
    pspec = lambda p: pl.BlockSpec((1, page, 2 * KV_W), lambda b, j, pt: (pt[b, j * P + p], 0, 1))
    return pl.pallas_call(
        functools.partial(_nsa_step_kernel, past=past, n_cmp=past // CMP_STRIDE - 1, pages_per_step=P),
        out_shape=[jax.ShapeDtypeStruct((Bd * Sd, nq), F32), jax.ShapeDtypeStruct(win.shape, F32)],
        grid_spec=pltpu.PrefetchScalarGridSpec(
            num_scalar_prefetch=1,
            grid=(Bd, n_pages // P),
            in_specs=[rows(nq), rows(LANES), per_b(NC, KV_W), per_b(NC, KV_W), rows(kv.shape[1]),
                      per_b(W, 2 * KV_W), pl.BlockSpec(wsel.shape, lambda b, j, pt: (0, 0))]
                     + [pspec(p) for p in range(P)],
            out_specs=[pl.BlockSpec((Sd, nq), lambda b, j, pt: (b, 0)), per_b(W, 2 * KV_W)],
            scratch_shapes=[pltpu.VMEM((R, KV_W), BF16), pltpu.VMEM((R, wsel.shape[1]), BF16),
                            pltpu.VMEM((R, 1), F32), pltpu.VMEM((R, 1), F32), pltpu.VMEM((R, KV_W), F32),
                            pltpu.VMEM((R, HEAD_DIM), F32)],
        ),
        compiler_params=_cparams(("parallel", "arbitrary")),
        name="nsa_step",
    )(page_table, q, gate, kc, vc, kv, win, wsel, *([cache] * P))


def _rope_tables(pos, width):
    half = ROT_DIM // 2
    inv = jnp.exp(-jnp.log(jnp.float32(ROPE_THETA)) * jnp.arange(half, dtype=F32) * (2.0 / ROT_DIM))
    ang = pos.astype(F32)[:, None] * inv[None, :]
    cos, sin = jnp.cos(ang), jnp.sin(ang)
    T = pos.shape[0]
    z8 = jnp.zeros((T, half), F32)
    rest = HEAD_DIM - ROT_DIM
    c = jnp.concatenate([cos, cos, jnp.ones((T, rest), F32)], axis=1)
    sa = jnp.concatenate([-sin, z8, jnp.zeros((T, rest), F32)], axis=1)
    sb = jnp.concatenate([z8, sin, jnp.zeros((T, rest), F32)], axis=1)
    reps = width // HEAD_DIM
    return tuple(jnp.tile(t, (1, reps)) for t in (c, sa, sb))


def _overlap_weights(n_cmp, n_cmp_pad, n_sb, n_sb_pad):
    j = jnp.arange(n_sb)
    cidx = (SEL_BLOCK // CMP_STRIDE) * j[:, None] - 1 + jnp.arange(len(SEL_OVERLAP_W))[None, :]
    wts = jnp.where((cidx >= 0) & (cidx < n_cmp), jnp.asarray(SEL_OVERLAP_W, F32), 0.0)
    onehot = (jnp.clip(cidx, 0, n_cmp - 1)[:, :, None] == jnp.arange(n_cmp_pad)[None, None, :]).astype(F32)
    wsel = jnp.einsum('jo,jon->nj', wts, onehot)
    return jnp.pad(wsel, ((0, 0), (0, n_sb_pad - n_sb))).astype(BF16)


def _pick(n, cands):
    for c in cands:
        if n % c == 0:
            return c
    return n


def _pad_to(n, m):
    return -(-n // m) * m


def kernel(x_prompt, x_sample, cache_a_g1, cache_a_g2, cache_a_g3, cache_b_kv, cache_b_win, page_table, g_mix, g_ffn, w_in_a, w_o_a, g_kv, w_kv_b, w_cmp1, w_cmp2, pe_cmp, w_qg_b, b_gate_b, w_o_b, w_peer_q, peer_subkeys, peer_u, peer_v, g_final):
    B, S, D = x_prompt.shape
    Bd, Sd, _ = x_sample.shape
    Tp, Ts = B * S, Bd * Sd
    T = Tp + Ts
    n_phys, page = cache_b_kv.shape[:2]
    n_pages = page_table.shape[1]
    past = n_pages * page
    hw = A_HEADS * HEAD_DIM
    tm_big = _pick(T, (1280, 640, 256, 128, 64, 32, 16, 8))
    tm_mid = _pick(T, (640, 256, 128, 64, 32, 16, 8))
    tm_peer = _pick(T, (256, 128, 64, 32, 16, 8))
    tm_exp = _pick(T, (520, 512, 256, 128, 64, 32, 16, 8))
    tm_rows = _pick(math.gcd(Tp, Ts), (256, 128, 64, 32, 16, 8))
    pages_per_step = _pick(n_pages, (8, 4, 2, 1))

    x = jnp.concatenate([x_prompt.reshape(Tp, D), x_sample.reshape(Ts, D)], axis=0)
    pos_p = jnp.arange(S, dtype=jnp.int32)
    pos_s = past + jnp.arange(Sd, dtype=jnp.int32)
    pos = jnp.concatenate([jnp.tile(pos_p, B), jnp.tile(pos_s, Bd)])
    tab512 = _rope_tables(pos, 512)
    tab256 = tuple(t[:, :KV_W] for t in tab512)

    n_grp = len(A_GROUPS)
    proj = rms_matmul(x, g_mix[0], w_in_a[0].astype(BF16), tm=tm_big, tn=hw, epilogue="rope",
                      flags=jnp.asarray([1, 1, 0] * n_grp, jnp.int32), tables=tab512)
    caches_a = (cache_a_g1, cache_a_g2, cache_a_g3)
    op, lp, os_, ls_, a_prompt, a_sample = [], [], [], [], [], []
    for g, (win, dil) in enumerate(A_GROUPS):
        o_g, l_g = band_attention(proj, g, win, dil, B, S)
        cache = caches_a[g]
        o_s, l_s, state = a_step(proj, g, win, dil, cache[0].reshape(Bd, cache.shape[2], 2 * hw), Tp, Bd, Sd)
        a_sample.append(state.reshape(cache.shape))
        keep = min(win, S)
        kv_cols = proj[:Tp].reshape(B, S, n_grp, 3 * hw)[:, S - keep:, g, hw:]
        a_prompt.append(kv_cols.reshape(1, B, keep, 2, A_HEADS, HEAD_DIM))
        op.append(o_g)
        lp.append(l_g)
        os_.append(o_s)
        ls_.append(l_s)
    x = merge_matmul_residual(op + lp, os_ + ls_, w_o_a[0].astype(BF16), x, tm=tm_rows)
    x = peer_layer(x, g_ffn[0], w_peer_q[0], peer_subkeys[0], peer_u[0], peer_v[0],
                   tm_route=tm_peer, tm_exp=tm_exp, chunk=1024)

    kv, kvb = rms_matmul(x, g_kv, w_kv_b.astype(BF16), tm=tm_big, tn=KV_W, epilogue="rope",
                         flags=jnp.asarray([0, 0, 1, 0, 1, 0], jnp.int32), tables=tab256, emit_bf16=True)
    kv_p = kv[:Tp].reshape(B, S, 6, B_KV_HEADS, HEAD_DIM)
    bkv_prompt = kv_p[:, :, :4]
    bkv_sample = kv[Tp:, :4 * KV_W].reshape(Bd, Sd, 4, B_KV_HEADS, HEAD_DIM)
    keep_w = min(B_WINDOW, S)
    bwin_prompt = kv_p[:, S - keep_w:, 4:]
    cache2d = cache_b_kv.reshape(n_phys, page, 4 * KV_W)
    kc_p, vc_p = compress_prompt(kv, _compress_weights(w_cmp1, w_cmp2, pe_cmp, S // CMP_STRIDE), B, S)
    kc_s, vc_s = compress_sample(cache2d, page_table, _compress_weights(w_cmp1, w_cmp2, pe_cmp, past // CMP_STRIDE),
                                 pages_per_step=pages_per_step)

    nq = B_HEADS * HEAD_DIM
    w_qg = w_qg_b[0]
    q = rms_matmul(x, g_mix[1], w_qg[:, :nq].astype(BF16), tm=tm_big, tn=512, epilogue="rope",
                   flags=jnp.ones((nq // 512,), jnp.int32), tables=tab512)
    ngate = 3 * B_HEADS
    w_gate = jnp.pad(w_qg[:, nq:], ((0, 0), (0, LANES - ngate))).astype(BF16)
    b_gate = jnp.pad(b_gate_b[0], (0, LANES - ngate))
    gate = rms_matmul(x, g_mix[1], w_gate, tm=tm_big, tn=LANES, epilogue="sigmoid", bias=b_gate)

    n_cmp_p = S // CMP_STRIDE - 1
    wsel_p = _overlap_weights(n_cmp_p, S // CMP_STRIDE, S // SEL_BLOCK, _pad_to(S // SEL_BLOCK, LANES))
    o_b_p = nsa_prompt(q, gate, kc_p, vc_p, kvb, wsel_p, B, S, kt_size=min(512, S))
    n_cmp_s = past // CMP_STRIDE - 1
    n_sb_s = -(-(past + Sd) // SEL_BLOCK)
    wsel_s = _overlap_weights(n_cmp_s, past // CMP_STRIDE, n_sb_s, _pad_to(n_sb_s, LANES))
    Wb = cache_b_win.shape[1]
    o_b_s, bwin_sample = nsa_step(q, gate, kc_s, vc_s, kv, cache_b_win.reshape(Bd, Wb, 2 * KV_W), cache2d,
                                  page_table, wsel_s, Tp, Bd, Sd, pages_per_step=pages_per_step)
    x = matmul_residual(o_b_p, o_b_s, w_o_b[0].astype(BF16), x, tm=tm_rows)
    x = peer_layer(x, g_ffn[1], w_peer_q[1], peer_subkeys[1], peer_u[1], peer_v[1],
                   tm_route=tm_peer, tm_exp=tm_exp, chunk=1024)
    y = rmsnorm_rows(x, g_final, tm=tm_mid)

    return (y[:Tp].reshape(B, S, D), y[Tp:].reshape(Bd, Sd, D),
            a_prompt[0], a_sample[0], a_prompt[1], a_sample[1], a_prompt[2], a_sample[2],
            bkv_prompt, bkv_sample, bwin_prompt, bwin_sample.reshape(cache_b_win.shape))
```

```python
import functools
import math

import jax
import jax.numpy as jnp
from jax import lax
from jax.experimental import pallas as pl
from jax.experimental.pallas import tpu as pltpu

F32 = jnp.float32
BF16 = jnp.bfloat16

HEAD_DIM = 64
ROT_DIM = HEAD_DIM // 4
ROPE_THETA = 500000.0
NORM_EPS = 1e-6
A_GROUPS = ((128, 1), (512, 4), (2048, 16))
A_HEADS = 8
A_BLOCK = 128
B_HEADS = 16
B_KV_HEADS = 4
B_GROUP = B_HEADS // B_KV_HEADS
CMP_LEN = 32
CMP_STRIDE = 16
SEL_BLOCK = 64
N_SEL = 16
SEL_OVERLAP_W = (1.0, 2.0, 2.0, 2.0, 1.0)
B_WINDOW = 512
Q_BLOCK = 128
PEER_HEADS = 8
PEER_KEYS = 128
PEER_QDIM = 256
PEER_TOPK = 16

LANES = 128
NEG = -1e30
VMEM_LIMIT = 50 * 1024 * 1024
G_PITCH = 72
KV_W = B_KV_HEADS * HEAD_DIM
SCALE = HEAD_DIM ** -0.5
LOG2E = 1.4426950408889634

_NT = (((1,), (1,)), ((), ()))


def _cparams(sem):
    return pltpu.CompilerParams(dimension_semantics=sem, vmem_limit_bytes=VMEM_LIMIT)


def _split2(a):
    hi = a.astype(BF16)
    lo = (a - hi.astype(F32)).astype(BF16)
    return hi, lo


def _dot(a, b):
    return jnp.dot(a, b, preferred_element_type=F32)


def _dot_nt(a, b):
    return lax.dot_general(a, b, _NT, preferred_element_type=F32)


def _dot_nt3(a, b):
    ah, al = _split2(a)
    bh, bl = _split2(b)
    return _dot_nt(ah, bh) + _dot_nt(ah, bl) + _dot_nt(al, bh)


def _softmax_rows(s, valid):
    s = jnp.where(valid, s, NEG)
    m = jnp.max(s, axis=-1, keepdims=True)
    e = jnp.where(valid, jnp.exp(s - m), 0.0)
    den = jnp.sum(e, axis=-1, keepdims=True)
    return e / jnp.where(den > 0, den, 1.0)


def _rope_lanes(y, cos, sa, sb):
    n = y.shape[-1]
    return y * cos + pltpu.roll(y, 8, 1) * sb + pltpu.roll(y, n - 8, 1) * sa


def _rms_norm_rows(x, g):
    ms = jnp.mean(x * x, axis=-1, keepdims=True)
    return x * lax.rsqrt(ms + NORM_EPS) * g


def _rms_mm_kernel(flags_ref, x_ref, g_ref, w_ref, *rest, epilogue, emit_bf16):
    n_out = 2 if emit_bf16 else 1
    extra, outs, xn_ref = rest[:len(rest) - n_out - 1], rest[len(rest) - n_out - 1:-1], rest[-1]
    j = pl.program_id(1)

    @pl.when(j == 0)
    def _():
        xn_ref[...] = _rms_norm_rows(x_ref[...], g_ref[...]).astype(BF16)

    def store(val):
        outs[0][...] = val
        if emit_bf16:
            outs[1][...] = val.astype(BF16)

    y = _dot(xn_ref[...], w_ref[...])
    if epilogue == "rope":
        cos_ref, sa_ref, sb_ref = extra
        f = flags_ref[j]

        @pl.when(f == 0)
        def _():
            store(y)

        @pl.when(f != 0)
        def _():
            store(_rope_lanes(y, cos_ref[...], sa_ref[...], sb_ref[...]))
    elif epilogue == "sigmoid":
        store(jax.nn.sigmoid(y + extra[0][...]))
    else:
        store(y)


def rms_matmul(x, g, w, *, tm, tn, epilogue="none", flags=None, tables=None, bias=None, emit_bf16=False):
    M, D = x.shape
    N = w.shape[1]
    assert M % tm == 0 and N % tn == 0
    nj = N // tn
    if flags is None:
        flags = jnp.zeros((nj,), jnp.int32)
    in_specs = [
        pl.BlockSpec((tm, D), lambda i, j, f: (i, 0)),
        pl.BlockSpec((1, D), lambda i, j, f: (0, 0)),
        pl.BlockSpec((D, tn), lambda i, j, f: (0, j)),
    ]
    args = [x, g.reshape(1, D), w]
    if epilogue == "rope":
        for t in tables:
            in_specs.append(pl.BlockSpec((tm, tn), lambda i, j, f: (i, 0)))
            args.append(t)
    elif epilogue == "sigmoid":
        in_specs.append(pl.BlockSpec((1, tn), lambda i, j, f: (0, j)))
        args.append(bias.reshape(1, N))
    out_spec = pl.BlockSpec((tm, tn), lambda i, j, f: (i, j))
    out_shape = [jax.ShapeDtypeStruct((M, N), F32)]
    if emit_bf16:
        out_shape.append(jax.ShapeDtypeStruct((M, N), BF16))
    res = pl.pallas_call(
        functools.partial(_rms_mm_kernel, epilogue=epilogue, emit_bf16=emit_bf16),
        out_shape=out_shape,
        grid_spec=pltpu.PrefetchScalarGridSpec(
            num_scalar_prefetch=1,
            grid=(M // tm, nj),
            in_specs=in_specs,
            out_specs=[out_spec] * len(out_shape),
            scratch_shapes=[pltpu.VMEM((tm, D), BF16)],
        ),
        compiler_params=_cparams(("parallel", "arbitrary")),
        name="rms_matmul_" + epilogue,
    )(flags, *args)
    return res if emit_bf16 else res[0]


def _two_source_specs(n_arr, tm, K, n_p):
    p_spec = pl.BlockSpec((tm, K), lambda i: (jnp.minimum(i, n_p - 1), 0))
    s_spec = pl.BlockSpec((tm, K), lambda i: (jnp.maximum(i - n_p, 0), 0))
    return [p_spec] * n_arr + [s_spec] * n_arr


def _mm_res_kernel(ap_ref, as_ref, w_ref, r_ref, o_ref, *, n_p):
    a = jnp.where(pl.program_id(0) >= n_p, as_ref[...], ap_ref[...])
    o_ref[...] = r_ref[...] + _dot(a.astype(BF16), w_ref[...])


def matmul_residual(a_p, a_s, w, res, *, tm):
    M, N = res.shape
    K = a_p.shape[1]
    n_p = a_p.shape[0] // tm
    assert a_p.shape[0] % tm == 0 and a_s.shape[0] % tm == 0
    return pl.pallas_call(
        functools.partial(_mm_res_kernel, n_p=n_p),
        out_shape=jax.ShapeDtypeStruct((M, N), F32),
        grid=(M // tm,),
        in_specs=_two_source_specs(1, tm, K, n_p) + [pl.BlockSpec((K, N), lambda i: (0, 0)),
                                                    pl.BlockSpec((tm, N), lambda i: (i, 0))],
        out_specs=pl.BlockSpec((tm, N), lambda i: (i, 0)),
        compiler_params=_cparams(("parallel",)),
        name="matmul_residual",
    )(a_p, a_s, w, res)


def _merge_mm_res_kernel(*refs, n_p):
    prompt, sample, (w_ref, r_ref, out_ref) = refs[:6], refs[6:12], refs[12:]
    is_s = pl.program_id(0) >= n_p
    o1, o2, o3, a1, a2, a3 = [jnp.where(is_s, s[...], p[...]) for p, s in zip(prompt, sample)]
    m = jnp.maximum(jnp.maximum(a1, a2), a3)
    e1, e2, e3 = jnp.exp(a1 - m), jnp.exp(a2 - m), jnp.exp(a3 - m)
    den = e1 + e2 + e3
    a = (e1 / den) * o1 + (e2 / den) * o2 + (e3 / den) * o3
    out_ref[...] = r_ref[...] + _dot(a.astype(BF16), w_ref[...])


def merge_matmul_residual(ol_p, ol_s, w, res, *, tm):
    M, N = res.shape
    K = ol_p[0].shape[1]
    n_p = ol_p[0].shape[0] // tm
    assert ol_p[0].shape[0] % tm == 0 and ol_s[0].shape[0] % tm == 0
    return pl.pallas_call(
        functools.partial(_merge_mm_res_kernel, n_p=n_p),
        out_shape=jax.ShapeDtypeStruct((M, N), F32),
        grid=(M // tm,),
        in_specs=_two_source_specs(6, tm, K, n_p) + [pl.BlockSpec((K, N), lambda i: (0, 0)),
                                                    pl.BlockSpec((tm, N), lambda i: (i, 0))],
        out_specs=pl.BlockSpec((tm, N), lambda i: (i, 0)),
        compiler_params=_cparams(("parallel",)),
        name="merge_matmul_residual",
    )(*ol_p, *ol_s, w, res)


def _rmsnorm_kernel(x_ref, g_ref, o_ref):
    o_ref[...] = _rms_norm_rows(x_ref[...], g_ref[...])


def rmsnorm_rows(x, g, *, tm):
    M, D = x.shape
    return pl.pallas_call(
        _rmsnorm_kernel,
        out_shape=jax.ShapeDtypeStruct((M, D), F32),
        grid=(M // tm,),
        in_specs=[pl.BlockSpec((tm, D), lambda i: (i, 0)), pl.BlockSpec((1, D), lambda i: (0, 0))],
        out_specs=pl.BlockSpec((tm, D), lambda i: (i, 0)),
        compiler_params=_cparams(("parallel",)),
        name="rmsnorm",
    )(x, g.reshape(1, D))


def _online_update(carry, s, mask, v):
    m_i, l_i, acc = carry
    s = jnp.where(mask, s, NEG)
    m_new = jnp.maximum(m_i, jnp.max(s, axis=-1, keepdims=True))
    alpha = jnp.exp(m_i - m_new)
    p = jnp.where(mask, jnp.exp(s - m_new), 0.0)
    return (m_new, alpha * l_i + jnp.sum(p, axis=-1, keepdims=True), alpha * acc + _dot(p.astype(BF16), v))


def _online_init(rows, width):
    return (jnp.full((rows, 1), NEG, F32), jnp.zeros((rows, 1), F32), jnp.zeros((rows, width), F32))


def _attend_heads(q, segments, n_heads):
    nq = q.shape[0]
    outs, lses = [], []
    for h in range(n_heads):
        qh = q[:, h * HEAD_DIM:(h + 1) * HEAD_DIM].astype(BF16)
        carry = _online_init(nq, HEAD_DIM)
        for k_rows, v_rows, mask in segments:
            carry = _online_update(carry, _dot_nt(qh, k_rows(h)) * SCALE, mask, v_rows(h))
        m, den, acc = carry
        outs.append(acc / den)
        lses.append(jnp.broadcast_to(m + jnp.log(den), (nq, HEAD_DIM)))
    return jnp.concatenate(outs, axis=1), jnp.concatenate(lses, axis=1)


def _pad_rows(x, n):
    return jnp.concatenate([x, jnp.zeros((n - x.shape[0], x.shape[1]), x.dtype)], axis=0)


def _band_attn_kernel(q_ref, kp_ref, kc_ref, vp_ref, vc_ref, o_ref, l_ref, *, nw):
    n = pl.program_id(2)
    kp, kc, vp, vc = kp_ref[...], kc_ref[...], vp_ref[...], vc_ref[...]
    i = lax.broadcasted_iota(jnp.int32, (A_BLOCK, 2 * A_BLOCK), 0)
    j = lax.broadcasted_iota(jnp.int32, (A_BLOCK, 2 * A_BLOCK), 1)
    rel = A_BLOCK + i - j
    mask = (rel >= 0) & (rel <= nw) & ((j >= A_BLOCK) | (n > 0))

    def rows(prev, cur):
        return lambda h: jnp.concatenate([prev[:, h * HEAD_DIM:(h + 1) * HEAD_DIM],
                                          cur[:, h * HEAD_DIM:(h + 1) * HEAD_DIM]], axis=0).astype(BF16)

    o, l = _attend_heads(q_ref[...], [(rows(kp, kc), rows(vp, vc), mask)], A_HEADS)
    o_ref[...] = o
    l_ref[...] = l


def band_attention(proj, g, window, dil, B, S):
    T, ncol = proj.shape
    L = S // dil
    nb = L // A_BLOCK
    hw = A_HEADS * HEAD_DIM
    per_res = ncol // hw
    view = proj.reshape(T // dil, dil * ncol)

    def spec(c, prev):
        def imap(b, r, n):
            nn = jnp.maximum(n - 1, 0) if prev else n
            return (b * nb + nn, r * per_res + g * 3 + c)
        return pl.BlockSpec((A_BLOCK, hw), imap)

    out_spec = pl.BlockSpec((A_BLOCK, hw), lambda b, r, n: (b * nb + n, r))
    o, l = pl.pallas_call(
        functools.partial(_band_attn_kernel, nw=window // dil),
        out_shape=[jax.ShapeDtypeStruct((B * L, dil * hw), F32)] * 2,
        grid=(B, dil, nb),
        in_specs=[spec(0, False), spec(1, True), spec(1, False), spec(2, True), spec(2, False)],
        out_specs=[out_spec, out_spec],
        compiler_params=_cparams(("parallel", "parallel", "arbitrary")),
        name=f"band_attention_d{dil}",
    )(view, view, view, view, view)
    return o.reshape(B * S, hw), l.reshape(B * S, hw)


def _a_step_kernel(q_ref, k_ref, v_ref, c_ref, o_ref, l_ref, s_ref, *, win, dil):
    W = c_ref.shape[1]
    Sd = q_ref.shape[0]
    hw = A_HEADS * HEAD_DIM
    knew, vnew = k_ref[...], v_ref[...]
    s_ref[0, pl.ds(0, W - Sd), :] = c_ref[0, pl.ds(Sd, W - Sd), :]
    s_ref[0, pl.ds(W - Sd, Sd), pl.ds(0, hw)] = knew
    s_ref[0, pl.ds(W - Sd, Sd), pl.ds(hw, hw)] = vnew

    def on_stride(dist):
        return (dist >= 0) & (dist <= win) & (jnp.bitwise_and(dist, dil - 1) == 0)

    dist_buf = (W + lax.broadcasted_iota(jnp.int32, (Sd, W), 0)) - lax.broadcasted_iota(jnp.int32, (Sd, W), 1)
    ncol = lax.broadcasted_iota(jnp.int32, (Sd, LANES), 1)
    dist_new = lax.broadcasted_iota(jnp.int32, (Sd, LANES), 0) - ncol
    knew_p, vnew_p = _pad_rows(knew, LANES), _pad_rows(vnew, LANES)
    buf_rows = lambda off: lambda h: c_ref[0, :, pl.ds(off + h * HEAD_DIM, HEAD_DIM)].astype(BF16)
    new_rows = lambda x: lambda h: x[:, h * HEAD_DIM:(h + 1) * HEAD_DIM].astype(BF16)
    segments = [(buf_rows(0), buf_rows(hw), on_stride(dist_buf)),
                (new_rows(knew_p), new_rows(vnew_p), on_stride(dist_new) & (ncol < Sd))]
    o, l = _attend_heads(q_ref[...], segments, A_HEADS)
    o_ref[...] = o
    l_ref[...] = l


def a_step(proj, g, window, dil, cache, Tp, Bd, Sd):
    assert dil & (dil - 1) == 0
    W = cache.shape[1]
    hw = A_HEADS * HEAD_DIM
    r0 = Tp // Sd
    col = lambda c: pl.BlockSpec((Sd, hw), lambda b: (r0 + b, g * 3 + c))
    row_out = pl.BlockSpec((Sd, hw), lambda b: (b, 0))
    return pl.pallas_call(
        functools.partial(_a_step_kernel, win=window, dil=dil),
        out_shape=[jax.ShapeDtypeStruct((Bd * Sd, hw), F32), jax.ShapeDtypeStruct((Bd * Sd, hw), F32),
                   jax.ShapeDtypeStruct(cache.shape, F32)],
        grid=(Bd,),
        in_specs=[col(0), col(1), col(2), pl.BlockSpec((1, W, 2 * hw), lambda b: (b, 0, 0))],
        out_specs=[row_out, row_out, pl.BlockSpec((1, W, 2 * hw), lambda b: (b, 0, 0))],
        compiler_params=_cparams(("parallel",)),
        name=f"a_step_d{dil}",
    )(proj, proj, proj, cache)


def _topk_cols(s, k, payload=None):
    N = s.shape[0]
    iota = lax.broadcasted_iota(jnp.int32, s.shape, 0).astype(F32)
    vals, picks = [], []
    for _ in range(k):
        m = jnp.max(s, axis=0, keepdims=True)
        idx = jnp.min(jnp.where(s == m, iota, float(N)), axis=0, keepdims=True)
        sel = iota == idx
        vals.append(m)
        if payload is None:
            picks.append(idx)
        else:
            picks.append(jnp.max(jnp.where(sel, payload, -1.0), axis=0, keepdims=True))
        s = jnp.where(sel, -jnp.inf, s)
    return vals, picks


def _topk_cols_many(arrs, k):
    N = arrs[0].shape[0]
    iota = lax.broadcasted_iota(jnp.int32, arrs[0].shape, 0).astype(F32)
    arrs = list(arrs)
    out = [([], []) for _ in arrs]
    for _ in range(k):
        for n, s in enumerate(arrs):
            m = jnp.max(s, axis=0, keepdims=True)
            idx = jnp.min(jnp.where(s == m, iota, float(N)), axis=0, keepdims=True)
            out[n][0].append(m)
            out[n][1].append(idx)
            arrs[n] = jnp.where(iota == idx, -jnp.inf, s)
    return out


def _topk_mask_cols(s, k):
    N = s.shape[0]
    iota = lax.broadcasted_iota(jnp.int32, s.shape, 0).astype(F32)
    chosen = jnp.zeros(s.shape, F32)
    for _ in range(k):
        m = jnp.max(s, axis=0, keepdims=True)
        idx = jnp.min(jnp.where(s == m, iota, float(N)), axis=0, keepdims=True)
        sel = iota == idx
        chosen = jnp.where(sel, 1.0, chosen)
        s = jnp.where(sel, -jnp.inf, s)
    return chosen


def _peer_route_kernel(x_ref, g_ref, wqh_ref, wql_ref, skh_ref, skl_ref,
                       hn_ref, e_ref, w_ref, xh_ref, xl_ref, et_ref, wt_ref, *, heads_per_step):
    h = pl.program_id(1)

    @pl.when(h == 0)
    def _():
        xn = _rms_norm_rows(x_ref[...], g_ref[...])
        hi, lo = _split2(xn)
        xh_ref[...] = hi
        xl_ref[...] = lo
        hn_ref[...] = hi

    xh, xl = xh_ref[...], xl_ref[...]
    K = PEER_TOPK
    b_iota = lax.broadcasted_iota(jnp.int32, (8, LANES), 0)
    jobs = []
    for hh in range(heads_per_step):
        cs = pl.ds(hh * PEER_QDIM, PEER_QDIM)
        wqh, wql = wqh_ref[:, cs], wql_ref[:, cs]
        q = _dot(xh, wqh) + _dot(xh, wql) + _dot(xl, wqh)
        sts = []
        for z in range(2):
            qh, ql = _split2(q[:, z * PEER_KEYS:(z + 1) * PEER_KEYS])
            skh, skl = skh_ref[z], skl_ref[z]
            sts.append(_dot_nt(skh, qh) + _dot_nt(skl, qh) + _dot_nt(skh, ql))
        row = pl.multiple_of((h * heads_per_step + hh) * K, K)
        jobs += [(sts, row, c0) for c0 in range(0, xh.shape[0], LANES)]
    for sts, row, c0 in jobs:
        (v1, i1), (v2, i2) = _topk_cols_many([st[:, c0:c0 + LANES] for st in sts], K)
        v2_lo, v2_hi = jnp.concatenate(v2[:8], axis=0), jnp.concatenate(v2[8:], axis=0)
        i2_lo, i2_hi = jnp.concatenate(i2[:8], axis=0), jnp.concatenate(i2[8:], axis=0)
        e_of = lambda a, i2rows: i1[a] * float(PEER_KEYS) + i2rows
        cand = [v1[0] + v2_lo, v1[0] + v2_hi, v1[1] + v2_lo]
        cand_e = [e_of(0, i2_lo), e_of(0, i2_hi), e_of(1, i2_lo)]
        for a in range(2, 8):
            cand.append(jnp.where(b_iota < K // (a + 1), v1[a] + v2_lo, -jnp.inf))
            cand_e.append(e_of(a, i2_lo))
        cand.append(jnp.concatenate(v1[8:], axis=0) + v2[0])
        cand_e.append(jnp.concatenate(i1[8:], axis=0) * float(PEER_KEYS) + i2[0])
        best, best_e = _topk_cols(jnp.concatenate(cand, axis=0), K, payload=jnp.concatenate(cand_e, axis=0))
        bs = jnp.concatenate(best, axis=0)
        ex = jnp.exp(bs - best[0])
        et_ref[pl.ds(row, K), pl.ds(c0, LANES)] = jnp.concatenate(best_e, axis=0)
        wt_ref[pl.ds(row, K), pl.ds(c0, LANES)] = ex / jnp.sum(ex, axis=0, keepdims=True)

    @pl.when(h == pl.num_programs(1) - 1)
    def _():
        e_ref[...] = et_ref[...].T.astype(jnp.int32)
        w_ref[...] = wt_ref[...].T


def peer_route(x, g, wq_hi, wq_lo, sk_hi, sk_lo, *, tm, heads_per_step=8):
    T, D = x.shape
    nslot = PEER_HEADS * PEER_TOPK
    wq_spec = pl.BlockSpec((D, heads_per_step * PEER_QDIM), lambda i, h: (0, h))
    return pl.pallas_call(
        functools.partial(_peer_route_kernel, heads_per_step=heads_per_step),
        out_shape=[jax.ShapeDtypeStruct((T, D), BF16),
                   jax.ShapeDtypeStruct((T, nslot), jnp.int32),
                   jax.ShapeDtypeStruct((T, nslot), F32)],
        grid=(T // tm, PEER_HEADS // heads_per_step),
        in_specs=[pl.BlockSpec((tm, D), lambda i, h: (i, 0)),
                  pl.BlockSpec((1, D), lambda i, h: (0, 0)),
                  wq_spec, wq_spec,
                  pl.BlockSpec((2, PEER_KEYS, PEER_QDIM // 2), lambda i, h: (0, 0, 0)),
                  pl.BlockSpec((2, PEER_KEYS, PEER_QDIM // 2), lambda i, h: (0, 0, 0))],
        out_specs=[pl.BlockSpec((tm, D), lambda i, h: (i, 0)),
                   pl.BlockSpec((tm, nslot), lambda i, h: (i, 0)),
                   pl.BlockSpec((tm, nslot), lambda i, h: (i, 0))],
        scratch_shapes=[pltpu.VMEM((tm, D), BF16), pltpu.VMEM((tm, D), BF16),
                        pltpu.VMEM((nslot, tm), F32), pltpu.VMEM((nslot, tm), F32)],
        compiler_params=_cparams(("parallel", "arbitrary")),
        name="peer_route",
    )(x, g.reshape(1, D), wq_hi, wq_lo, sk_hi, sk_lo)


HI16 = -65536


def _peer_expert_kernel(hn_ref, e_ref, w_ref, x_ref, ua_ref, ub_ref, va_ref, vb_ref, o_ref, gs_ref, acc_ref,
                        *, tm, rows_per_chunk):
    c = pl.program_id(1)
    nk = PEER_KEYS
    half = nk // 2

    @pl.when(c == 0)
    def _():
        acc_ref[...] = jnp.zeros_like(acc_ref)
        sub = lax.broadcasted_iota(jnp.int32, (nk, nk), 0)

        def body(t8, carry):
            r0 = pl.multiple_of(t8 * 8, 8)
            e8 = e_ref[pl.ds(r0, 8), :]
            w8 = w_ref[pl.ds(r0, 8), :]
            for r in range(8):
                e, w = e8[r:r + 1], w8[r:r + 1]
                i1 = jnp.right_shift(e, 7)
                i2 = jnp.bitwise_and(e, nk - 1)
                pt = jnp.where(sub == i1, w, 0.0).astype(BF16)
                qt = jnp.where(sub == i2, 1.0, 0.0).astype(BF16)
                gt = _dot_nt(pt, qt).astype(BF16).astype(F32)
                bits = lax.bitcast_convert_type(gt, jnp.int32)
                word = jnp.bitwise_or(lax.shift_right_logical(bits[:half], 16), bits[half:])
                gs_ref[pl.ds(pl.multiple_of((r0 + r) * G_PITCH, 8), half), :] = word
            return carry

        lax.fori_loop(0, tm // 8, body, 0)

    hn = hn_ref[...]
    act_a = jax.nn.gelu(_dot_nt(hn, ua_ref[...]))
    act_b = jax.nn.gelu(_dot_nt(hn, ub_ref[...]))
    pa, pb = [], []
    for r in range(rows_per_chunk):
        word = gs_ref[pl.ds(c * rows_per_chunk + r, tm, stride=G_PITCH), :]
        g_lo = lax.bitcast_convert_type(jnp.left_shift(word, 16), F32)
        g_hi = lax.bitcast_convert_type(jnp.bitwise_and(word, HI16), F32)
        pa.append((act_a[:, r * nk:(r + 1) * nk] * g_lo).astype(BF16))
        pb.append((act_b[:, r * nk:(r + 1) * nk] * g_hi).astype(BF16))
    acc_ref[...] += (_dot(jnp.concatenate(pa, axis=1), va_ref[...])
                     + _dot(jnp.concatenate(pb, axis=1), vb_ref[...]))

    @pl.when(c == pl.num_programs(1) - 1)
    def _():
        o_ref[...] = x_ref[...] + acc_ref[...]


def peer_experts(hn, e, w, x, u, v, *, tm, chunk):
    T, D = x.shape
    n_exp = u.shape[0]
    nslot = e.shape[1]
    blk = chunk // 2
    rows_per_chunk = blk // PEER_KEYS
    n_chunks = n_exp // chunk
    lo = pl.BlockSpec((blk, D), lambda i, c: (c, 0))
    hi = pl.BlockSpec((blk, D), lambda i, c: (c + n_chunks, 0))
    return pl.pallas_call(
        functools.partial(_peer_expert_kernel, tm=tm, rows_per_chunk=rows_per_chunk),
        out_shape=jax.ShapeDtypeStruct((T, D), F32),
        grid=(T // tm, n_chunks),
        in_specs=[pl.BlockSpec((tm, D), lambda i, c: (i, 0)),
                  pl.BlockSpec((tm, nslot), lambda i, c: (i, 0)),
                  pl.BlockSpec((tm, nslot), lambda i, c: (i, 0)),
                  pl.BlockSpec((tm, D), lambda i, c: (i, 0)),
                  lo, hi, lo, hi],
        out_specs=pl.BlockSpec((tm, D), lambda i, c: (i, 0)),
        scratch_shapes=[pltpu.VMEM((tm * G_PITCH, PEER_KEYS), jnp.int32), pltpu.VMEM((tm, D), F32)],
        compiler_params=_cparams(("parallel", "arbitrary")),
        name="peer_experts",
    )(hn, e, w, x, u, u, v, v)


def peer_layer(x, g, w_q, subkeys, u, v, *, tm_route, tm_exp, chunk):
    wq_hi, wq_lo = _split2(w_q)
    sk_hi, sk_lo = _split2(subkeys)
    hn, e, w = peer_route(x, g, wq_hi, wq_lo, sk_hi, sk_lo, tm=tm_route)
    return peer_experts(hn, e, w, x, u.astype(BF16), v.astype(BF16), tm=tm_exp, chunk=chunk)


def _compress_core(get_x, w1k_ref, w1v_ref, pe_ref, w1f_ref, w2k_ref, w2v_ref, cos_ref, sa_ref, sb_ref,
                   kc_ref, vc_ref, acc_ref, n_cmp):
    nch = acc_ref.shape[0]
    for l in range(CMP_STRIDE):
        x = get_x(l)
        yk = _dot(x[:, :KV_W].astype(BF16), w1k_ref[l])
        yv = _dot(x[:, KV_W:].astype(BF16), w1v_ref[l])
        if l == 0:
            acc_ref[:, pl.ds(0, 2 * KV_W)] = yk
            acc_ref[:, pl.ds(2 * KV_W, 2 * KV_W)] = yv
        else:
            acc_ref[:, pl.ds(0, 2 * KV_W)] += yk
            acc_ref[:, pl.ds(2 * KV_W, 2 * KV_W)] += yv
    keep = lax.broadcasted_iota(jnp.int32, (nch, KV_W), 0) < n_cmp
    for z, (w2_ref, out_ref) in enumerate(((w2k_ref, kc_ref), (w2v_ref, vc_ref))):
        ph, pl_ = _split2(pe_ref[z])
        wh, wl = _split2(w1f_ref[z])
        pe_t = _dot(ph, wh) + _dot(ph, wl) + _dot(pl_, wh)
        pe_t = jnp.concatenate([pe_t] * B_KV_HEADS, axis=1)
        lo = acc_ref[:, pl.ds(z * 2 * KV_W, KV_W)]
        hi = acc_ref[:, pl.ds(z * 2 * KV_W + KV_W, KV_W)]
        hid = jax.nn.gelu(lo + pltpu.roll(hi, nch - 1, 0) + pe_t)
        out = _dot(hid.astype(BF16), w2_ref[...])
        if z == 0:
            out = _rope_lanes(out, cos_ref[...], sa_ref[...], sb_ref[...])
        out_ref[0] = jnp.where(keep, out, 0.0)


def _compress_prompt_kernel(x0, x1, x2, x3, *rest, n_cmp):
    slabs = (x0, x1, x2, x3)
    nch = rest[-1].shape[0]

    def get_x(l):
        return jnp.concatenate([s[pl.ds(l, nch, stride=CMP_STRIDE), :] for s in slabs], axis=1)

    _compress_core(get_x, *rest, n_cmp=n_cmp)


def _compress_weights(w_cmp1, w_cmp2, pe_cmp, n_chunk):
    eye = jnp.eye(B_KV_HEADS, dtype=F32)
    kron = lambda w: jnp.kron(eye, w)
    w1 = []
    for z in range(2):
        lo = jnp.stack([kron(w_cmp1[z, l]) for l in range(CMP_STRIDE)])
        hi = jnp.stack([kron(w_cmp1[z, CMP_STRIDE + l]) for l in range(CMP_STRIDE)])
        w1.append(jnp.concatenate([lo, hi], axis=2).astype(BF16))
    w2 = [kron(w_cmp2[z]).astype(BF16) for z in range(2)]
    pe = pe_cmp.reshape(2, 1, CMP_LEN * HEAD_DIM)
    w1f = w_cmp1.reshape(2, CMP_LEN * HEAD_DIM, -1)
    end_pos = jnp.arange(n_chunk, dtype=jnp.int32) * CMP_STRIDE + (CMP_LEN - 1)
    tabs = _rope_tables(end_pos, KV_W)
    return (w1[0], w1[1], pe, w1f, w2[0], w2[1]) + tabs


def _const_specs(arrs):
    return [pl.BlockSpec(a.shape, (lambda *i, nd=a.ndim: (0,) * nd), pipeline_mode=pl.Buffered(1)) for a in arrs]


def compress_prompt(kv, cw, B, S):
    nch = S // CMP_STRIDE
    slab = lambda s: pl.BlockSpec((S, LANES), lambda b: (b, s), pipeline_mode=pl.Buffered(1))
    out = pl.BlockSpec((1, nch, KV_W), lambda b: (b, 0, 0))
    return pl.pallas_call(
        functools.partial(_compress_prompt_kernel, n_cmp=nch - 1),
        out_shape=[jax.ShapeDtypeStruct((B, nch, KV_W), F32)] * 2,
        grid=(B,),
        in_specs=[slab(s) for s in range(4)] + _const_specs(cw),
        out_specs=[out, out],
        scratch_shapes=[pltpu.VMEM((nch, 4 * KV_W), F32)],
        compiler_params=_cparams(("parallel",)),
        name="compress_prompt",
    )(kv, kv, kv, kv, *cw)


def _compress_sample_kernel(pt_ref, *refs, n_cmp, pages_per_step):
    n_pg = 4 * pages_per_step
    pages, rest, x_ref = refs[:n_pg], refs[n_pg:-1], refs[-1]
    j = pl.program_id(1)
    ch_per_page = pages[0].shape[1] // CMP_STRIDE
    for p in range(pages_per_step):
        r0 = pl.multiple_of((j * pages_per_step + p) * ch_per_page, ch_per_page)
        for s in range(4):
            for l in range(CMP_STRIDE):
                x_ref[l, pl.ds(r0, ch_per_page), pl.ds(s * LANES, LANES)] = (
                    pages[p * 4 + s][0, pl.ds(l, ch_per_page, stride=CMP_STRIDE), :])

    @pl.when(j == pl.num_programs(1) - 1)
    def _():
        _compress_core(lambda l: x_ref[l], *rest, n_cmp=n_cmp)


def compress_sample(cache, page_table, cw, *, pages_per_step):
    Bd, n_pages = page_table.shape
    page = cache.shape[1]
    nch = n_pages * page // CMP_STRIDE
    P = pages_per_step

    def pspec(p, s):
        return pl.BlockSpec((1, page, LANES), lambda b, j, pt: (pt[b, j * P + p], 0, s))

    out = pl.BlockSpec((1, nch, KV_W), lambda b, j, pt: (b, 0, 0))
    return pl.pallas_call(
        functools.partial(_compress_sample_kernel, n_cmp=nch - 1, pages_per_step=P),
        out_shape=[jax.ShapeDtypeStruct((Bd, nch, KV_W), F32)] * 2,
        grid_spec=pltpu.PrefetchScalarGridSpec(
            num_scalar_prefetch=1,
            grid=(Bd, n_pages // P),
            in_specs=[pspec(p, s) for p in range(P) for s in range(4)] + _const_specs(cw),
            out_specs=[out, out],
            scratch_shapes=[pltpu.VMEM((nch, 4 * KV_W), F32),
                            pltpu.VMEM((CMP_STRIDE, nch, 2 * KV_W), F32)],
        ),
        compiler_params=_cparams(("parallel", "arbitrary")),
        name="compress_sample",
    )(page_table, *([cache] * (4 * P)), *cw)


def _select_blocks(imp, wsel, blk, cur):
    i_hi = imp.astype(BF16)
    r1 = imp - i_hi.astype(F32)
    i_mid = r1.astype(BF16)
    i_lo = (r1 - i_mid.astype(F32)).astype(BF16)
    imp_sel = _dot(i_hi, wsel) + _dot(i_mid, wsel) + _dot(i_lo, wsel)
    forced = (blk == 0) | (blk == cur) | (blk == cur - 1)
    imp_sel = jnp.where(blk > cur, -jnp.inf, jnp.where(forced, jnp.inf, imp_sel))
    return _topk_mask_cols(imp_sel.T, N_SEL).T


def _tile_block_hits(chosen_q, k0, kt_size):
    n_sb = chosen_q.shape[1]
    blk = lax.broadcasted_iota(jnp.int32, (n_sb, kt_size), 0)
    key_blk = jnp.right_shift(k0 + lax.broadcasted_iota(jnp.int32, (n_sb, kt_size), 1), 6)
    expand = jnp.where(blk == key_blk, 1.0, 0.0).astype(BF16)
    return _dot(chosen_q, expand)


def _tile_block_mask(chosen_q, k0, kt_size):
    return _tile_block_hits(chosen_q, k0, kt_size) > 0.5


def _nsa_prompt_kernel(q_ref, gate_ref, kc_ref, vc_ref, ks_ref, vs_ref, kw_ref, vw_ref, wsel_ref,
                       o_ref, *, n_cmp, kt_size):
    n = pl.program_id(1)
    t0 = n * Q_BLOCK
    G, TQ = B_GROUP, Q_BLOCK
    R = G * TQ
    NC = kc_ref.shape[1]
    NSB = wsel_ref.shape[1]
    wsel = wsel_ref[...]
    gt = gate_ref[...]
    row_c = lax.broadcasted_iota(jnp.int32, (R, NC), 0)
    col_c = lax.broadcasted_iota(jnp.int32, (R, NC), 1)
    valid_c = (col_c * CMP_STRIDE + (CMP_LEN - 1) <= t0 + jnp.bitwise_and(row_c, TQ - 1)) & (col_c < n_cmp)
    bias_c = jnp.where(valid_c, 0.0, NEG)
    sees_cmp = t0 + jnp.bitwise_and(lax.broadcasted_iota(jnp.int32, (R, 1), 0), TQ - 1) >= CMP_LEN - 1
    blk = lax.broadcasted_iota(jnp.int32, (TQ, NSB), 1)
    cur = jnp.right_shift(t0 + lax.broadcasted_iota(jnp.int32, (TQ, NSB), 0), 6)
    WK = B_WINDOW + TQ
    w0 = pl.multiple_of(jnp.maximum(t0 - B_WINDOW, 0), TQ)
    wpos = w0 + lax.broadcasted_iota(jnp.int32, (R, WK), 1)
    rel = t0 + jnp.bitwise_and(lax.broadcasted_iota(jnp.int32, (R, WK), 0), TQ - 1) - wpos
    bias_w = jnp.where((rel >= 0) & (rel <= B_WINDOW), 0.0, NEG)
    n_kt = (t0 + TQ + kt_size - 1) // kt_size
    k_last = pl.multiple_of((n_kt - 1) * kt_size, kt_size)
    causal_last = jnp.where(k_last + lax.broadcasted_iota(jnp.int32, (TQ, kt_size), 1)
                            <= t0 + lax.broadcasted_iota(jnp.int32, (TQ, kt_size), 0), 0.0, NEG)
    ones_col = jnp.where(lax.broadcasted_iota(jnp.int32, (kt_size, HEAD_DIM), 1) == 0, 1.0, 0.0).astype(BF16)

    head_cols = [pl.ds(h * HEAD_DIM, HEAD_DIM) for h in range(B_KV_HEADS)]

    qbs, o_cs, chosens = [], [], []
    for h, hs in enumerate(head_cols):
        qs = jnp.concatenate([q_ref[:, pl.ds((h * G + g) * HEAD_DIM, HEAD_DIM)] for g in range(G)], axis=0)
        qs = qs * (SCALE * LOG2E)
        s_c = _dot_nt3(qs, kc_ref[0, :, hs]) + bias_c
        e_c = jnp.exp2(s_c - jnp.max(s_c, axis=-1, keepdims=True))
        p_c = e_c * jnp.where(sees_cmp, 1.0 / jnp.sum(e_c, axis=-1, keepdims=True), 0.0)
        imp = p_c[0:TQ]
        for g in range(1, G):
            imp = imp + p_c[g * TQ:(g + 1) * TQ]
        qbs.append(qs.astype(BF16))
        o_cs.append(_dot(p_c.astype(BF16), vc_ref[0, :, hs].astype(BF16)))
        chosens.append(_select_blocks(imp, wsel, blk, cur).astype(BF16))

    def sel_tile(h, k0, carry, extra_bias):
        m_i, acc = carry
        hs = head_cols[h]
        bias = (_tile_block_hits(chosens[h], k0, kt_size) - 1.0) * (-NEG)
        if extra_bias is not None:
            bias = bias + extra_bias
        s = _dot_nt(qbs[h], ks_ref[pl.ds(k0, kt_size), hs]).reshape(G, TQ, kt_size) + bias[None]
        s = s.reshape(R, kt_size)
        m_new = jnp.maximum(m_i, jnp.max(s, axis=-1, keepdims=True))
        p = jnp.exp2(s - m_new).astype(BF16)
        v1 = jnp.concatenate([vs_ref[pl.ds(k0, kt_size), hs], ones_col], axis=1)
        return m_new, jnp.exp2(m_i - m_new) * acc + _dot(p, v1)

    def sel_body(kt, carries):
        k0 = pl.multiple_of(kt * kt_size, kt_size)
        return tuple(sel_tile(h, k0, carries[h], None) for h in range(B_KV_HEADS))

    init = tuple((jnp.full((R, 1), NEG, F32), jnp.zeros((R, 2 * HEAD_DIM), F32)) for _ in range(B_KV_HEADS))
    carries = lax.fori_loop(0, n_kt - 1, sel_body, init)

    outs = []
    for h, hs in enumerate(head_cols):
        _, acc_s = sel_tile(h, k_last, carries[h], causal_last)
        o_s = acc_s[:, :HEAD_DIM] / acc_s[:, HEAD_DIM:HEAD_DIM + 1]
        s_w = _dot_nt(qbs[h], kw_ref[pl.ds(w0, WK), hs]) + bias_w
        e_w = jnp.exp2(s_w - jnp.max(s_w, axis=-1, keepdims=True))
        p_w = e_w * (1.0 / jnp.sum(e_w, axis=-1, keepdims=True))
        o_w = _dot(p_w.astype(BF16), vw_ref[pl.ds(w0, WK), hs])
        o_c = o_cs[h]
        for g in range(G):
            rs = slice(g * TQ, (g + 1) * TQ)
            c = 3 * (h * G + g)
            outs.append(gt[:, c:c + 1] * o_c[rs] + gt[:, c + 1:c + 2] * o_s[rs] + gt[:, c + 2:c + 3] * o_w[rs])
    o_ref[...] = jnp.concatenate(outs, axis=1)


def nsa_prompt(q, gate, kc, vc, kvb, wsel, B, S, *, kt_size):
    nq = q.shape[1]
    nqb = S // Q_BLOCK
    NC = kc.shape[1]
    comp = lambda c: pl.BlockSpec((S, KV_W), lambda b, n: (b, c))
    cmp_spec = pl.BlockSpec((1, NC, KV_W), lambda b, n: (b, 0, 0))
    return pl.pallas_call(
        functools.partial(_nsa_prompt_kernel, n_cmp=S // CMP_STRIDE - 1, kt_size=kt_size),
        out_shape=jax.ShapeDtypeStruct((B * S, nq), F32),
        grid=(B, nqb),
        in_specs=[pl.BlockSpec((Q_BLOCK, nq), lambda b, n: (b * nqb + n, 0)),
                  pl.BlockSpec((Q_BLOCK, LANES), lambda b, n: (b * nqb + n, 0)),
                  cmp_spec, cmp_spec, comp(2), comp(3), comp(4), comp(5),
                  pl.BlockSpec(wsel.shape, lambda b, n: (0, 0))],
        out_specs=pl.BlockSpec((Q_BLOCK, nq), lambda b, n: (b * nqb + n, 0)),
        compiler_params=_cparams(("parallel", "arbitrary")),
        name="nsa_prompt",
    )(q, gate, kc, vc, kvb, kvb, kvb, kvb, wsel)


def _nsa_step_kernel(pt_ref, q_ref, gate_ref, kc_ref, vc_ref, new_ref, win_ref, wsel_ref, *rest,
                     past, n_cmp, pages_per_step):
    P = pages_per_step
    pages = rest[:P]
    o_ref, st_ref, a_ref, ch_ref, m_ref, l_ref, acc_ref, oc_ref = rest[P:]
    del pt_ref
    j = pl.program_id(1)
    G, KVH = B_GROUP, B_KV_HEADS
    Sd = q_ref.shape[0]
    R = B_HEADS * Sd
    page = pages[0].shape[1]
    kt = P * page
    row_head = lambda shape: lax.broadcasted_iota(jnp.int32, shape, 0) // (G * Sd)
    own = lambda width: row_head((R, width)) == lax.broadcasted_iota(jnp.int32, (R, width), 1) // HEAD_DIM

    def diag(x):
        rows = G * Sd
        return jnp.concatenate([x[h * rows:(h + 1) * rows, h * HEAD_DIM:(h + 1) * HEAD_DIM]
                                for h in range(KVH)], axis=0)

    @pl.when(j == 0)
    def _():
        q = q_ref[...]
        a = jnp.concatenate([q[:, hd * HEAD_DIM:(hd + 1) * HEAD_DIM] for hd in range(B_HEADS)], axis=0)
        a = jnp.where(own(KV_W), jnp.concatenate([a] * KVH, axis=1), 0.0)
        a_ref[...] = a.astype(BF16)
        NC = kc_ref.shape[1]
        valid_c = lax.broadcasted_iota(jnp.int32, (R, NC), 1) < n_cmp
        p_c = _softmax_rows(_dot_nt3(a, kc_ref[0]) * SCALE, valid_c)
        oc_ref[...] = diag(_dot(p_c.astype(BF16), vc_ref[0].astype(BF16)))
        pooled = jnp.sum(p_c.reshape(KVH, G, Sd, NC), axis=1, keepdims=True)
        imp = jnp.broadcast_to(pooled, (KVH, G, Sd, NC)).reshape(R, NC)
        NSB = wsel_ref.shape[1]
        blk = lax.broadcasted_iota(jnp.int32, (R, NSB), 1)
        srow = jnp.bitwise_and(lax.broadcasted_iota(jnp.int32, (R, NSB), 0), Sd - 1)
        cur = jnp.right_shift(past + srow, 6)
        ch_ref[...] = _select_blocks(imp, wsel_ref[...], blk, cur).astype(BF16)
        m_ref[...] = jnp.full(m_ref.shape, NEG, F32)
        l_ref[...] = jnp.zeros(l_ref.shape, F32)
        acc_ref[...] = jnp.zeros(acc_ref.shape, F32)

    a = a_ref[...]
    ks = jnp.concatenate([pg[0, :, pl.ds(0, KV_W)] for pg in pages], axis=0).astype(BF16)
    vs = jnp.concatenate([pg[0, :, pl.ds(KV_W, KV_W)] for pg in pages], axis=0).astype(BF16)
    mk = _tile_block_mask(ch_ref[...], j * kt, kt)
    m_n, l_n, acc_n = _online_update((m_ref[...], l_ref[...], acc_ref[...]), _dot_nt(a, ks) * SCALE, mk, vs)
    m_ref[...] = m_n
    l_ref[...] = l_n
    acc_ref[...] = acc_n

    @pl.when(j == pl.num_programs(1) - 1)
    def _():
        new = _pad_rows(new_ref[...], LANES)
        comp = lambda c: new[:, c * KV_W:(c + 1) * KV_W].astype(BF16)
        srow = jnp.bitwise_and(lax.broadcasted_iota(jnp.int32, (R, LANES), 0), Sd - 1)
        scol = lax.broadcasted_iota(jnp.int32, (R, LANES), 1)
        causal_new = (scol <= srow) & (scol < Sd)
        nblk = past // SEL_BLOCK
        in_new = ch_ref[:, nblk:nblk + 1].astype(F32) > 0.5
        _, l_s, acc_s = _online_update((m_n, l_n, acc_n), _dot_nt(a, comp(2)) * SCALE, in_new & causal_new, comp(3))
        o_s = diag(acc_s) / jnp.where(l_s > 0, l_s, 1.0)

        W = win_ref.shape[1]
        wrow = jnp.bitwise_and(lax.broadcasted_iota(jnp.int32, (R, W), 0), Sd - 1)
        rel = W + wrow - lax.broadcasted_iota(jnp.int32, (R, W), 1)
        carry = _online_update(_online_init(R, KV_W), _dot_nt(a, win_ref[0, :, pl.ds(0, KV_W)].astype(BF16)) * SCALE,
                               (rel >= 0) & (rel <= B_WINDOW), win_ref[0, :, pl.ds(KV_W, KV_W)].astype(BF16))
        _, l_w, acc_w = _online_update(carry, _dot_nt(a, comp(4)) * SCALE, causal_new, comp(5))
        o_w = diag(acc_w) / jnp.where(l_w > 0, l_w, 1.0)
        o_c = oc_ref[...]
        gt = gate_ref[...]
        outs = []
        for hd in range(B_HEADS):
            rs = slice(hd * Sd, (hd + 1) * Sd)
            c = 3 * hd
            outs.append(gt[:, c:c + 1] * o_c[rs] + gt[:, c + 1:c + 2] * o_s[rs] + gt[:, c + 2:c + 3] * o_w[rs])
        o_ref[...] = jnp.concatenate(outs, axis=1)
        st_ref[0, pl.ds(0, W - Sd), :] = win_ref[0, pl.ds(Sd, W - Sd), :]
        st_ref[0, pl.ds(W - Sd, Sd), :] = new_ref[:, pl.ds(4 * KV_W, 2 * KV_W)]


def nsa_step(q, gate, kc, vc, kv, win, cache, page_table, wsel, Tp, Bd, Sd, *, pages_per_step):
    nq = q.shape[1]
    n_pages = page_table.shape[1]
    page = cache.shape[1]
    past = n_pages * page
    W = win.shape[1]
    NC = kc.shape[1]
    P = pages_per_step
    r0 = Tp // Sd
    R = B_HEADS * Sd
    rows = lambda w: pl.BlockSpec((Sd, w), lambda b, j, pt: (r0 + b, 0))
    per_b = lambda s1, s2: pl.BlockSpec((1, s1, s2), lambda b, j, pt: (b, 0, 0))
    pspec = lambda p: pl.BlockSpec((1, page, 2 * KV_W), lambda b, j, pt: (pt[b, j * P + p], 0, 1))
    return pl.pallas_call(
        functools.partial(_nsa_step_kernel, past=past, n_cmp=past // CMP_STRIDE - 1, pages_per_step=P),
        out_shape=[jax.ShapeDtypeStruct((Bd * Sd, nq), F32), jax.ShapeDtypeStruct(win.shape, F32)],
        grid_spec=pltpu.PrefetchScalarGridSpec(
            num_scalar_prefetch=1,
            grid=(Bd, n_pages // P),
            in_specs=[rows(nq), rows(LANES), per_b(NC, KV_W), per_b(NC, KV_W), rows(kv.shape[1]),
                      per_b(W, 2 * KV_W), pl.BlockSpec(wsel.shape, lambda b, j, pt: (0, 0))]
                     + [pspec(p) for p in range(P)],
            out_specs=[pl.BlockSpec((Sd, nq), lambda b, j, pt: (b, 0)), per_b(W, 2 * KV_W)],
            scratch_shapes=[pltpu.VMEM((R, KV_W), BF16), pltpu.VMEM((R, wsel.shape[1]), BF16),
                            pltpu.VMEM((R, 1), F32), pltpu.VMEM((R, 1), F32), pltpu.VMEM((R, KV_W), F32),
                            pltpu.VMEM((R, HEAD_DIM), F32)],
        ),
        compiler_params=_cparams(("parallel", "arbitrary")),
        name="nsa_step",
    )(page_table, q, gate, kc, vc, kv, win, wsel, *([cache] * P))


def _rope_tables(pos, width):
    half = ROT_DIM // 2
    inv = jnp.exp(-jnp.log(jnp.float32(ROPE_THETA)) * jnp.arange(half, dtype=F32) * (2.0 / ROT_DIM))
    ang = pos.astype(F32)[:, None] * inv[None, :]
    cos, sin = jnp.cos(ang), jnp.sin(ang)
    T = pos.shape[0]
    z8 = jnp.zeros((T, half), F32)
    rest = HEAD_DIM - ROT_DIM
    c = jnp.concatenate([cos, cos, jnp.ones((T, rest), F32)], axis=1)
    sa = jnp.concatenate([-sin, z8, jnp.zeros((T, rest), F32)], axis=1)
    sb = jnp.concatenate([z8, sin, jnp.zeros((T, rest), F32)], axis=1)
    reps = width // HEAD_DIM
    return tuple(jnp.tile(t, (1, reps)) for t in (c, sa, sb))


def _overlap_weights(n_cmp, n_cmp_pad, n_sb, n_sb_pad):
    j = jnp.arange(n_sb)
    cidx = (SEL_BLOCK // CMP_STRIDE) * j[:, None] - 1 + jnp.arange(len(SEL_OVERLAP_W))[None, :]
    wts = jnp.where((cidx >= 0) & (cidx < n_cmp), jnp.asarray(SEL_OVERLAP_W, F32), 0.0)
    onehot = (jnp.clip(cidx, 0, n_cmp - 1)[:, :, None] == jnp.arange(n_cmp_pad)[None, None, :]).astype(F32)
    wsel = jnp.einsum('jo,jon->nj', wts, onehot)
    return jnp.pad(wsel, ((0, 0), (0, n_sb_pad - n_sb))).astype(BF16)


def _pick(n, cands):
    for c in cands:
        if n % c == 0:
            return c
    return n


def _pad_to(n, m):
    return -(-n // m) * m


def kernel(x_prompt, x_sample, cache_a_g1, cache_a_g2, cache_a_g3, cache_b_kv, cache_b_win, page_table, g_mix, g_ffn, w_in_a, w_o_a, g_kv, w_kv_b, w_cmp1, w_cmp2, pe_cmp, w_qg_b, b_gate_b, w_o_b, w_peer_q, peer_subkeys, peer_u, peer_v, g_final):
    B, S, D = x_prompt.shape
    Bd, Sd, _ = x_sample.shape
    Tp, Ts = B * S, Bd * Sd
    T = Tp + Ts
    n_phys, page = cache_b_kv.shape[:2]
    n_pages = page_table.shape[1]
    past = n_pages * page
    hw = A_HEADS * HEAD_DIM
    tm_big = _pick(T, (1280, 640, 256, 128, 64, 32, 16, 8))
    tm_mid = _pick(T, (640, 256, 128, 64, 32, 16, 8))
    tm_peer = _pick(T, (256, 128, 64, 32, 16, 8))
    tm_exp = _pick(T, (520, 512, 256, 128, 64, 32, 16, 8))
    tm_rows = _pick(math.gcd(Tp, Ts), (256, 128, 64, 32, 16, 8))
    pages_per_step = _pick(n_pages, (8, 4, 2, 1))

    x = jnp.concatenate([x_prompt.reshape(Tp, D), x_sample.reshape(Ts, D)], axis=0)
    pos_p = jnp.arange(S, dtype=jnp.int32)
    pos_s = past + jnp.arange(Sd, dtype=jnp.int32)
    pos = jnp.concatenate([jnp.tile(pos_p, B), jnp.tile(pos_s, Bd)])
    tab512 = _rope_tables(pos, 512)
    tab256 = tuple(t[:, :KV_W] for t in tab512)

    n_grp = len(A_GROUPS)
    proj = rms_matmul(x, g_mix[0], w_in_a[0].astype(BF16), tm=tm_big, tn=hw, epilogue="rope",
                      flags=jnp.asarray([1, 1, 0] * n_grp, jnp.int32), tables=tab512)
    caches_a = (cache_a_g1, cache_a_g2, cache_a_g3)
    op, lp, os_, ls_, a_prompt, a_sample = [], [], [], [], [], []
    for g, (win, dil) in enumerate(A_GROUPS):
        o_g, l_g = band_attention(proj, g, win, dil, B, S)
        cache = caches_a[g]
        o_s, l_s, state = a_step(proj, g, win, dil, cache[0].reshape(Bd, cache.shape[2], 2 * hw), Tp, Bd, Sd)
        a_sample.append(state.reshape(cache.shape))
        keep = min(win, S)
        kv_cols = proj[:Tp].reshape(B, S, n_grp, 3 * hw)[:, S - keep:, g, hw:]
        a_prompt.append(kv_cols.reshape(1, B, keep, 2, A_HEADS, HEAD_DIM))
        op.append(o_g)
        lp.append(l_g)
        os_.append(o_s)
        ls_.append(l_s)
    x = merge_matmul_residual(op + lp, os_ + ls_, w_o_a[0].astype(BF16), x, tm=tm_rows)
    x = peer_layer(x, g_ffn[0], w_peer_q[0], peer_subkeys[0], peer_u[0], peer_v[0],
                   tm_route=tm_peer, tm_exp=tm_exp, chunk=1024)

    kv, kvb = rms_matmul(x, g_kv, w_kv_b.astype(BF16), tm=tm_big, tn=KV_W, epilogue="rope",
                         flags=jnp.asarray([0, 0, 1, 0, 1, 0], jnp.int32), tables=tab256, emit_bf16=True)
    kv_p = kv[:Tp].reshape(B, S, 6, B_KV_HEADS, HEAD_DIM)
    bkv_prompt = kv_p[:, :, :4]
    bkv_sample = kv[Tp:, :4 * KV_W].reshape(Bd, Sd, 4, B_KV_HEADS, HEAD_DIM)
    keep_w = min(B_WINDOW, S)
    bwin_prompt = kv_p[:, S - keep_w:, 4:]
    cache2d = cache_b_kv.reshape(n_phys, page, 4 * KV_W)
    kc_p, vc_p = compress_prompt(kv, _compress_weights(w_cmp1, w_cmp2, pe_cmp, S // CMP_STRIDE), B, S)
    kc_s, vc_s = compress_sample(cache2d, page_table, _compress_weights(w_cmp1, w_cmp2, pe_cmp, past // CMP_STRIDE),
                                 pages_per_step=pages_per_step)

    nq = B_HEADS * HEAD_DIM
    w_qg = w_qg_b[0]
    q = rms_matmul(x, g_mix[1], w_qg[:, :nq].astype(BF16), tm=tm_big, tn=512, epilogue="rope",
                   flags=jnp.ones((nq // 512,), jnp.int32), tables=tab512)
    ngate = 3 * B_HEADS
    w_gate = jnp.pad(w_qg[:, nq:], ((0, 0), (0, LANES - ngate))).astype(BF16)
    b_gate = jnp.pad(b_gate_b[0], (0, LANES - ngate))
    gate = rms_matmul(x, g_mix[1], w_gate, tm=tm_big, tn=LANES, epilogue="sigmoid", bias=b_gate)

    n_cmp_p = S // CMP_STRIDE - 1
    wsel_p = _overlap_weights(n_cmp_p, S // CMP_STRIDE, S // SEL_BLOCK, _pad_to(S // SEL_BLOCK, LANES))
    o_b_p = nsa_prompt(q, gate, kc_p, vc_p, kvb, wsel_p, B, S, kt_size=min(1024, S))
    n_cmp_s = past // CMP_STRIDE - 1
    n_sb_s = -(-(past + Sd) // SEL_BLOCK)
    wsel_s = _overlap_weights(n_cmp_s, past // CMP_STRIDE, n_sb_s, _pad_to(n_sb_s, LANES))
    Wb = cache_b_win.shape[1]
    o_b_s, bwin_sample = nsa_step(q, gate, kc_s, vc_s, kv, cache_b_win.reshape(Bd, Wb, 2 * KV_W), cache2d,
                                  page_table, wsel_s, Tp, Bd, Sd, pages_per_step=pages_per_step)
    x = matmul_residual(o_b_p, o_b_s, w_o_b[0].astype(BF16), x, tm=tm_rows)
    x = peer_layer(x, g_ffn[1], w_peer_q[1], peer_subkeys[1], peer_u[1], peer_v[1],
                   tm_route=tm_peer, tm_exp=tm_exp, chunk=1024)
    y = rmsnorm_rows(x, g_final, tm=tm_mid)

    return (y[:Tp].reshape(B, S, D), y[Tp:].reshape(Bd, Sd, D),
            a_prompt[0], a_sample[0], a_prompt[1], a_sample[1], a_prompt[2], a_sample[2],
            bkv_prompt, bkv_sample, bwin_prompt, bwin_sample.reshape(cache_b_win.shape))
```

```python
import functools
import math

import jax
import jax.numpy as jnp
from jax import lax
from jax.experimental import pallas as pl
from jax.experimental.pallas import tpu as pltpu

F32 = jnp.float32
BF16 = jnp.bfloat16

HEAD_DIM = 64
ROT_DIM = HEAD_DIM // 4
ROPE_THETA = 500000.0
NORM_EPS = 1e-6
A_GROUPS = ((128, 1), (512, 4), (2048, 16))
A_HEADS = 8
A_BLOCK = 128
B_HEADS = 16
B_KV_HEADS = 4
B_GROUP = B_HEADS // B_KV_HEADS
CMP_LEN = 32
CMP_STRIDE = 16
SEL_BLOCK = 64
N_SEL = 16
SEL_OVERLAP_W = (1.0, 2.0, 2.0, 2.0, 1.0)
B_WINDOW = 512
Q_BLOCK = 128
PEER_HEADS = 8
PEER_KEYS = 128
PEER_QDIM = 256
PEER_TOPK = 16

LANES = 128
NEG = -1e30
VMEM_LIMIT = 50 * 1024 * 1024
G_PITCH = 72
KV_W = B_KV_HEADS * HEAD_DIM
SCALE = HEAD_DIM ** -0.5
LOG2E = 1.4426950408889634

_NT = (((1,), (1,)), ((), ()))


def _cparams(sem):
    return pltpu.CompilerParams(dimension_semantics=sem, vmem_limit_bytes=VMEM_LIMIT)


def _split2(a):
    hi = a.astype(BF16)
    lo = (a - hi.astype(F32)).astype(BF16)
    return hi, lo


def _dot(a, b):
    return jnp.dot(a, b, preferred_element_type=F32)


def _dot_nt(a, b):
    return lax.dot_general(a, b, _NT, preferred_element_type=F32)


def _dot_nt3(a, b):
    ah, al = _split2(a)
    bh, bl = _split2(b)
    return _dot_nt(ah, bh) + _dot_nt(ah, bl) + _dot_nt(al, bh)


def _softmax_rows(s, valid):
    s = jnp.where(valid, s, NEG)
    m = jnp.max(s, axis=-1, keepdims=True)
    e = jnp.where(valid, jnp.exp(s - m), 0.0)
    den = jnp.sum(e, axis=-1, keepdims=True)
    return e / jnp.where(den > 0, den, 1.0)


def _rope_lanes(y, cos, sa, sb):
    n = y.shape[-1]
    return y * cos + pltpu.roll(y, 8, 1) * sb + pltpu.roll(y, n - 8, 1) * sa


def _rms_norm_rows(x, g):
    ms = jnp.mean(x * x, axis=-1, keepdims=True)
    return x * lax.rsqrt(ms + NORM_EPS) * g


def _rms_mm_kernel(flags_ref, x_ref, g_ref, w_ref, *rest, epilogue, emit_bf16):
    n_out = 2 if emit_bf16 else 1
    extra, outs, xn_ref = rest[:len(rest) - n_out - 1], rest[len(rest) - n_out - 1:-1], rest[-1]
    j = pl.program_id(1)

    @pl.when(j == 0)
    def _():
        xn_ref[...] = _rms_norm_rows(x_ref[...], g_ref[...]).astype(BF16)

    def store(val):
        outs[0][...] = val
        if emit_bf16:
            outs[1][...] = val.astype(BF16)

    y = _dot(xn_ref[...], w_ref[...])
    if epilogue == "rope":
        cos_ref, sa_ref, sb_ref = extra
        f = flags_ref[j]

        @pl.when(f == 0)
        def _():
            store(y)

        @pl.when(f != 0)
        def _():
            store(_rope_lanes(y, cos_ref[...], sa_ref[...], sb_ref[...]))
    elif epilogue == "sigmoid":
        store(jax.nn.sigmoid(y + extra[0][...]))
    else:
        store(y)


def rms_matmul(x, g, w, *, tm, tn, epilogue="none", flags=None, tables=None, bias=None, emit_bf16=False):
    M, D = x.shape
    N = w.shape[1]
    assert M % tm == 0 and N % tn == 0
    nj = N // tn
    if flags is None:
        flags = jnp.zeros((nj,), jnp.int32)
    in_specs = [
        pl.BlockSpec((tm, D), lambda i, j, f: (i, 0)),
        pl.BlockSpec((1, D), lambda i, j, f: (0, 0)),
        pl.BlockSpec((D, tn), lambda i, j, f: (0, j)),
    ]
    args = [x, g.reshape(1, D), w]
    if epilogue == "rope":
        for t in tables:
            in_specs.append(pl.BlockSpec((tm, tn), lambda i, j, f: (i, 0)))
            args.append(t)
    elif epilogue == "sigmoid":
        in_specs.append(pl.BlockSpec((1, tn), lambda i, j, f: (0, j)))
        args.append(bias.reshape(1, N))
    out_spec = pl.BlockSpec((tm, tn), lambda i, j, f: (i, j))
    out_shape = [jax.ShapeDtypeStruct((M, N), F32)]
    if emit_bf16:
        out_shape.append(jax.ShapeDtypeStruct((M, N), BF16))
    res = pl.pallas_call(
        functools.partial(_rms_mm_kernel, epilogue=epilogue, emit_bf16=emit_bf16),
        out_shape=out_shape,
        grid_spec=pltpu.PrefetchScalarGridSpec(
            num_scalar_prefetch=1,
            grid=(M // tm, nj),
            in_specs=in_specs,
            out_specs=[out_spec] * len(out_shape),
            scratch_shapes=[pltpu.VMEM((tm, D), BF16)],
        ),
        compiler_params=_cparams(("parallel", "arbitrary")),
        name="rms_matmul_" + epilogue,
    )(flags, *args)
    return res if emit_bf16 else res[0]


def _two_source_specs(n_arr, tm, K, n_p):
    p_spec = pl.BlockSpec((tm, K), lambda i: (jnp.minimum(i, n_p - 1), 0))
    s_spec = pl.BlockSpec((tm, K), lambda i: (jnp.maximum(i - n_p, 0), 0))
    return [p_spec] * n_arr + [s_spec] * n_arr


def _mm_res_kernel(ap_ref, as_ref, w_ref, r_ref, o_ref, *, n_p):
    a = jnp.where(pl.program_id(0) >= n_p, as_ref[...], ap_ref[...])
    o_ref[...] = r_ref[...] + _dot(a.astype(BF16), w_ref[...])


def matmul_residual(a_p, a_s, w, res, *, tm):
    M, N = res.shape
    K = a_p.shape[1]
    n_p = a_p.shape[0] // tm
    assert a_p.shape[0] % tm == 0 and a_s.shape[0] % tm == 0
    return pl.pallas_call(
        functools.partial(_mm_res_kernel, n_p=n_p),
        out_shape=jax.ShapeDtypeStruct((M, N), F32),
        grid=(M // tm,),
        in_specs=_two_source_specs(1, tm, K, n_p) + [pl.BlockSpec((K, N), lambda i: (0, 0)),
                                                    pl.BlockSpec((tm, N), lambda i: (i, 0))],
        out_specs=pl.BlockSpec((tm, N), lambda i: (i, 0)),
        compiler_params=_cparams(("parallel",)),
        name="matmul_residual",
    )(a_p, a_s, w, res)


def _merge_mm_res_kernel(*refs, n_p):
    prompt, sample, (w_ref, r_ref, out_ref) = refs[:6], refs[6:12], refs[12:]
    is_s = pl.program_id(0) >= n_p
    o1, o2, o3, a1, a2, a3 = [jnp.where(is_s, s[...], p[...]) for p, s in zip(prompt, sample)]
    m = jnp.maximum(jnp.maximum(a1, a2), a3)
    e1, e2, e3 = jnp.exp(a1 - m), jnp.exp(a2 - m), jnp.exp(a3 - m)
    den = e1 + e2 + e3
    a = (e1 / den) * o1 + (e2 / den) * o2 + (e3 / den) * o3
    out_ref[...] = r_ref[...] + _dot(a.astype(BF16), w_ref[...])


def merge_matmul_residual(ol_p, ol_s, w, res, *, tm):
    M, N = res.shape
    K = ol_p[0].shape[1]
    n_p = ol_p[0].shape[0] // tm
    assert ol_p[0].shape[0] % tm == 0 and ol_s[0].shape[0] % tm == 0
    return pl.pallas_call(
        functools.partial(_merge_mm_res_kernel, n_p=n_p),
        out_shape=jax.ShapeDtypeStruct((M, N), F32),
        grid=(M // tm,),
        in_specs=_two_source_specs(6, tm, K, n_p) + [pl.BlockSpec((K, N), lambda i: (0, 0)),
                                                    pl.BlockSpec((tm, N), lambda i: (i, 0))],
        out_specs=pl.BlockSpec((tm, N), lambda i: (i, 0)),
        compiler_params=_cparams(("parallel",)),
        name="merge_matmul_residual",
    )(*ol_p, *ol_s, w, res)


def _rmsnorm_kernel(x_ref, g_ref, o_ref):
    o_ref[...] = _rms_norm_rows(x_ref[...], g_ref[...])


def rmsnorm_rows(x, g, *, tm):
    M, D = x.shape
    return pl.pallas_call(
        _rmsnorm_kernel,
        out_shape=jax.ShapeDtypeStruct((M, D), F32),
        grid=(M // tm,),
        in_specs=[pl.BlockSpec((tm, D), lambda i: (i, 0)), pl.BlockSpec((1, D), lambda i: (0, 0))],
        out_specs=pl.BlockSpec((tm, D), lambda i: (i, 0)),
        compiler_params=_cparams(("parallel",)),
        name="rmsnorm",
    )(x, g.reshape(1, D))


def _online_update(carry, s, mask, v):
    m_i, l_i, acc = carry
    s = jnp.where(mask, s, NEG)
    m_new = jnp.maximum(m_i, jnp.max(s, axis=-1, keepdims=True))
    alpha = jnp.exp(m_i - m_new)
    p = jnp.where(mask, jnp.exp(s - m_new), 0.0)
    return (m_new, alpha * l_i + jnp.sum(p, axis=-1, keepdims=True), alpha * acc + _dot(p.astype(BF16), v))


def _online_init(rows, width):
    return (jnp.full((rows, 1), NEG, F32), jnp.zeros((rows, 1), F32), jnp.zeros((rows, width), F32))


def _attend_heads(q, segments, n_heads):
    nq = q.shape[0]
    outs, lses = [], []
    for h in range(n_heads):
        qh = q[:, h * HEAD_DIM:(h + 1) * HEAD_DIM].astype(BF16)
        carry = _online_init(nq, HEAD_DIM)
        for k_rows, v_rows, mask in segments:
            carry = _online_update(carry, _dot_nt(qh, k_rows(h)) * SCALE, mask, v_rows(h))
        m, den, acc = carry
        outs.append(acc / den)
        lses.append(jnp.broadcast_to(m + jnp.log(den), (nq, HEAD_DIM)))
    return jnp.concatenate(outs, axis=1), jnp.concatenate(lses, axis=1)


def _pad_rows(x, n):
    return jnp.concatenate([x, jnp.zeros((n - x.shape[0], x.shape[1]), x.dtype)], axis=0)


def _band_attn_kernel(q_ref, kp_ref, kc_ref, vp_ref, vc_ref, o_ref, l_ref, *, nw):
    n = pl.program_id(2)
    kp, kc, vp, vc = kp_ref[...], kc_ref[...], vp_ref[...], vc_ref[...]
    i = lax.broadcasted_iota(jnp.int32, (A_BLOCK, 2 * A_BLOCK), 0)
    j = lax.broadcasted_iota(jnp.int32, (A_BLOCK, 2 * A_BLOCK), 1)
    rel = A_BLOCK + i - j
    mask = (rel >= 0) & (rel <= nw) & ((j >= A_BLOCK) | (n > 0))

    def rows(prev, cur):
        return lambda h: jnp.concatenate([prev[:, h * HEAD_DIM:(h + 1) * HEAD_DIM],
                                          cur[:, h * HEAD_DIM:(h + 1) * HEAD_DIM]], axis=0).astype(BF16)

    o, l = _attend_heads(q_ref[...], [(rows(kp, kc), rows(vp, vc), mask)], A_HEADS)
    o_ref[...] = o
    l_ref[...] = l


def band_attention(proj, g, window, dil, B, S):
    T, ncol = proj.shape
    L = S // dil
    nb = L // A_BLOCK
    hw = A_HEADS * HEAD_DIM
    per_res = ncol // hw
    view = proj.reshape(T // dil, dil * ncol)

    def spec(c, prev):
        def imap(b, r, n):
            nn = jnp.maximum(n - 1, 0) if prev else n
            return (b * nb + nn, r * per_res + g * 3 + c)
        return pl.BlockSpec((A_BLOCK, hw), imap)

    out_spec = pl.BlockSpec((A_BLOCK, hw), lambda b, r, n: (b * nb + n, r))
    o, l = pl.pallas_call(
        functools.partial(_band_attn_kernel, nw=window // dil),
        out_shape=[jax.ShapeDtypeStruct((B * L, dil * hw), F32)] * 2,
        grid=(B, dil, nb),
        in_specs=[spec(0, False), spec(1, True), spec(1, False), spec(2, True), spec(2, False)],
        out_specs=[out_spec, out_spec],
        compiler_params=_cparams(("parallel", "parallel", "arbitrary")),
        name=f"band_attention_d{dil}",
    )(view, view, view, view, view)
    return o.reshape(B * S, hw), l.reshape(B * S, hw)


def _a_step_kernel(q_ref, k_ref, v_ref, c_ref, o_ref, l_ref, s_ref, *, win, dil):
    W = c_ref.shape[1]
    Sd = q_ref.shape[0]
    hw = A_HEADS * HEAD_DIM
    knew, vnew = k_ref[...], v_ref[...]
    s_ref[0, pl.ds(0, W - Sd), :] = c_ref[0, pl.ds(Sd, W - Sd), :]
    s_ref[0, pl.ds(W - Sd, Sd), pl.ds(0, hw)] = knew
    s_ref[0, pl.ds(W - Sd, Sd), pl.ds(hw, hw)] = vnew

    def on_stride(dist):
        return (dist >= 0) & (dist <= win) & (jnp.bitwise_and(dist, dil - 1) == 0)

    dist_buf = (W + lax.broadcasted_iota(jnp.int32, (Sd, W), 0)) - lax.broadcasted_iota(jnp.int32, (Sd, W), 1)
    ncol = lax.broadcasted_iota(jnp.int32, (Sd, LANES), 1)
    dist_new = lax.broadcasted_iota(jnp.int32, (Sd, LANES), 0) - ncol
    knew_p, vnew_p = _pad_rows(knew, LANES), _pad_rows(vnew, LANES)
    buf_rows = lambda off: lambda h: c_ref[0, :, pl.ds(off + h * HEAD_DIM, HEAD_DIM)].astype(BF16)
    new_rows = lambda x: lambda h: x[:, h * HEAD_DIM:(h + 1) * HEAD_DIM].astype(BF16)
    segments = [(buf_rows(0), buf_rows(hw), on_stride(dist_buf)),
                (new_rows(knew_p), new_rows(vnew_p), on_stride(dist_new) & (ncol < Sd))]
    o, l = _attend_heads(q_ref[...], segments, A_HEADS)
    o_ref[...] = o
    l_ref[...] = l


def a_step(proj, g, window, dil, cache, Tp, Bd, Sd):
    assert dil & (dil - 1) == 0
    W = cache.shape[1]
    hw = A_HEADS * HEAD_DIM
    r0 = Tp // Sd
    col = lambda c: pl.BlockSpec((Sd, hw), lambda b: (r0 + b, g * 3 + c))
    row_out = pl.BlockSpec((Sd, hw), lambda b: (b, 0))
    return pl.pallas_call(
        functools.partial(_a_step_kernel, win=window, dil=dil),
        out_shape=[jax.ShapeDtypeStruct((Bd * Sd, hw), F32), jax.ShapeDtypeStruct((Bd * Sd, hw), F32),
                   jax.ShapeDtypeStruct(cache.shape, F32)],
        grid=(Bd,),
        in_specs=[col(0), col(1), col(2), pl.BlockSpec((1, W, 2 * hw), lambda b: (b, 0, 0))],
        out_specs=[row_out, row_out, pl.BlockSpec((1, W, 2 * hw), lambda b: (b, 0, 0))],
        compiler_params=_cparams(("parallel",)),
        name=f"a_step_d{dil}",
    )(proj, proj, proj, cache)


def _topk_cols(s, k, payload=None):
    N = s.shape[0]
    iota = lax.broadcasted_iota(jnp.int32, s.shape, 0).astype(F32)
    vals, picks = [], []
    for _ in range(k):
        m = jnp.max(s, axis=0, keepdims=True)
        idx = jnp.min(jnp.where(s == m, iota, float(N)), axis=0, keepdims=True)
        sel = iota == idx
        vals.append(m)
        if payload is None:
            picks.append(idx)
        else:
            picks.append(jnp.max(jnp.where(sel, payload, -1.0), axis=0, keepdims=True))
        s = jnp.where(sel, -jnp.inf, s)
    return vals, picks


def _topk_cols_many(arrs, k):
    N = arrs[0].shape[0]
    iota = lax.broadcasted_iota(jnp.int32, arrs[0].shape, 0).astype(F32)
    arrs = list(arrs)
    out = [([], []) for _ in arrs]
    for _ in range(k):
        for n, s in enumerate(arrs):
            m = jnp.max(s, axis=0, keepdims=True)
            idx = jnp.min(jnp.where(s == m, iota, float(N)), axis=0, keepdims=True)
            out[n][0].append(m)
            out[n][1].append(idx)
            arrs[n] = jnp.where(iota == idx, -jnp.inf, s)
    return out


def _topk_mask_cols(s, k):
    N = s.shape[0]
    iota = lax.broadcasted_iota(jnp.int32, s.shape, 0).astype(F32)
    chosen = jnp.zeros(s.shape, F32)
    for _ in range(k):
        m = jnp.max(s, axis=0, keepdims=True)
        idx = jnp.min(jnp.where(s == m, iota, float(N)), axis=0, keepdims=True)
        sel = iota == idx
        chosen = jnp.where(sel, 1.0, chosen)
        s = jnp.where(sel, -jnp.inf, s)
    return chosen


def _peer_route_kernel(x_ref, g_ref, wqh_ref, wql_ref, skh_ref, skl_ref,
                       hn_ref, e_ref, w_ref, xh_ref, xl_ref, et_ref, wt_ref, *, heads_per_step):
    h = pl.program_id(1)

    @pl.when(h == 0)
    def _():
        xn = _rms_norm_rows(x_ref[...], g_ref[...])
        hi, lo = _split2(xn)
        xh_ref[...] = hi
        xl_ref[...] = lo
        hn_ref[...] = hi

    xh, xl = xh_ref[...], xl_ref[...]
    K = PEER_TOPK
    b_iota = lax.broadcasted_iota(jnp.int32, (8, LANES), 0)
    jobs = []
    for hh in range(heads_per_step):
        cs = pl.ds(hh * PEER_QDIM, PEER_QDIM)
        wqh, wql = wqh_ref[:, cs], wql_ref[:, cs]
        q = _dot(xh, wqh) + _dot(xh, wql) + _dot(xl, wqh)
        sts = []
        for z in range(2):
            qh, ql = _split2(q[:, z * PEER_KEYS:(z + 1) * PEER_KEYS])
            skh, skl = skh_ref[z], skl_ref[z]
            sts.append(_dot_nt(skh, qh) + _dot_nt(skl, qh) + _dot_nt(skh, ql))
        row = pl.multiple_of((h * heads_per_step + hh) * K, K)
        jobs += [(sts, row, c0) for c0 in range(0, xh.shape[0], LANES)]
    for sts, row, c0 in jobs:
        (v1, i1), (v2, i2) = _topk_cols_many([st[:, c0:c0 + LANES] for st in sts], K)
        v2_lo, v2_hi = jnp.concatenate(v2[:8], axis=0), jnp.concatenate(v2[8:], axis=0)
        i2_lo, i2_hi = jnp.concatenate(i2[:8], axis=0), jnp.concatenate(i2[8:], axis=0)
        e_of = lambda a, i2rows: i1[a] * float(PEER_KEYS) + i2rows
        cand = [v1[0] + v2_lo, v1[0] + v2_hi, v1[1] + v2_lo]
        cand_e = [e_of(0, i2_lo), e_of(0, i2_hi), e_of(1, i2_lo)]
        for a in range(2, 8):
            cand.append(jnp.where(b_iota < K // (a + 1), v1[a] + v2_lo, -jnp.inf))
            cand_e.append(e_of(a, i2_lo))
        cand.append(jnp.concatenate(v1[8:], axis=0) + v2[0])
        cand_e.append(jnp.concatenate(i1[8:], axis=0) * float(PEER_KEYS) + i2[0])
        best, best_e = _topk_cols(jnp.concatenate(cand, axis=0), K, payload=jnp.concatenate(cand_e, axis=0))
        bs = jnp.concatenate(best, axis=0)
        ex = jnp.exp(bs - best[0])
        et_ref[pl.ds(row, K), pl.ds(c0, LANES)] = jnp.concatenate(best_e, axis=0)
        wt_ref[pl.ds(row, K), pl.ds(c0, LANES)] = ex / jnp.sum(ex, axis=0, keepdims=True)

    @pl.when(h == pl.num_programs(1) - 1)
    def _():
        e_ref[...] = et_ref[...].T.astype(jnp.int32)
        w_ref[...] = wt_ref[...].T


def peer_route(x, g, wq_hi, wq_lo, sk_hi, sk_lo, *, tm, heads_per_step=8):
    T, D = x.shape
    nslot = PEER_HEADS * PEER_TOPK
    wq_spec = pl.BlockSpec((D, heads_per_step * PEER_QDIM), lambda i, h: (0, h))
    return pl.pallas_call(
        functools.partial(_peer_route_kernel, heads_per_step=heads_per_step),
        out_shape=[jax.ShapeDtypeStruct((T, D), BF16),
                   jax.ShapeDtypeStruct((T, nslot), jnp.int32),
                   jax.ShapeDtypeStruct((T, nslot), F32)],
        grid=(T // tm, PEER_HEADS // heads_per_step),
        in_specs=[pl.BlockSpec((tm, D), lambda i, h: (i, 0)),
                  pl.BlockSpec((1, D), lambda i, h: (0, 0)),
                  wq_spec, wq_spec,
                  pl.BlockSpec((2, PEER_KEYS, PEER_QDIM // 2), lambda i, h: (0, 0, 0)),
                  pl.BlockSpec((2, PEER_KEYS, PEER_QDIM // 2), lambda i, h: (0, 0, 0))],
        out_specs=[pl.BlockSpec((tm, D), lambda i, h: (i, 0)),
                   pl.BlockSpec((tm, nslot), lambda i, h: (i, 0)),
                   pl.BlockSpec((tm, nslot), lambda i, h: (i, 0))],
        scratch_shapes=[pltpu.VMEM((tm, D), BF16), pltpu.VMEM((tm, D), BF16),
                        pltpu.VMEM((nslot, tm), F32), pltpu.VMEM((nslot, tm), F32)],
        compiler_params=_cparams(("parallel", "arbitrary")),
        name="peer_route",
    )(x, g.reshape(1, D), wq_hi, wq_lo, sk_hi, sk_lo)


HI16 = -65536


def _peer_expert_kernel(hn_ref, e_ref, w_ref, x_ref, ua_ref, ub_ref, va_ref, vb_ref, o_ref, gs_ref, acc_ref,
                        *, tm, rows_per_chunk):
    c = pl.program_id(1)
    nk = PEER_KEYS
    half = nk // 2

    @pl.when(c == 0)
    def _():
        acc_ref[...] = jnp.zeros_like(acc_ref)
        sub = lax.broadcasted_iota(jnp.int32, (nk, nk), 0)

        def body(t8, carry):
            r0 = pl.multiple_of(t8 * 8, 8)
            e8 = e_ref[pl.ds(r0, 8), :]
            w8 = w_ref[pl.ds(r0, 8), :]
            for r in range(8):
                e, w = e8[r:r + 1], w8[r:r + 1]
                i1 = jnp.right_shift(e, 7)
                i2 = jnp.bitwise_and(e, nk - 1)
                pt = jnp.where(sub == i1, w, 0.0).astype(BF16)
                qt = jnp.where(sub == i2, 1.0, 0.0).astype(BF16)
                gt = _dot_nt(pt, qt).astype(BF16).astype(F32)
                bits = lax.bitcast_convert_type(gt, jnp.int32)
                word = jnp.bitwise_or(lax.shift_right_logical(bits[:half], 16), bits[half:])
                gs_ref[pl.ds(pl.multiple_of((r0 + r) * G_PITCH, 8), half), :] = word
            return carry

        lax.fori_loop(0, tm // 8, body, 0)

    hn = hn_ref[...]
    act_a = jax.nn.gelu(_dot(hn, ua_ref[...]))
    act_b = jax.nn.gelu(_dot(hn, ub_ref[...]))
    pa, pb = [], []
    for r in range(rows_per_chunk):
        word = gs_ref[pl.ds(c * rows_per_chunk + r, tm, stride=G_PITCH), :]
        g_lo = lax.bitcast_convert_type(jnp.left_shift(word, 16), F32)
        g_hi = lax.bitcast_convert_type(jnp.bitwise_and(word, HI16), F32)
        pa.append((act_a[:, r * nk:(r + 1) * nk] * g_lo).astype(BF16))
        pb.append((act_b[:, r * nk:(r + 1) * nk] * g_hi).astype(BF16))
    acc_ref[...] += (_dot(jnp.concatenate(pa, axis=1), va_ref[...])
                     + _dot(jnp.concatenate(pb, axis=1), vb_ref[...]))

    @pl.when(c == pl.num_programs(1) - 1)
    def _():
        o_ref[...] = x_ref[...] + acc_ref[...]


def peer_experts(hn, e, w, x, ut, v, *, tm, chunk):
    T, D = x.shape
    n_exp = v.shape[0]
    nslot = e.shape[1]
    blk = chunk // 2
    rows_per_chunk = blk // PEER_KEYS
    n_chunks = n_exp // chunk
    lo = pl.BlockSpec((blk, D), lambda i, c: (c, 0))
    hi = pl.BlockSpec((blk, D), lambda i, c: (c + n_chunks, 0))
    return pl.pallas_call(
        functools.partial(_peer_expert_kernel, tm=tm, rows_per_chunk=rows_per_chunk),
        out_shape=jax.ShapeDtypeStruct((T, D), F32),
        grid=(T // tm, n_chunks),
        in_specs=[pl.BlockSpec((tm, D), lambda i, c: (i, 0)),
                  pl.BlockSpec((tm, nslot), lambda i, c: (i, 0)),
                  pl.BlockSpec((tm, nslot), lambda i, c: (i, 0)),
                  pl.BlockSpec((tm, D), lambda i, c: (i, 0)),
                  pl.BlockSpec((D, blk), lambda i, c: (0, c)), pl.BlockSpec((D, blk), lambda i, c: (0, c + n_chunks)),
                  lo, hi],
        out_specs=pl.BlockSpec((tm, D), lambda i, c: (i, 0)),
        scratch_shapes=[pltpu.VMEM((tm * G_PITCH, PEER_KEYS), jnp.int32), pltpu.VMEM((tm, D), F32)],
        compiler_params=_cparams(("parallel", "arbitrary")),
        name="peer_experts",
    )(hn, e, w, x, ut, ut, v, v)


def peer_layer(x, g, w_q, subkeys, u, v, *, tm_route, tm_exp, chunk):
    wq_hi, wq_lo = _split2(w_q)
    sk_hi, sk_lo = _split2(subkeys)
    hn, e, w = peer_route(x, g, wq_hi, wq_lo, sk_hi, sk_lo, tm=tm_route)
    return peer_experts(hn, e, w, x, u.astype(BF16).T, v.astype(BF16), tm=tm_exp, chunk=chunk)


def _compress_core(get_x, w1k_ref, w1v_ref, pe_ref, w1f_ref, w2k_ref, w2v_ref, cos_ref, sa_ref, sb_ref,
                   kc_ref, vc_ref, acc_ref, n_cmp):
    nch = acc_ref.shape[0]
    for l in range(CMP_STRIDE):
        x = get_x(l)
        yk = _dot(x[:, :KV_W].astype(BF16), w1k_ref[l])
        yv = _dot(x[:, KV_W:].astype(BF16), w1v_ref[l])
        if l == 0:
            acc_ref[:, pl.ds(0, 2 * KV_W)] = yk
            acc_ref[:, pl.ds(2 * KV_W, 2 * KV_W)] = yv
        else:
            acc_ref[:, pl.ds(0, 2 * KV_W)] += yk
            acc_ref[:, pl.ds(2 * KV_W, 2 * KV_W)] += yv
    keep = lax.broadcasted_iota(jnp.int32, (nch, KV_W), 0) < n_cmp
    for z, (w2_ref, out_ref) in enumerate(((w2k_ref, kc_ref), (w2v_ref, vc_ref))):
        ph, pl_ = _split2(pe_ref[z])
        wh, wl = _split2(w1f_ref[z])
        pe_t = _dot(ph, wh) + _dot(ph, wl) + _dot(pl_, wh)
        pe_t = jnp.concatenate([pe_t] * B_KV_HEADS, axis=1)
        lo = acc_ref[:, pl.ds(z * 2 * KV_W, KV_W)]
        hi = acc_ref[:, pl.ds(z * 2 * KV_W + KV_W, KV_W)]
        hid = jax.nn.gelu(lo + pltpu.roll(hi, nch - 1, 0) + pe_t)
        out = _dot(hid.astype(BF16), w2_ref[...])
        if z == 0:
            out = _rope_lanes(out, cos_ref[...], sa_ref[...], sb_ref[...])
        out_ref[0] = jnp.where(keep, out, 0.0)


def _compress_prompt_kernel(x0, x1, x2, x3, *rest, n_cmp):
    slabs = (x0, x1, x2, x3)
    nch = rest[-1].shape[0]

    def get_x(l):
        return jnp.concatenate([s[pl.ds(l, nch, stride=CMP_STRIDE), :] for s in slabs], axis=1)

    _compress_core(get_x, *rest, n_cmp=n_cmp)


def _compress_weights(w_cmp1, w_cmp2, pe_cmp, n_chunk):
    eye = jnp.eye(B_KV_HEADS, dtype=F32)

    def per_head(w):
        bd = jnp.einsum('hg,...df->...hdgf', eye, w)
        return bd.reshape(w.shape[:-2] + (KV_W, KV_W)).astype(BF16)

    w1 = per_head(w_cmp1.reshape(2, 2, CMP_STRIDE, HEAD_DIM, -1))
    w1 = jnp.concatenate([w1[:, 0], w1[:, 1]], axis=-1)
    w2 = per_head(w_cmp2)
    pe = pe_cmp.reshape(2, 1, CMP_LEN * HEAD_DIM)
    w1f = w_cmp1.reshape(2, CMP_LEN * HEAD_DIM, -1)
    end_pos = jnp.arange(n_chunk, dtype=jnp.int32) * CMP_STRIDE + (CMP_LEN - 1)
    tabs = _rope_tables(end_pos, KV_W)
    return (w1[0], w1[1], pe, w1f, w2[0], w2[1]) + tabs


def _const_specs(arrs):
    return [pl.BlockSpec(a.shape, (lambda *i, nd=a.ndim: (0,) * nd), pipeline_mode=pl.Buffered(1)) for a in arrs]


def compress_prompt(kv, cw, B, S):
    nch = S // CMP_STRIDE
    slab = lambda s: pl.BlockSpec((S, LANES), lambda b: (b, s), pipeline_mode=pl.Buffered(1))
    out = pl.BlockSpec((1, nch, KV_W), lambda b: (b, 0, 0))
    return pl.pallas_call(
        functools.partial(_compress_prompt_kernel, n_cmp=nch - 1),
        out_shape=[jax.ShapeDtypeStruct((B, nch, KV_W), F32)] * 2,
        grid=(B,),
        in_specs=[slab(s) for s in range(4)] + _const_specs(cw),
        out_specs=[out, out],
        scratch_shapes=[pltpu.VMEM((nch, 4 * KV_W), F32)],
        compiler_params=_cparams(("parallel",)),
        name="compress_prompt",
    )(kv, kv, kv, kv, *cw)


def _compress_sample_kernel(pt_ref, *refs, n_cmp, pages_per_step):
    n_pg = 4 * pages_per_step
    pages, rest, x_ref = refs[:n_pg], refs[n_pg:-1], refs[-1]
    j = pl.program_id(1)
    ch_per_page = pages[0].shape[1] // CMP_STRIDE
    for p in range(pages_per_step):
        r0 = pl.multiple_of((j * pages_per_step + p) * ch_per_page, ch_per_page)
        for s in range(4):
            for l in range(CMP_STRIDE):
                x_ref[l, pl.ds(r0, ch_per_page), pl.ds(s * LANES, LANES)] = (
                    pages[p * 4 + s][0, pl.ds(l, ch_per_page, stride=CMP_STRIDE), :])

    @pl.when(j == pl.num_programs(1) - 1)
    def _():
        _compress_core(lambda l: x_ref[l], *rest, n_cmp=n_cmp)


def compress_sample(cache, page_table, cw, *, pages_per_step):
    Bd, n_pages = page_table.shape
    page = cache.shape[1]
    nch = n_pages * page // CMP_STRIDE
    P = pages_per_step

    def pspec(p, s):
        return pl.BlockSpec((1, page, LANES), lambda b, j, pt: (pt[b, j * P + p], 0, s))

    out = pl.BlockSpec((1, nch, KV_W), lambda b, j, pt: (b, 0, 0))
    return pl.pallas_call(
        functools.partial(_compress_sample_kernel, n_cmp=nch - 1, pages_per_step=P),
        out_shape=[jax.ShapeDtypeStruct((Bd, nch, KV_W), F32)] * 2,
        grid_spec=pltpu.PrefetchScalarGridSpec(
            num_scalar_prefetch=1,
            grid=(Bd, n_pages // P),
            in_specs=[pspec(p, s) for p in range(P) for s in range(4)] + _const_specs(cw),
            out_specs=[out, out],
            scratch_shapes=[pltpu.VMEM((nch, 4 * KV_W), F32),
                            pltpu.VMEM((CMP_STRIDE, nch, 2 * KV_W), F32)],
        ),
        compiler_params=_cparams(("parallel", "arbitrary")),
        name="compress_sample",
    )(page_table, *([cache] * (4 * P)), *cw)


def _select_blocks(imp, wsel, blk, cur):
    i_hi = imp.astype(BF16)
    r1 = imp - i_hi.astype(F32)
    i_mid = r1.astype(BF16)
    i_lo = (r1 - i_mid.astype(F32)).astype(BF16)
    imp_sel = _dot(i_hi, wsel) + _dot(i_mid, wsel) + _dot(i_lo, wsel)
    forced = (blk == 0) | (blk == cur) | (blk == cur - 1)
    imp_sel = jnp.where(blk > cur, -jnp.inf, jnp.where(forced, jnp.inf, imp_sel))
    return _topk_mask_cols(imp_sel.T, N_SEL).T


def _tile_block_hits(chosen_q, k0, kt_size):
    n_sb = chosen_q.shape[1]
    blk = lax.broadcasted_iota(jnp.int32, (n_sb, kt_size), 0)
    key_blk = jnp.right_shift(k0 + lax.broadcasted_iota(jnp.int32, (n_sb, kt_size), 1), 6)
    expand = jnp.where(blk == key_blk, 1.0, 0.0).astype(BF16)
    return _dot(chosen_q, expand)


def _tile_block_mask(chosen_q, k0, kt_size):
    return _tile_block_hits(chosen_q, k0, kt_size) > 0.5


def _nsa_prompt_kernel(q_ref, gate_ref, kc_ref, vc_ref, ks_ref, vs_ref, kw_ref, vw_ref, wsel_ref,
                       o_ref, *, n_cmp, kt_size):
    n = pl.program_id(1)
    t0 = n * Q_BLOCK
    G, TQ = B_GROUP, Q_BLOCK
    R = G * TQ
    NC = kc_ref.shape[1]
    NSB = wsel_ref.shape[1]
    wsel = wsel_ref[...]
    gt = gate_ref[...]
    row_c = lax.broadcasted_iota(jnp.int32, (R, NC), 0)
    col_c = lax.broadcasted_iota(jnp.int32, (R, NC), 1)
    valid_c = (col_c * CMP_STRIDE + (CMP_LEN - 1) <= t0 + jnp.bitwise_and(row_c, TQ - 1)) & (col_c < n_cmp)
    bias_c = jnp.where(valid_c, 0.0, NEG)
    sees_cmp = t0 + jnp.bitwise_and(lax.broadcasted_iota(jnp.int32, (R, 1), 0), TQ - 1) >= CMP_LEN - 1
    blk = lax.broadcasted_iota(jnp.int32, (TQ, NSB), 1)
    cur = jnp.right_shift(t0 + lax.broadcasted_iota(jnp.int32, (TQ, NSB), 0), 6)
    WK = B_WINDOW + TQ
    w0 = pl.multiple_of(jnp.maximum(t0 - B_WINDOW, 0), TQ)
    wpos = w0 + lax.broadcasted_iota(jnp.int32, (R, WK), 1)
    rel = t0 + jnp.bitwise_and(lax.broadcasted_iota(jnp.int32, (R, WK), 0), TQ - 1) - wpos
    bias_w = jnp.where((rel >= 0) & (rel <= B_WINDOW), 0.0, NEG)
    n_kt = (t0 + TQ + kt_size - 1) // kt_size
    k_last = pl.multiple_of((n_kt - 1) * kt_size, kt_size)
    causal_last = jnp.where(k_last + lax.broadcasted_iota(jnp.int32, (TQ, kt_size), 1)
                            <= t0 + lax.broadcasted_iota(jnp.int32, (TQ, kt_size), 0), 0.0, NEG)
    ones_col = jnp.where(lax.broadcasted_iota(jnp.int32, (kt_size, HEAD_DIM), 1) == 0, 1.0, 0.0).astype(BF16)

    head_cols = [pl.ds(h * HEAD_DIM, HEAD_DIM) for h in range(B_KV_HEADS)]

    qbs, o_cs, chosens = [], [], []
    for h, hs in enumerate(head_cols):
        qs = jnp.concatenate([q_ref[:, pl.ds((h * G + g) * HEAD_DIM, HEAD_DIM)] for g in range(G)], axis=0)
        qs = qs * (SCALE * LOG2E)
        s_c = _dot_nt3(qs, kc_ref[0, :, hs]) + bias_c
        e_c = jnp.exp2(s_c - jnp.max(s_c, axis=-1, keepdims=True))
        p_c = e_c * jnp.where(sees_cmp, 1.0 / jnp.sum(e_c, axis=-1, keepdims=True), 0.0)
        imp = p_c[0:TQ]
        for g in range(1, G):
            imp = imp + p_c[g * TQ:(g + 1) * TQ]
        qbs.append(qs.astype(BF16))
        o_cs.append(_dot(p_c.astype(BF16), vc_ref[0, :, hs].astype(BF16)))
        chosens.append(_select_blocks(imp, wsel, blk, cur).astype(BF16))

    def sel_tile(h, k0, carry, extra_bias):
        m_i, acc = carry
        hs = head_cols[h]
        bias = (_tile_block_hits(chosens[h], k0, kt_size) - 1.0) * (-NEG)
        if extra_bias is not None:
            bias = bias + extra_bias
        s = _dot_nt(qbs[h], ks_ref[pl.ds(k0, kt_size), hs]).reshape(G, TQ, kt_size) + bias[None]
        s = s.reshape(R, kt_size)
        m_new = jnp.maximum(m_i, jnp.max(s, axis=-1, keepdims=True))
        p = jnp.exp2(s - m_new).astype(BF16)
        v1 = jnp.concatenate([vs_ref[pl.ds(k0, kt_size), hs], ones_col], axis=1)
        return m_new, jnp.exp2(m_i - m_new) * acc + _dot(p, v1)

    def sel_body(kt, carries):
        k0 = pl.multiple_of(kt * kt_size, kt_size)
        return tuple(sel_tile(h, k0, carries[h], None) for h in range(B_KV_HEADS))

    init = tuple((jnp.full((R, 1), NEG, F32), jnp.zeros((R, 2 * HEAD_DIM), F32)) for _ in range(B_KV_HEADS))
    carries = lax.fori_loop(0, n_kt - 1, sel_body, init)

    outs = []
    for h, hs in enumerate(head_cols):
        _, acc_s = sel_tile(h, k_last, carries[h], causal_last)
        o_s = acc_s[:, :HEAD_DIM] / acc_s[:, HEAD_DIM:HEAD_DIM + 1]
        s_w = _dot_nt(qbs[h], kw_ref[pl.ds(w0, WK), hs]) + bias_w
        e_w = jnp.exp2(s_w - jnp.max(s_w, axis=-1, keepdims=True))
        p_w = e_w * (1.0 / jnp.sum(e_w, axis=-1, keepdims=True))
        o_w = _dot(p_w.astype(BF16), vw_ref[pl.ds(w0, WK), hs])
        o_c = o_cs[h]
        for g in range(G):
            rs = slice(g * TQ, (g + 1) * TQ)
            c = 3 * (h * G + g)
            outs.append(gt[:, c:c + 1] * o_c[rs] + gt[:, c + 1:c + 2] * o_s[rs] + gt[:, c + 2:c + 3] * o_w[rs])
    o_ref[...] = jnp.concatenate(outs, axis=1)


def nsa_prompt(q, gate, kc, vc, kvb, wsel, B, S, *, kt_size):
    nq = q.shape[1]
    nqb = S // Q_BLOCK
    NC = kc.shape[1]
    comp = lambda c: pl.BlockSpec((S, KV_W), lambda b, n: (b, c))
    cmp_spec = pl.BlockSpec((1, NC, KV_W), lambda b, n: (b, 0, 0))
    return pl.pallas_call(
        functools.partial(_nsa_prompt_kernel, n_cmp=S // CMP_STRIDE - 1, kt_size=kt_size),
        out_shape=jax.ShapeDtypeStruct((B * S, nq), F32),
        grid=(B, nqb),
        in_specs=[pl.BlockSpec((Q_BLOCK, nq), lambda b, n: (b * nqb + n, 0)),
                  pl.BlockSpec((Q_BLOCK, LANES), lambda b, n: (b * nqb + n, 0)),
                  cmp_spec, cmp_spec, comp(2), comp(3), comp(4), comp(5),
                  pl.BlockSpec(wsel.shape, lambda b, n: (0, 0))],
        out_specs=pl.BlockSpec((Q_BLOCK, nq), lambda b, n: (b * nqb + n, 0)),
        compiler_params=_cparams(("parallel", "arbitrary")),
        name="nsa_prompt",
    )(q, gate, kc, vc, kvb, kvb, kvb, kvb, wsel)


def _nsa_step_kernel(pt_ref, q_ref, gate_ref, kc_ref, vc_ref, new_ref, win_ref, wsel_ref, *rest,
                     past, n_cmp, pages_per_step):
    P = pages_per_step
    pages = rest[:P]
    o_ref, st_ref, a_ref, ch_ref, m_ref, l_ref, acc_ref, oc_ref = rest[P:]
    del pt_ref
    j = pl.program_id(1)
    G, KVH = B_GROUP, B_KV_HEADS
    Sd = q_ref.shape[0]
    R = B_HEADS * Sd
    page = pages[0].shape[1]
    kt = P * page
    row_head = lambda shape: lax.broadcasted_iota(jnp.int32, shape, 0) // (G * Sd)
    own = lambda width: row_head((R, width)) == lax.broadcasted_iota(jnp.int32, (R, width), 1) // HEAD_DIM

    def diag(x):
        rows = G * Sd
        return jnp.concatenate([x[h * rows:(h + 1) * rows, h * HEAD_DIM:(h + 1) * HEAD_DIM]
                                for h in range(KVH)], axis=0)

    @pl.when(j == 0)
    def _():
        q = q_ref[...]
        a = jnp.concatenate([q[:, hd * HEAD_DIM:(hd + 1) * HEAD_DIM] for hd in range(B_HEADS)], axis=0)
        a = jnp.where(own(KV_W), jnp.concatenate([a] * KVH, axis=1), 0.0)
        a_ref[...] = a.astype(BF16)
        NC = kc_ref.shape[1]
        valid_c = lax.broadcasted_iota(jnp.int32, (R, NC), 1) < n_cmp
        p_c = _softmax_rows(_dot_nt3(a, kc_ref[0]) * SCALE, valid_c)
        oc_ref[...] = diag(_dot(p_c.astype(BF16), vc_ref[0].astype(BF16)))
        pooled = jnp.sum(p_c.reshape(KVH, G, Sd, NC), axis=1, keepdims=True)
        imp = jnp.broadcast_to(pooled, (KVH, G, Sd, NC)).reshape(R, NC)
        NSB = wsel_ref.shape[1]
        blk = lax.broadcasted_iota(jnp.int32, (R, NSB), 1)
        srow = jnp.bitwise_and(lax.broadcasted_iota(jnp.int32, (R, NSB), 0), Sd - 1)
        cur = jnp.right_shift(past + srow, 6)
        ch_ref[...] = _select_blocks(imp, wsel_ref[...], blk, cur).astype(BF16)
        m_ref[...] = jnp.full(m_ref.shape, NEG, F32)
        l_ref[...] = jnp.zeros(l_ref.shape, F32)
        acc_ref[...] = jnp.zeros(acc_ref.shape, F32)

    a = a_ref[...]
    ks = jnp.concatenate([pg[0, :, pl.ds(0, KV_W)] for pg in pages], axis=0).astype(BF16)
    vs = jnp.concatenate([pg[0, :, pl.ds(KV_W, KV_W)] for pg in pages], axis=0).astype(BF16)
    mk = _tile_block_mask(ch_ref[...], j * kt, kt)
    m_n, l_n, acc_n = _online_update((m_ref[...], l_ref[...], acc_ref[...]), _dot_nt(a, ks) * SCALE, mk, vs)
    m_ref[...] = m_n
    l_ref[...] = l_n
    acc_ref[...] = acc_n

    @pl.when(j == pl.num_programs(1) - 1)
    def _():
        new = _pad_rows(new_ref[...], LANES)
        comp = lambda c: new[:, c * KV_W:(c + 1) * KV_W].astype(BF16)
        srow = jnp.bitwise_and(lax.broadcasted_iota(jnp.int32, (R, LANES), 0), Sd - 1)
        scol = lax.broadcasted_iota(jnp.int32, (R, LANES), 1)
        causal_new = (scol <= srow) & (scol < Sd)
        nblk = past // SEL_BLOCK
        in_new = ch_ref[:, nblk:nblk + 1].astype(F32) > 0.5
        _, l_s, acc_s = _online_update((m_n, l_n, acc_n), _dot_nt(a, comp(2)) * SCALE, in_new & causal_new, comp(3))
        o_s = diag(acc_s) / jnp.where(l_s > 0, l_s, 1.0)

        W = win_ref.shape[1]
        wrow = jnp.bitwise_and(lax.broadcasted_iota(jnp.int32, (R, W), 0), Sd - 1)
        rel = W + wrow - lax.broadcasted_iota(jnp.int32, (R, W), 1)
        carry = _online_update(_online_init(R, KV_W), _dot_nt(a, win_ref[0, :, pl.ds(0, KV_W)].astype(BF16)) * SCALE,
                               (rel >= 0) & (rel <= B_WINDOW), win_ref[0, :, pl.ds(KV_W, KV_W)].astype(BF16))
        _, l_w, acc_w = _online_update(carry, _dot_nt(a, comp(4)) * SCALE, causal_new, comp(5))
        o_w = diag(acc_w) / jnp.where(l_w > 0, l_w, 1.0)
        o_c = oc_ref[...]
        gt = gate_ref[...]
        outs = []
        for hd in range(B_HEADS):
            rs = slice(hd * Sd, (hd + 1) * Sd)
            c = 3 * hd
            outs.append(gt[:, c:c + 1] * o_c[rs] + gt[:, c + 1:c + 2] * o_s[rs] + gt[:, c + 2:c + 3] * o_w[rs])
        o_ref[...] = jnp.concatenate(outs, axis=1)
        st_ref[0, pl.ds(0, W - Sd), :] = win_ref[0, pl.ds(Sd, W - Sd), :]
        st_ref[0, pl.ds(W - Sd, Sd), :] = new_ref[:, pl.ds(4 * KV_W, 2 * KV_W)]


def nsa_step(q, gate, kc, vc, kv, win, cache, page_table, wsel, Tp, Bd, Sd, *, pages_per_step):
    nq = q.shape[1]
    n_pages = page_table.shape[1]
    page = cache.shape[1]
    past = n_pages * page
    W = win.shape[1]
    NC = kc.shape[1]
    P = pages_per_step
    r0 = Tp // Sd
    R = B_HEADS * Sd
    rows = lambda w: pl.BlockSpec((Sd, w), lambda b, j, pt: (r0 + b, 0))
    per_b = lambda s1, s2: pl.BlockSpec((1, s1, s2), lambda b, j, pt: (b, 0, 0))
    pspec = lambda p: pl.BlockSpec((1, page, 2 * KV_W), lambda b, j, pt: (pt[b, j * P + p], 0, 1))
    return pl.pallas_call(
        functools.partial(_nsa_step_kernel, past=past, n_cmp=past // CMP_STRIDE - 1, pages_per_step=P),
        out_shape=[jax.ShapeDtypeStruct((Bd * Sd, nq), F32), jax.ShapeDtypeStruct(win.shape, F32)],
        grid_spec=pltpu.PrefetchScalarGridSpec(
            num_scalar_prefetch=1,
            grid=(Bd, n_pages // P),
            in_specs=[rows(nq), rows(LANES), per_b(NC, KV_W), per_b(NC, KV_W), rows(kv.shape[1]),
                      per_b(W, 2 * KV_W), pl.BlockSpec(wsel.shape, lambda b, j, pt: (0, 0))]
                     + [pspec(p) for p in range(P)],
            out_specs=[pl.BlockSpec((Sd, nq), lambda b, j, pt: (b, 0)), per_b(W, 2 * KV_W)],
            scratch_shapes=[pltpu.VMEM((R, KV_W), BF16), pltpu.VMEM((R, wsel.shape[1]), BF16),
                            pltpu.VMEM((R, 1), F32), pltpu.VMEM((R, 1), F32), pltpu.VMEM((R, KV_W), F32),
                            pltpu.VMEM((R, HEAD_DIM), F32)],
        ),
        compiler_params=_cparams(("parallel", "arbitrary")),
        name="nsa_step",
    )(page_table, q, gate, kc, vc, kv, win, wsel, *([cache] * P))


def _rope_tables(pos, width):
    half = ROT_DIM // 2
    inv = jnp.exp(-jnp.log(jnp.float32(ROPE_THETA)) * jnp.arange(half, dtype=F32) * (2.0 / ROT_DIM))
    ang = pos.astype(F32)[:, None] * inv[None, :]
    cos, sin = jnp.cos(ang), jnp.sin(ang)
    T = pos.shape[0]
    z8 = jnp.zeros((T, half), F32)
    rest = HEAD_DIM - ROT_DIM
    c = jnp.concatenate([cos, cos, jnp.ones((T, rest), F32)], axis=1)
    sa = jnp.concatenate([-sin, z8, jnp.zeros((T, rest), F32)], axis=1)
    sb = jnp.concatenate([z8, sin, jnp.zeros((T, rest), F32)], axis=1)
    reps = width // HEAD_DIM
    return tuple(jnp.tile(t, (1, reps)) for t in (c, sa, sb))


def _overlap_weights(n_cmp, n_cmp_pad, n_sb, n_sb_pad):
    j = jnp.arange(n_sb)
    cidx = (SEL_BLOCK // CMP_STRIDE) * j[:, None] - 1 + jnp.arange(len(SEL_OVERLAP_W))[None, :]
    wts = jnp.where((cidx >= 0) & (cidx < n_cmp), jnp.asarray(SEL_OVERLAP_W, F32), 0.0)
    onehot = (jnp.clip(cidx, 0, n_cmp - 1)[:, :, None] == jnp.arange(n_cmp_pad)[None, None, :]).astype(F32)
    wsel = jnp.einsum('jo,jon->nj', wts, onehot)
    return jnp.pad(wsel, ((0, 0), (0, n_sb_pad - n_sb))).astype(BF16)


def _pick(n, cands):
    for c in cands:
        if n % c == 0:
            return c
    return n


def _pad_to(n, m):
    return -(-n // m) * m


def kernel(x_prompt, x_sample, cache_a_g1, cache_a_g2, cache_a_g3, cache_b_kv, cache_b_win, page_table, g_mix, g_ffn, w_in_a, w_o_a, g_kv, w_kv_b, w_cmp1, w_cmp2, pe_cmp, w_qg_b, b_gate_b, w_o_b, w_peer_q, peer_subkeys, peer_u, peer_v, g_final):
    B, S, D = x_prompt.shape
    Bd, Sd, _ = x_sample.shape
    Tp, Ts = B * S, Bd * Sd
    T = Tp + Ts
    n_phys, page = cache_b_kv.shape[:2]
    n_pages = page_table.shape[1]
    past = n_pages * page
    hw = A_HEADS * HEAD_DIM
    tm_big = _pick(T, (1280, 640, 256, 128, 64, 32, 16, 8))
    tm_mid = _pick(T, (640, 256, 128, 64, 32, 16, 8))
    tm_peer = _pick(T, (256, 128, 64, 32, 16, 8))
    tm_exp = _pick(T, (520, 512, 256, 128, 64, 32, 16, 8))
    tm_rows = _pick(math.gcd(Tp, Ts), (256, 128, 64, 32, 16, 8))
    pages_per_step = _pick(n_pages, (8, 4, 2, 1))

    x = jnp.concatenate([x_prompt.reshape(Tp, D), x_sample.reshape(Ts, D)], axis=0)
    pos_p = jnp.arange(S, dtype=jnp.int32)
    pos_s = past + jnp.arange(Sd, dtype=jnp.int32)
    pos = jnp.concatenate([jnp.tile(pos_p, B), jnp.tile(pos_s, Bd)])
    tab512 = _rope_tables(pos, 512)
    tab256 = tuple(t[:, :KV_W] for t in tab512)

    n_grp = len(A_GROUPS)
    proj = rms_matmul(x, g_mix[0], w_in_a[0].astype(BF16), tm=tm_big, tn=hw, epilogue="rope",
                      flags=jnp.asarray([1, 1, 0] * n_grp, jnp.int32), tables=tab512)
    caches_a = (cache_a_g1, cache_a_g2, cache_a_g3)
    op, lp, os_, ls_, a_prompt, a_sample = [], [], [], [], [], []
    for g, (win, dil) in enumerate(A_GROUPS):
        o_g, l_g = band_attention(proj, g, win, dil, B, S)
        cache = caches_a[g]
        o_s, l_s, state = a_step(proj, g, win, dil, cache[0].reshape(Bd, cache.shape[2], 2 * hw), Tp, Bd, Sd)
        a_sample.append(state.reshape(cache.shape))
        keep = min(win, S)
        kv_cols = proj[:Tp].reshape(B, S, n_grp, 3 * hw)[:, S - keep:, g, hw:]
        a_prompt.append(kv_cols.reshape(1, B, keep, 2, A_HEADS, HEAD_DIM))
        op.append(o_g)
        lp.append(l_g)
        os_.append(o_s)
        ls_.append(l_s)
    x = merge_matmul_residual(op + lp, os_ + ls_, w_o_a[0].astype(BF16), x, tm=tm_rows)
    x = peer_layer(x, g_ffn[0], w_peer_q[0], peer_subkeys[0], peer_u[0], peer_v[0],
                   tm_route=tm_peer, tm_exp=tm_exp, chunk=1024)

    kv, kvb = rms_matmul(x, g_kv, w_kv_b.astype(BF16), tm=tm_big, tn=KV_W, epilogue="rope",
                         flags=jnp.asarray([0, 0, 1, 0, 1, 0], jnp.int32), tables=tab256, emit_bf16=True)
    kv_p = kv[:Tp].reshape(B, S, 6, B_KV_HEADS, HEAD_DIM)
    bkv_prompt = kv_p[:, :, :4]
    bkv_sample = kv[Tp:, :4 * KV_W].reshape(Bd, Sd, 4, B_KV_HEADS, HEAD_DIM)
    keep_w = min(B_WINDOW, S)
    bwin_prompt = kv_p[:, S - keep_w:, 4:]
    cache2d = cache_b_kv.reshape(n_phys, page, 4 * KV_W)
    cw_p = _compress_weights(w_cmp1, w_cmp2, pe_cmp, S // CMP_STRIDE)
    cw_s = cw_p if past == S else _compress_weights(w_cmp1, w_cmp2, pe_cmp, past // CMP_STRIDE)
    kc_p, vc_p = compress_prompt(kv, cw_p, B, S)
    kc_s, vc_s = compress_sample(cache2d, page_table, cw_s, pages_per_step=pages_per_step)

    nq = B_HEADS * HEAD_DIM
    w_qg = w_qg_b[0]
    q = rms_matmul(x, g_mix[1], w_qg[:, :nq].astype(BF16), tm=tm_big, tn=512, epilogue="rope",
                   flags=jnp.ones((nq // 512,), jnp.int32), tables=tab512)
    ngate = 3 * B_HEADS
    w_gate = jnp.pad(w_qg[:, nq:], ((0, 0), (0, LANES - ngate))).astype(BF16)
    b_gate = jnp.pad(b_gate_b[0], (0, LANES - ngate))
    gate = rms_matmul(x, g_mix[1], w_gate, tm=tm_big, tn=LANES, epilogue="sigmoid", bias=b_gate)

    n_cmp_p = S // CMP_STRIDE - 1
    wsel_p = _overlap_weights(n_cmp_p, S // CMP_STRIDE, S // SEL_BLOCK, _pad_to(S // SEL_BLOCK, LANES))
    o_b_p = nsa_prompt(q, gate, kc_p, vc_p, kvb, wsel_p, B, S, kt_size=min(1024, S))
    n_cmp_s = past // CMP_STRIDE - 1
    n_sb_s = -(-(past + Sd) // SEL_BLOCK)
    wsel_s = _overlap_weights(n_cmp_s, past // CMP_STRIDE, n_sb_s, _pad_to(n_sb_s, LANES))
    Wb = cache_b_win.shape[1]
    o_b_s, bwin_sample = nsa_step(q, gate, kc_s, vc_s, kv, cache_b_win.reshape(Bd, Wb, 2 * KV_W), cache2d,
                                  page_table, wsel_s, Tp, Bd, Sd, pages_per_step=pages_per_step)
    x = matmul_residual(o_b_p, o_b_s, w_o_b[0].astype(BF16), x, tm=tm_rows)
    x = peer_layer(x, g_ffn[1], w_peer_q[1], peer_subkeys[1], peer_u[1], peer_v[1],
                   tm_route=tm_peer, tm_exp=tm_exp, chunk=1024)
    y = rmsnorm_rows(x, g_final, tm=tm_mid)

    return (y[:Tp].reshape(B, S, D), y[Tp:].reshape(Bd, Sd, D),
            a_prompt[0], a_sample[0], a_prompt[1], a_sample[1], a_prompt[2], a_sample[2],
            bkv_prompt, bkv_sample, bwin_prompt, bwin_sample.reshape(cache_b_win.shape))
```

```python
import functools
import math

import jax
import jax.numpy as jnp
from jax import lax
from jax.experimental import pallas as pl
from jax.experimental.pallas import tpu as pltpu

F32 = jnp.float32
BF16 = jnp.bfloat16

HEAD_DIM = 64
ROT_DIM = HEAD_DIM // 4
ROPE_THETA = 500000.0
NORM_EPS = 1e-6
A_GROUPS = ((128, 1), (512, 4), (2048, 16))
A_HEADS = 8
A_BLOCK = 128
B_HEADS = 16
B_KV_HEADS = 4
B_GROUP = B_HEADS // B_KV_HEADS
CMP_LEN = 32
CMP_STRIDE = 16
SEL_BLOCK = 64
N_SEL = 16
SEL_OVERLAP_W = (1.0, 2.0, 2.0, 2.0, 1.0)
B_WINDOW = 512
Q_BLOCK = 128
PEER_HEADS = 8
PEER_KEYS = 128
PEER_QDIM = 256
PEER_TOPK = 16

LANES = 128
NEG = -1e30
VMEM_LIMIT = 50 * 1024 * 1024
G_PITCH = 72
KV_W = B_KV_HEADS * HEAD_DIM
SCALE = HEAD_DIM ** -0.5
LOG2E = 1.4426950408889634
LN2 = 0.6931471805599453

_NT = (((1,), (1,)), ((), ()))


def _cparams(sem):
    return pltpu.CompilerParams(dimension_semantics=sem, vmem_limit_bytes=VMEM_LIMIT)


def _split2(a):
    hi = a.astype(BF16)
    lo = (a - hi.astype(F32)).astype(BF16)
    return hi, lo


def _dot(a, b):
    return jnp.dot(a, b, preferred_element_type=F32)


def _dot_nt(a, b):
    return lax.dot_general(a, b, _NT, preferred_element_type=F32)


def _dot_nt3(a, b):
    ah, al = _split2(a)
    bh, bl = _split2(b)
    return _dot_nt(ah, bh) + _dot_nt(ah, bl) + _dot_nt(al, bh)


def _softmax_rows(s, valid):
    s = jnp.where(valid, s, NEG)
    m = jnp.max(s, axis=-1, keepdims=True)
    e = jnp.where(valid, jnp.exp(s - m), 0.0)
    den = jnp.sum(e, axis=-1, keepdims=True)
    return e / jnp.where(den > 0, den, 1.0)


def _rope_lanes(y, cos, sa, sb):
    n = y.shape[-1]
    return y * cos + pltpu.roll(y, 8, 1) * sb + pltpu.roll(y, n - 8, 1) * sa


def _rms_norm_rows(x, g):
    ms = jnp.mean(x * x, axis=-1, keepdims=True)
    return x * lax.rsqrt(ms + NORM_EPS) * g


def _rms_mm_kernel(flags_ref, x_ref, g_ref, w_ref, *rest, epilogue, emit_bf16):
    n_out = 2 if emit_bf16 else 1
    extra, outs, xn_ref = rest[:len(rest) - n_out - 1], rest[len(rest) - n_out - 1:-1], rest[-1]
    j = pl.program_id(1)

    @pl.when(j == 0)
    def _():
        xn_ref[...] = _rms_norm_rows(x_ref[...], g_ref[...]).astype(BF16)

    def store(val):
        outs[0][...] = val
        if emit_bf16:
            outs[1][...] = val.astype(BF16)

    y = _dot(xn_ref[...], w_ref[...])
    if epilogue == "rope":
        cos_ref, sa_ref, sb_ref = extra
        f = flags_ref[j]

        @pl.when(f == 0)
        def _():
            store(y)

        @pl.when(f != 0)
        def _():
            store(_rope_lanes(y, cos_ref[...], sa_ref[...], sb_ref[...]))
    elif epilogue == "sigmoid":
        store(jax.nn.sigmoid(y + extra[0][...]))
    else:
        store(y)


def rms_matmul(x, g, w, *, tm, tn, epilogue="none", flags=None, tables=None, bias=None, emit_bf16=False):
    M, D = x.shape
    N = w.shape[1]
    assert M % tm == 0 and N % tn == 0
    nj = N // tn
    if flags is None:
        flags = jnp.zeros((nj,), jnp.int32)
    in_specs = [
        pl.BlockSpec((tm, D), lambda i, j, f: (i, 0)),
        pl.BlockSpec((1, D), lambda i, j, f: (0, 0)),
        pl.BlockSpec((D, tn), lambda i, j, f: (0, j)),
    ]
    args = [x, g.reshape(1, D), w]
    if epilogue == "rope":
        for t in tables:
            in_specs.append(pl.BlockSpec((tm, tn), lambda i, j, f: (i, 0)))
            args.append(t)
    elif epilogue == "sigmoid":
        in_specs.append(pl.BlockSpec((1, tn), lambda i, j, f: (0, j)))
        args.append(bias.reshape(1, N))
    out_spec = pl.BlockSpec((tm, tn), lambda i, j, f: (i, j))
    out_shape = [jax.ShapeDtypeStruct((M, N), F32)]
    if emit_bf16:
        out_shape.append(jax.ShapeDtypeStruct((M, N), BF16))
    res = pl.pallas_call(
        functools.partial(_rms_mm_kernel, epilogue=epilogue, emit_bf16=emit_bf16),
        out_shape=out_shape,
        grid_spec=pltpu.PrefetchScalarGridSpec(
            num_scalar_prefetch=1,
            grid=(M // tm, nj),
            in_specs=in_specs,
            out_specs=[out_spec] * len(out_shape),
            scratch_shapes=[pltpu.VMEM((tm, D), BF16)],
        ),
        compiler_params=_cparams(("parallel", "arbitrary")),
        name="rms_matmul_" + epilogue,
    )(flags, *args)
    return res if emit_bf16 else res[0]


def _two_source_specs(n_arr, tm, K, n_p):
    p_spec = pl.BlockSpec((tm, K), lambda i: (jnp.minimum(i, n_p - 1), 0))
    s_spec = pl.BlockSpec((tm, K), lambda i: (jnp.maximum(i - n_p, 0), 0))
    return [p_spec] * n_arr + [s_spec] * n_arr


def _mm_res_kernel(ap_ref, as_ref, w_ref, r_ref, o_ref, *, n_p):
    a = jnp.where(pl.program_id(0) >= n_p, as_ref[...], ap_ref[...])
    o_ref[...] = r_ref[...] + _dot(a.astype(BF16), w_ref[...])


def matmul_residual(a_p, a_s, w, res, *, tm):
    M, N = res.shape
    K = a_p.shape[1]
    n_p = a_p.shape[0] // tm
    assert a_p.shape[0] % tm == 0 and a_s.shape[0] % tm == 0
    return pl.pallas_call(
        functools.partial(_mm_res_kernel, n_p=n_p),
        out_shape=jax.ShapeDtypeStruct((M, N), F32),
        grid=(M // tm,),
        in_specs=_two_source_specs(1, tm, K, n_p) + [pl.BlockSpec((K, N), lambda i: (0, 0)),
                                                    pl.BlockSpec((tm, N), lambda i: (i, 0))],
        out_specs=pl.BlockSpec((tm, N), lambda i: (i, 0)),
        compiler_params=_cparams(("parallel",)),
        name="matmul_residual",
    )(a_p, a_s, w, res)


def _merge_mm_res_kernel(*refs, n_p):
    prompt, sample, (w_ref, r_ref, out_ref) = refs[:6], refs[6:12], refs[12:]
    is_s = pl.program_id(0) >= n_p
    o1, o2, o3, a1, a2, a3 = [jnp.where(is_s, s[...], p[...]) for p, s in zip(prompt, sample)]
    m = jnp.maximum(jnp.maximum(a1, a2), a3)
    e1, e2, e3 = jnp.exp(a1 - m), jnp.exp(a2 - m), jnp.exp(a3 - m)
    den = e1 + e2 + e3
    a = (e1 / den) * o1 + (e2 / den) * o2 + (e3 / den) * o3
    out_ref[...] = r_ref[...] + _dot(a.astype(BF16), w_ref[...])


def merge_matmul_residual(ol_p, ol_s, w, res, *, tm):
    M, N = res.shape
    K = ol_p[0].shape[1]
    n_p = ol_p[0].shape[0] // tm
    assert ol_p[0].shape[0] % tm == 0 and ol_s[0].shape[0] % tm == 0
    return pl.pallas_call(
        functools.partial(_merge_mm_res_kernel, n_p=n_p),
        out_shape=jax.ShapeDtypeStruct((M, N), F32),
        grid=(M // tm,),
        in_specs=_two_source_specs(6, tm, K, n_p) + [pl.BlockSpec((K, N), lambda i: (0, 0)),
                                                    pl.BlockSpec((tm, N), lambda i: (i, 0))],
        out_specs=pl.BlockSpec((tm, N), lambda i: (i, 0)),
        compiler_params=_cparams(("parallel",)),
        name="merge_matmul_residual",
    )(*ol_p, *ol_s, w, res)


def _rmsnorm_kernel(x_ref, g_ref, o_ref):
    o_ref[...] = _rms_norm_rows(x_ref[...], g_ref[...])


def rmsnorm_rows(x, g, *, tm):
    M, D = x.shape
    return pl.pallas_call(
        _rmsnorm_kernel,
        out_shape=jax.ShapeDtypeStruct((M, D), F32),
        grid=(M // tm,),
        in_specs=[pl.BlockSpec((tm, D), lambda i: (i, 0)), pl.BlockSpec((1, D), lambda i: (0, 0))],
        out_specs=pl.BlockSpec((tm, D), lambda i: (i, 0)),
        compiler_params=_cparams(("parallel",)),
        name="rmsnorm",
    )(x, g.reshape(1, D))


def _online_update(carry, s, mask, v):
    m_i, l_i, acc = carry
    s = jnp.where(mask, s, NEG)
    m_new = jnp.maximum(m_i, jnp.max(s, axis=-1, keepdims=True))
    alpha = jnp.exp(m_i - m_new)
    p = jnp.where(mask, jnp.exp(s - m_new), 0.0)
    return (m_new, alpha * l_i + jnp.sum(p, axis=-1, keepdims=True), alpha * acc + _dot(p.astype(BF16), v))


def _online_init(rows, width):
    return (jnp.full((rows, 1), NEG, F32), jnp.zeros((rows, 1), F32), jnp.zeros((rows, width), F32))


def _attend_heads(q, segments, n_heads):
    nq = q.shape[0]
    outs, lses = [], []
    for h in range(n_heads):
        qh = q[:, h * HEAD_DIM:(h + 1) * HEAD_DIM].astype(BF16)
        carry = _online_init(nq, HEAD_DIM)
        for k_rows, v_rows, mask in segments:
            carry = _online_update(carry, _dot_nt(qh, k_rows(h)) * SCALE, mask, v_rows(h))
        m, den, acc = carry
        outs.append(acc / den)
        lses.append(jnp.broadcast_to(m + jnp.log(den), (nq, HEAD_DIM)))
    return jnp.concatenate(outs, axis=1), jnp.concatenate(lses, axis=1)


def _pad_rows(x, n):
    return jnp.concatenate([x, jnp.zeros((n - x.shape[0], x.shape[1]), x.dtype)], axis=0)


def _band_attn_kernel(q_ref, kp_ref, kc_ref, vp_ref, vc_ref, o_ref, l_ref, *, nw):
    n = pl.program_id(2)
    kp, kc, vp, vc = kp_ref[...], kc_ref[...], vp_ref[...], vc_ref[...]
    i = lax.broadcasted_iota(jnp.int32, (A_BLOCK, 2 * A_BLOCK), 0)
    j = lax.broadcasted_iota(jnp.int32, (A_BLOCK, 2 * A_BLOCK), 1)
    rel = A_BLOCK + i - j
    bias = jnp.where((rel >= 0) & (rel <= nw) & ((j >= A_BLOCK) | (n > 0)), 0.0, NEG)
    q = q_ref[...] * (SCALE * LOG2E)
    outs, lses = [], []
    for h in range(A_HEADS):
        hs = slice(h * HEAD_DIM, (h + 1) * HEAD_DIM)
        kk = jnp.concatenate([kp[:, hs], kc[:, hs]], axis=0).astype(BF16)
        vv = jnp.concatenate([vp[:, hs], vc[:, hs]], axis=0).astype(BF16)
        s = _dot_nt(q[:, hs].astype(BF16), kk) + bias
        m = jnp.max(s, axis=-1, keepdims=True)
        e = jnp.exp2(s - m)
        den = jnp.sum(e, axis=-1, keepdims=True)
        outs.append(_dot(e.astype(BF16), vv) / den)
        lses.append(jnp.broadcast_to((m + jnp.log2(den)) * LN2, (A_BLOCK, HEAD_DIM)))
    o_ref[...] = jnp.concatenate(outs, axis=1)
    l_ref[...] = jnp.concatenate(lses, axis=1)


def band_attention(proj, g, window, dil, B, S):
    T, ncol = proj.shape
    L = S // dil
    nb = L // A_BLOCK
    hw = A_HEADS * HEAD_DIM
    per_res = ncol // hw
    view = proj.reshape(T // dil, dil * ncol)

    def spec(c, prev):
        def imap(b, r, n):
            nn = jnp.maximum(n - 1, 0) if prev else n
            return (b * nb + nn, r * per_res + g * 3 + c)
        return pl.BlockSpec((A_BLOCK, hw), imap)

    out_spec = pl.BlockSpec((A_BLOCK, hw), lambda b, r, n: (b * nb + n, r))
    o, l = pl.pallas_call(
        functools.partial(_band_attn_kernel, nw=window // dil),
        out_shape=[jax.ShapeDtypeStruct((B * L, dil * hw), F32)] * 2,
        grid=(B, dil, nb),
        in_specs=[spec(0, False), spec(1, True), spec(1, False), spec(2, True), spec(2, False)],
        out_specs=[out_spec, out_spec],
        compiler_params=_cparams(("parallel", "parallel", "arbitrary")),
        name=f"band_attention_d{dil}",
    )(view, view, view, view, view)
    return o.reshape(B * S, hw), l.reshape(B * S, hw)


def _a_step_kernel(q_ref, k_ref, v_ref, c_ref, o_ref, l_ref, s_ref, *, win, dil):
    W = c_ref.shape[1]
    Sd = q_ref.shape[0]
    hw = A_HEADS * HEAD_DIM
    knew, vnew = k_ref[...], v_ref[...]
    s_ref[0, pl.ds(0, W - Sd), :] = c_ref[0, pl.ds(Sd, W - Sd), :]
    s_ref[0, pl.ds(W - Sd, Sd), pl.ds(0, hw)] = knew
    s_ref[0, pl.ds(W - Sd, Sd), pl.ds(hw, hw)] = vnew

    def on_stride(dist):
        return (dist >= 0) & (dist <= win) & (jnp.bitwise_and(dist, dil - 1) == 0)

    dist_buf = (W + lax.broadcasted_iota(jnp.int32, (Sd, W), 0)) - lax.broadcasted_iota(jnp.int32, (Sd, W), 1)
    ncol = lax.broadcasted_iota(jnp.int32, (Sd, LANES), 1)
    dist_new = lax.broadcasted_iota(jnp.int32, (Sd, LANES), 0) - ncol
    knew_p, vnew_p = _pad_rows(knew, LANES), _pad_rows(vnew, LANES)
    buf_rows = lambda off: lambda h: c_ref[0, :, pl.ds(off + h * HEAD_DIM, HEAD_DIM)].astype(BF16)
    new_rows = lambda x: lambda h: x[:, h * HEAD_DIM:(h + 1) * HEAD_DIM].astype(BF16)
    segments = [(buf_rows(0), buf_rows(hw), on_stride(dist_buf)),
                (new_rows(knew_p), new_rows(vnew_p), on_stride(dist_new) & (ncol < Sd))]
    o, l = _attend_heads(q_ref[...], segments, A_HEADS)
    o_ref[...] = o
    l_ref[...] = l


def a_step(proj, g, window, dil, cache, Tp, Bd, Sd):
    assert dil & (dil - 1) == 0
    W = cache.shape[1]
    hw = A_HEADS * HEAD_DIM
    r0 = Tp // Sd
    col = lambda c: pl.BlockSpec((Sd, hw), lambda b: (r0 + b, g * 3 + c))
    row_out = pl.BlockSpec((Sd, hw), lambda b: (b, 0))
    return pl.pallas_call(
        functools.partial(_a_step_kernel, win=window, dil=dil),
        out_shape=[jax.ShapeDtypeStruct((Bd * Sd, hw), F32), jax.ShapeDtypeStruct((Bd * Sd, hw), F32),
                   jax.ShapeDtypeStruct(cache.shape, F32)],
        grid=(Bd,),
        in_specs=[col(0), col(1), col(2), pl.BlockSpec((1, W, 2 * hw), lambda b: (b, 0, 0))],
        out_specs=[row_out, row_out, pl.BlockSpec((1, W, 2 * hw), lambda b: (b, 0, 0))],
        compiler_params=_cparams(("parallel",)),
        name=f"a_step_d{dil}",
    )(proj, proj, proj, cache)


def _topk_cols(s, k, payload=None):
    N = s.shape[0]
    iota = lax.broadcasted_iota(jnp.int32, s.shape, 0).astype(F32)
    vals, picks = [], []
    for _ in range(k):
        m = jnp.max(s, axis=0, keepdims=True)
        idx = jnp.min(jnp.where(s == m, iota, float(N)), axis=0, keepdims=True)
        sel = iota == idx
        vals.append(m)
        if payload is None:
            picks.append(idx)
        else:
            picks.append(jnp.max(jnp.where(sel, payload, -1.0), axis=0, keepdims=True))
        s = jnp.where(sel, -jnp.inf, s)
    return vals, picks


def _topk_cols_many(arrs, k):
    N = arrs[0].shape[0]
    iota = lax.broadcasted_iota(jnp.int32, arrs[0].shape, 0).astype(F32)
    arrs = list(arrs)
    out = [([], []) for _ in arrs]
    for _ in range(k):
        for n, s in enumerate(arrs):
            m = jnp.max(s, axis=0, keepdims=True)
            idx = jnp.min(jnp.where(s == m, iota, float(N)), axis=0, keepdims=True)
            out[n][0].append(m)
            out[n][1].append(idx)
            arrs[n] = jnp.where(iota == idx, -jnp.inf, s)
    return out


def _topk_mask_cols(s, k):
    N = s.shape[0]
    iota = lax.broadcasted_iota(jnp.int32, s.shape, 0).astype(F32)
    chosen = jnp.zeros(s.shape, F32)
    for _ in range(k):
        m = jnp.max(s, axis=0, keepdims=True)
        idx = jnp.min(jnp.where(s == m, iota, float(N)), axis=0, keepdims=True)
        sel = iota == idx
        chosen = jnp.where(sel, 1.0, chosen)
        s = jnp.where(sel, -jnp.inf, s)
    return chosen


def _peer_route_kernel(x_ref, g_ref, wqh_ref, wql_ref, skh_ref, skl_ref,
                       hn_ref, e_ref, w_ref, xh_ref, xl_ref, et_ref, wt_ref, *, heads_per_step):
    h = pl.program_id(1)

    @pl.when(h == 0)
    def _():
        xn = _rms_norm_rows(x_ref[...], g_ref[...])
        hi, lo = _split2(xn)
        xh_ref[...] = hi
        xl_ref[...] = lo
        hn_ref[...] = hi

    xh, xl = xh_ref[...], xl_ref[...]
    K = PEER_TOPK
    b_iota = lax.broadcasted_iota(jnp.int32, (8, LANES), 0)
    jobs = []
    for hh in range(heads_per_step):
        cs = pl.ds(hh * PEER_QDIM, PEER_QDIM)
        wqh, wql = wqh_ref[:, cs], wql_ref[:, cs]
        q = _dot(xh, wqh) + _dot(xh, wql) + _dot(xl, wqh)
        sts = []
        for z in range(2):
            qh, ql = _split2(q[:, z * PEER_KEYS:(z + 1) * PEER_KEYS])
            skh, skl = skh_ref[z], skl_ref[z]
            sts.append(_dot_nt(skh, qh) + _dot_nt(skl, qh) + _dot_nt(skh, ql))
        row = pl.multiple_of((h * heads_per_step + hh) * K, K)
        jobs += [(sts, row, c0) for c0 in range(0, xh.shape[0], LANES)]
    for sts, row, c0 in jobs:
        (v1, i1), (v2, i2) = _topk_cols_many([st[:, c0:c0 + LANES] for st in sts], K)
        v2_lo, v2_hi = jnp.concatenate(v2[:8], axis=0), jnp.concatenate(v2[8:], axis=0)
        i2_lo, i2_hi = jnp.concatenate(i2[:8], axis=0), jnp.concatenate(i2[8:], axis=0)
        e_of = lambda a, i2rows: i1[a] * float(PEER_KEYS) + i2rows
        cand = [v1[0] + v2_lo, v1[0] + v2_hi, v1[1] + v2_lo]
        cand_e = [e_of(0, i2_lo), e_of(0, i2_hi), e_of(1, i2_lo)]
        for a in range(2, 8):
            cand.append(jnp.where(b_iota < K // (a + 1), v1[a] + v2_lo, -jnp.inf))
            cand_e.append(e_of(a, i2_lo))
        cand.append(jnp.concatenate(v1[8:], axis=0) + v2[0])
        cand_e.append(jnp.concatenate(i1[8:], axis=0) * float(PEER_KEYS) + i2[0])
        best, best_e = _topk_cols(jnp.concatenate(cand, axis=0), K, payload=jnp.concatenate(cand_e, axis=0))
        bs = jnp.concatenate(best, axis=0)
        ex = jnp.exp(bs - best[0])
        et_ref[pl.ds(row, K), pl.ds(c0, LANES)] = jnp.concatenate(best_e, axis=0)
        wt_ref[pl.ds(row, K), pl.ds(c0, LANES)] = ex / jnp.sum(ex, axis=0, keepdims=True)

    @pl.when(h == pl.num_programs(1) - 1)
    def _():
        e_ref[...] = et_ref[...].T.astype(jnp.int32)
        w_ref[...] = wt_ref[...].T


def peer_route(x, g, wq_hi, wq_lo, sk_hi, sk_lo, *, tm, heads_per_step=8):
    T, D = x.shape
    assert T % tm == 0 and tm % LANES == 0
    nslot = PEER_HEADS * PEER_TOPK
    wq_spec = pl.BlockSpec((D, heads_per_step * PEER_QDIM), lambda i, h: (0, h))
    return pl.pallas_call(
        functools.partial(_peer_route_kernel, heads_per_step=heads_per_step),
        out_shape=[jax.ShapeDtypeStruct((T, D), BF16),
                   jax.ShapeDtypeStruct((T, nslot), jnp.int32),
                   jax.ShapeDtypeStruct((T, nslot), F32)],
        grid=(T // tm, PEER_HEADS // heads_per_step),
        in_specs=[pl.BlockSpec((tm, D), lambda i, h: (i, 0)),
                  pl.BlockSpec((1, D), lambda i, h: (0, 0)),
                  wq_spec, wq_spec,
                  pl.BlockSpec((2, PEER_KEYS, PEER_QDIM // 2), lambda i, h: (0, 0, 0)),
                  pl.BlockSpec((2, PEER_KEYS, PEER_QDIM // 2), lambda i, h: (0, 0, 0))],
        out_specs=[pl.BlockSpec((tm, D), lambda i, h: (i, 0)),
                   pl.BlockSpec((tm, nslot), lambda i, h: (i, 0)),
                   pl.BlockSpec((tm, nslot), lambda i, h: (i, 0))],
        scratch_shapes=[pltpu.VMEM((tm, D), BF16), pltpu.VMEM((tm, D), BF16),
                        pltpu.VMEM((nslot, tm), F32), pltpu.VMEM((nslot, tm), F32)],
        compiler_params=_cparams(("parallel", "arbitrary")),
        name="peer_route",
    )(x, g.reshape(1, D), wq_hi, wq_lo, sk_hi, sk_lo)


HI16 = -65536


def _peer_expert_kernel(hn_ref, e_ref, w_ref, x_ref, ua_ref, ub_ref, va_ref, vb_ref, o_ref, gs_ref, acc_ref,
                        *, tm, rows_per_chunk):
    c = pl.program_id(1)
    nk = PEER_KEYS
    half = nk // 2

    @pl.when(c == 0)
    def _():
        acc_ref[...] = jnp.zeros_like(acc_ref)
        sub = lax.broadcasted_iota(jnp.int32, (nk, nk), 0)

        def body(t8, carry):
            r0 = pl.multiple_of(t8 * 8, 8)
            e8 = e_ref[pl.ds(r0, 8), :]
            w8 = w_ref[pl.ds(r0, 8), :]
            for r in range(8):
                e, w = e8[r:r + 1], w8[r:r + 1]
                i1 = jnp.right_shift(e, 7)
                i2 = jnp.bitwise_and(e, nk - 1)
                pt = jnp.where(sub == i1, w, 0.0).astype(BF16)
                qt = jnp.where(sub == i2, 1.0, 0.0).astype(BF16)
                gt = _dot_nt(pt, qt).astype(BF16).astype(F32)
                bits = lax.bitcast_convert_type(gt, jnp.int32)
                word = jnp.bitwise_or(lax.shift_right_logical(bits[:half], 16), bits[half:])
                gs_ref[pl.ds(pl.multiple_of((r0 + r) * G_PITCH, 8), half), :] = word
            return carry

        lax.fori_loop(0, tm // 8, body, 0)

    hn = hn_ref[...]
    act_a = jax.nn.gelu(_dot_nt(hn, ua_ref[...]))
    act_b = jax.nn.gelu(_dot_nt(hn, ub_ref[...]))
    pa, pb = [], []
    for r in range(rows_per_chunk):
        word = gs_ref[pl.ds(c * rows_per_chunk + r, tm, stride=G_PITCH), :]
        g_lo = lax.bitcast_convert_type(jnp.left_shift(word, 16), F32)
        g_hi = lax.bitcast_convert_type(jnp.bitwise_and(word, HI16), F32)
        pa.append((act_a[:, r * nk:(r + 1) * nk] * g_lo).astype(BF16))
        pb.append((act_b[:, r * nk:(r + 1) * nk] * g_hi).astype(BF16))
    acc_ref[...] += (_dot(jnp.concatenate(pa, axis=1), va_ref[...])
                     + _dot(jnp.concatenate(pb, axis=1), vb_ref[...]))

    @pl.when(c == pl.num_programs(1) - 1)
    def _():
        o_ref[...] = x_ref[...] + acc_ref[...]


def peer_experts(hn, e, w, x, u, v, *, tm, chunk):
    T, D = x.shape
    n_exp = v.shape[0]
    nslot = e.shape[1]
    blk = chunk // 2
    rows_per_chunk = blk // PEER_KEYS
    n_chunks = n_exp // chunk
    lo = pl.BlockSpec((blk, D), lambda i, c: (c, 0))
    hi = pl.BlockSpec((blk, D), lambda i, c: (c + n_chunks, 0))
    return pl.pallas_call(
        functools.partial(_peer_expert_kernel, tm=tm, rows_per_chunk=rows_per_chunk),
        out_shape=jax.ShapeDtypeStruct((T, D), F32),
        grid=(T // tm, n_chunks),
        in_specs=[pl.BlockSpec((tm, D), lambda i, c: (i, 0)),
                  pl.BlockSpec((tm, nslot), lambda i, c: (i, 0)),
                  pl.BlockSpec((tm, nslot), lambda i, c: (i, 0)),
                  pl.BlockSpec((tm, D), lambda i, c: (i, 0)),
                  lo, hi, lo, hi],
        out_specs=pl.BlockSpec((tm, D), lambda i, c: (i, 0)),
        scratch_shapes=[pltpu.VMEM((tm * G_PITCH, PEER_KEYS), jnp.int32), pltpu.VMEM((tm, D), F32)],
        compiler_params=_cparams(("parallel", "arbitrary")),
        name="peer_experts",
    )(hn, e, w, x, u, u, v, v)


def peer_layer(x, g, w_q, subkeys, u, v, *, tm_route, tm_exp, chunk):
    wq_hi, wq_lo = _split2(w_q)
    sk_hi, sk_lo = _split2(subkeys)
    hn, e, w = peer_route(x, g, wq_hi, wq_lo, sk_hi, sk_lo, tm=tm_route)
    return peer_experts(hn, e, w, x, u.astype(BF16), v.astype(BF16), tm=tm_exp, chunk=chunk)


def _compress_core(get_x, w1k_ref, w1v_ref, pe_ref, w1f_ref, w2k_ref, w2v_ref, cos_ref, sa_ref, sb_ref,
                   kc_ref, vc_ref, acc_ref, n_cmp):
    nch = acc_ref.shape[0]
    for l in range(CMP_STRIDE):
        x = get_x(l)
        yk = _dot(x[:, :KV_W].astype(BF16), w1k_ref[l])
        yv = _dot(x[:, KV_W:].astype(BF16), w1v_ref[l])
        if l == 0:
            acc_ref[:, pl.ds(0, 2 * KV_W)] = yk
            acc_ref[:, pl.ds(2 * KV_W, 2 * KV_W)] = yv
        else:
            acc_ref[:, pl.ds(0, 2 * KV_W)] += yk
            acc_ref[:, pl.ds(2 * KV_W, 2 * KV_W)] += yv
    keep = lax.broadcasted_iota(jnp.int32, (nch, KV_W), 0) < n_cmp
    for z, (w2_ref, out_ref) in enumerate(((w2k_ref, kc_ref), (w2v_ref, vc_ref))):
        ph, pl_ = _split2(pe_ref[z])
        wh, wl = _split2(w1f_ref[z])
        pe_t = _dot(ph, wh) + _dot(ph, wl) + _dot(pl_, wh)
        pe_t = jnp.concatenate([pe_t] * B_KV_HEADS, axis=1)
        lo = acc_ref[:, pl.ds(z * 2 * KV_W, KV_W)]
        hi = acc_ref[:, pl.ds(z * 2 * KV_W + KV_W, KV_W)]
        hid = jax.nn.gelu(lo + pltpu.roll(hi, nch - 1, 0) + pe_t)
        out = _dot(hid.astype(BF16), w2_ref[...])
        if z == 0:
            out = _rope_lanes(out, cos_ref[...], sa_ref[...], sb_ref[...])
        out_ref[0] = jnp.where(keep, out, 0.0)


def _compress_prompt_kernel(x0, x1, x2, x3, *rest, n_cmp):
    slabs = (x0, x1, x2, x3)
    nch = rest[-1].shape[0]

    def get_x(l):
        return jnp.concatenate([s[pl.ds(l, nch, stride=CMP_STRIDE), :] for s in slabs], axis=1)

    _compress_core(get_x, *rest, n_cmp=n_cmp)


def _compress_weights(w_cmp1, w_cmp2, pe_cmp, n_chunk):
    eye = jnp.eye(B_KV_HEADS, dtype=F32)

    def per_head(w):
        bd = jnp.einsum('hg,...df->...hdgf', eye, w)
        return bd.reshape(w.shape[:-2] + (KV_W, KV_W)).astype(BF16)

    w1 = per_head(w_cmp1.reshape(2, 2, CMP_STRIDE, HEAD_DIM, -1))
    w1 = jnp.concatenate([w1[:, 0], w1[:, 1]], axis=-1)
    w2 = per_head(w_cmp2)
    pe = pe_cmp.reshape(2, 1, CMP_LEN * HEAD_DIM)
    w1f = w_cmp1.reshape(2, CMP_LEN * HEAD_DIM, -1)
    end_pos = jnp.arange(n_chunk, dtype=jnp.int32) * CMP_STRIDE + (CMP_LEN - 1)
    tabs = _rope_tables(end_pos, KV_W)
    return (w1[0], w1[1], pe, w1f, w2[0], w2[1]) + tabs


def _const_specs(arrs):
    return [pl.BlockSpec(a.shape, (lambda *i, nd=a.ndim: (0,) * nd), pipeline_mode=pl.Buffered(1)) for a in arrs]


def compress_prompt(kv, cw, B, S):
    nch = S // CMP_STRIDE
    slab = lambda s: pl.BlockSpec((S, LANES), lambda b: (b, s), pipeline_mode=pl.Buffered(1))
    out = pl.BlockSpec((1, nch, KV_W), lambda b: (b, 0, 0))
    return pl.pallas_call(
        functools.partial(_compress_prompt_kernel, n_cmp=nch - 1),
        out_shape=[jax.ShapeDtypeStruct((B, nch, KV_W), F32)] * 2,
        grid=(B,),
        in_specs=[slab(s) for s in range(4)] + _const_specs(cw),
        out_specs=[out, out],
        scratch_shapes=[pltpu.VMEM((nch, 4 * KV_W), F32)],
        compiler_params=_cparams(("parallel",)),
        name="compress_prompt",
    )(kv, kv, kv, kv, *cw)


def _compress_sample_kernel(pt_ref, *refs, n_cmp, pages_per_step):
    n_pg = 4 * pages_per_step
    pages, rest, x_ref = refs[:n_pg], refs[n_pg:-1], refs[-1]
    j = pl.program_id(1)
    ch_per_page = pages[0].shape[1] // CMP_STRIDE
    for p in range(pages_per_step):
        r0 = pl.multiple_of((j * pages_per_step + p) * ch_per_page, ch_per_page)
        for s in range(4):
            for l in range(CMP_STRIDE):
                x_ref[l, pl.ds(r0, ch_per_page), pl.ds(s * LANES, LANES)] = (
                    pages[p * 4 + s][0, pl.ds(l, ch_per_page, stride=CMP_STRIDE), :])

    @pl.when(j == pl.num_programs(1) - 1)
    def _():
        _compress_core(lambda l: x_ref[l], *rest, n_cmp=n_cmp)


def compress_sample(cache, page_table, cw, *, pages_per_step):
    Bd, n_pages = page_table.shape
    page = cache.shape[1]
    nch = n_pages * page // CMP_STRIDE
    P = pages_per_step

    def pspec(p, s):
        return pl.BlockSpec((1, page, LANES), lambda b, j, pt: (pt[b, j * P + p], 0, s))

    out = pl.BlockSpec((1, nch, KV_W), lambda b, j, pt: (b, 0, 0))
    return pl.pallas_call(
        functools.partial(_compress_sample_kernel, n_cmp=nch - 1, pages_per_step=P),
        out_shape=[jax.ShapeDtypeStruct((Bd, nch, KV_W), F32)] * 2,
        grid_spec=pltpu.PrefetchScalarGridSpec(
            num_scalar_prefetch=1,
            grid=(Bd, n_pages // P),
            in_specs=[pspec(p, s) for p in range(P) for s in range(4)] + _const_specs(cw),
            out_specs=[out, out],
            scratch_shapes=[pltpu.VMEM((nch, 4 * KV_W), F32),
                            pltpu.VMEM((CMP_STRIDE, nch, 2 * KV_W), F32)],
        ),
        compiler_params=_cparams(("parallel", "arbitrary")),
        name="compress_sample",
    )(page_table, *([cache] * (4 * P)), *cw)


def _select_blocks(imp, wsel, blk, cur):
    i_hi = imp.astype(BF16)
    r1 = imp - i_hi.astype(F32)
    i_mid = r1.astype(BF16)
    i_lo = (r1 - i_mid.astype(F32)).astype(BF16)
    imp_sel = _dot(i_hi, wsel) + _dot(i_mid, wsel) + _dot(i_lo, wsel)
    forced = (blk == 0) | (blk == cur) | (blk == cur - 1)
    imp_sel = jnp.where(blk > cur, -jnp.inf, jnp.where(forced, jnp.inf, imp_sel))
    return _topk_mask_cols(imp_sel.T, N_SEL).T


def _tile_block_hits(chosen_q, k0, kt_size):
    n_sb = chosen_q.shape[1]
    blk = lax.broadcasted_iota(jnp.int32, (n_sb, kt_size), 0)
    key_blk = jnp.right_shift(k0 + lax.broadcasted_iota(jnp.int32, (n_sb, kt_size), 1), 6)
    expand = jnp.where(blk == key_blk, 1.0, 0.0).astype(BF16)
    return _dot(chosen_q, expand)


def _tile_block_mask(chosen_q, k0, kt_size):
    return _tile_block_hits(chosen_q, k0, kt_size) > 0.5


def _nsa_prompt_kernel(q_ref, gate_ref, kc_ref, vc_ref, ks_ref, vs_ref, kw_ref, vw_ref, wsel_ref,
                       o_ref, *, n_cmp, kt_size):
    n = pl.program_id(1)
    t0 = n * Q_BLOCK
    G, TQ = B_GROUP, Q_BLOCK
    R = G * TQ
    NC = kc_ref.shape[1]
    NSB = wsel_ref.shape[1]
    wsel = wsel_ref[...]
    gt = gate_ref[...]
    row_c = lax.broadcasted_iota(jnp.int32, (R, NC), 0)
    col_c = lax.broadcasted_iota(jnp.int32, (R, NC), 1)
    valid_c = (col_c * CMP_STRIDE + (CMP_LEN - 1) <= t0 + jnp.bitwise_and(row_c, TQ - 1)) & (col_c < n_cmp)
    bias_c = jnp.where(valid_c, 0.0, NEG)
    sees_cmp = t0 + jnp.bitwise_and(lax.broadcasted_iota(jnp.int32, (R, 1), 0), TQ - 1) >= CMP_LEN - 1
    blk = lax.broadcasted_iota(jnp.int32, (TQ, NSB), 1)
    cur = jnp.right_shift(t0 + lax.broadcasted_iota(jnp.int32, (TQ, NSB), 0), 6)
    WK = B_WINDOW + TQ
    w0 = pl.multiple_of(jnp.maximum(t0 - B_WINDOW, 0), TQ)
    wpos = w0 + lax.broadcasted_iota(jnp.int32, (R, WK), 1)
    rel = t0 + jnp.bitwise_and(lax.broadcasted_iota(jnp.int32, (R, WK), 0), TQ - 1) - wpos
    bias_w = jnp.where((rel >= 0) & (rel <= B_WINDOW), 0.0, NEG)
    n_kt = (t0 + TQ + kt_size - 1) // kt_size
    k_last = pl.multiple_of((n_kt - 1) * kt_size, kt_size)
    causal_last = jnp.where(k_last + lax.broadcasted_iota(jnp.int32, (TQ, kt_size), 1)
                            <= t0 + lax.broadcasted_iota(jnp.int32, (TQ, kt_size), 0), 0.0, NEG)
    ones_col = jnp.where(lax.broadcasted_iota(jnp.int32, (kt_size, HEAD_DIM), 1) == 0, 1.0, 0.0).astype(BF16)

    head_cols = [pl.ds(h * HEAD_DIM, HEAD_DIM) for h in range(B_KV_HEADS)]

    qbs, o_cs, chosens = [], [], []
    for h, hs in enumerate(head_cols):
        qs = jnp.concatenate([q_ref[:, pl.ds((h * G + g) * HEAD_DIM, HEAD_DIM)] for g in range(G)], axis=0)
        qs = qs * (SCALE * LOG2E)
        s_c = _dot_nt3(qs, kc_ref[0, :, hs]) + bias_c
        e_c = jnp.exp2(s_c - jnp.max(s_c, axis=-1, keepdims=True))
        p_c = e_c * jnp.where(sees_cmp, 1.0 / jnp.sum(e_c, axis=-1, keepdims=True), 0.0)
        imp = p_c[0:TQ]
        for g in range(1, G):
            imp = imp + p_c[g * TQ:(g + 1) * TQ]
        qbs.append(qs.astype(BF16))
        o_cs.append(_dot(p_c.astype(BF16), vc_ref[0, :, hs].astype(BF16)))
        chosens.append(_select_blocks(imp, wsel, blk, cur).astype(BF16))

    def sel_tile(h, k0, carry, extra_bias):
        m_i, acc = carry
        hs = head_cols[h]
        bias = (_tile_block_hits(chosens[h], k0, kt_size) - 1.0) * (-NEG)
        if extra_bias is not None:
            bias = bias + extra_bias
        s = _dot_nt(qbs[h], ks_ref[pl.ds(k0, kt_size), hs]).reshape(G, TQ, kt_size) + bias[None]
        s = s.reshape(R, kt_size)
        m_new = jnp.maximum(m_i, jnp.max(s, axis=-1, keepdims=True))
        p = jnp.exp2(s - m_new).astype(BF16)
        v1 = jnp.concatenate([vs_ref[pl.ds(k0, kt_size), hs], ones_col], axis=1)
        return m_new, jnp.exp2(m_i - m_new) * acc + _dot(p, v1)

    def sel_body(kt, carries):
        k0 = pl.multiple_of(kt * kt_size, kt_size)
        return tuple(sel_tile(h, k0, carries[h], None) for h in range(B_KV_HEADS))

    init = tuple((jnp.full((R, 1), NEG, F32), jnp.zeros((R, 2 * HEAD_DIM), F32)) for _ in range(B_KV_HEADS))
    carries = lax.fori_loop(0, n_kt - 1, sel_body, init)

    outs = []
    for h, hs in enumerate(head_cols):
        _, acc_s = sel_tile(h, k_last, carries[h], causal_last)
        o_s = acc_s[:, :HEAD_DIM] / acc_s[:, HEAD_DIM:HEAD_DIM + 1]
        s_w = _dot_nt(qbs[h], kw_ref[pl.ds(w0, WK), hs]) + bias_w
        e_w = jnp.exp2(s_w - jnp.max(s_w, axis=-1, keepdims=True))
        p_w = e_w * (1.0 / jnp.sum(e_w, axis=-1, keepdims=True))
        o_w = _dot(p_w.astype(BF16), vw_ref[pl.ds(w0, WK), hs])
        o_c = o_cs[h]
        for g in range(G):
            rs = slice(g * TQ, (g + 1) * TQ)
            c = 3 * (h * G + g)
            outs.append(gt[:, c:c + 1] * o_c[rs] + gt[:, c + 1:c + 2] * o_s[rs] + gt[:, c + 2:c + 3] * o_w[rs])
    o_ref[...] = jnp.concatenate(outs, axis=1)


def nsa_prompt(q, gate, kc, vc, kvb, wsel, B, S, *, kt_size):
    nq = q.shape[1]
    nqb = S // Q_BLOCK
    NC = kc.shape[1]
    comp = lambda c: pl.BlockSpec((S, KV_W), lambda b, n: (b, c))
    cmp_spec = pl.BlockSpec((1, NC, KV_W), lambda b, n: (b, 0, 0))
    return pl.pallas_call(
        functools.partial(_nsa_prompt_kernel, n_cmp=S // CMP_STRIDE - 1, kt_size=kt_size),
        out_shape=jax.ShapeDtypeStruct((B * S, nq), F32),
        grid=(B, nqb),
        in_specs=[pl.BlockSpec((Q_BLOCK, nq), lambda b, n: (b * nqb + n, 0)),
                  pl.BlockSpec((Q_BLOCK, LANES), lambda b, n: (b * nqb + n, 0)),
                  cmp_spec, cmp_spec, comp(2), comp(3), comp(4), comp(5),
                  pl.BlockSpec(wsel.shape, lambda b, n: (0, 0))],
        out_specs=pl.BlockSpec((Q_BLOCK, nq), lambda b, n: (b * nqb + n, 0)),
        compiler_params=_cparams(("parallel", "arbitrary")),
        name="nsa_prompt",
    )(q, gate, kc, vc, kvb, kvb, kvb, kvb, wsel)


def _nsa_step_kernel(pt_ref, q_ref, gate_ref, kc_ref, vc_ref, new_ref, win_ref, wsel_ref, *rest,
                     past, n_cmp, pages_per_step):
    P = pages_per_step
    pages = rest[:P]
    o_ref, st_ref, a_ref, ch_ref, m_ref, l_ref, acc_ref, oc_ref = rest[P:]
    del pt_ref
    j = pl.program_id(1)
    G, KVH = B_GROUP, B_KV_HEADS
    Sd = q_ref.shape[0]
    R = B_HEADS * Sd
    page = pages[0].shape[1]
    kt = P * page
    row_head = lambda shape: lax.broadcasted_iota(jnp.int32, shape, 0) // (G * Sd)
    own = lambda width: row_head((R, width)) == lax.broadcasted_iota(jnp.int32, (R, width), 1) // HEAD_DIM

    def diag(x):
        rows = G * Sd
        return jnp.concatenate([x[h * rows:(h + 1) * rows, h * HEAD_DIM:(h + 1) * HEAD_DIM]
                                for h in range(KVH)], axis=0)

    @pl.when(j == 0)
    def _():
        q = q_ref[...]
        a = jnp.concatenate([q[:, hd * HEAD_DIM:(hd + 1) * HEAD_DIM] for hd in range(B_HEADS)], axis=0)
        a = jnp.where(own(KV_W), jnp.concatenate([a] * KVH, axis=1), 0.0)
        a_ref[...] = a.astype(BF16)
        NC = kc_ref.shape[1]
        valid_c = lax.broadcasted_iota(jnp.int32, (R, NC), 1) < n_cmp
        p_c = _softmax_rows(_dot_nt3(a, kc_ref[0]) * SCALE, valid_c)
        oc_ref[...] = diag(_dot(p_c.astype(BF16), vc_ref[0].astype(BF16)))
        pooled = jnp.sum(p_c.reshape(KVH, G, Sd, NC), axis=1, keepdims=True)
        imp = jnp.broadcast_to(pooled, (KVH, G, Sd, NC)).reshape(R, NC)
        NSB = wsel_ref.shape[1]
        blk = lax.broadcasted_iota(jnp.int32, (R, NSB), 1)
        srow = jnp.bitwise_and(lax.broadcasted_iota(jnp.int32, (R, NSB), 0), Sd - 1)
        cur = jnp.right_shift(past + srow, 6)
        ch_ref[...] = _select_blocks(imp, wsel_ref[...], blk, cur).astype(BF16)
        m_ref[...] = jnp.full(m_ref.shape, NEG, F32)
        l_ref[...] = jnp.zeros(l_ref.shape, F32)
        acc_ref[...] = jnp.zeros(acc_ref.shape, F32)

    a = a_ref[...]
    ks = jnp.concatenate([pg[0, :, pl.ds(0, KV_W)] for pg in pages], axis=0).astype(BF16)
    vs = jnp.concatenate([pg[0, :, pl.ds(KV_W, KV_W)] for pg in pages], axis=0).astype(BF16)
    mk = _tile_block_mask(ch_ref[...], j * kt, kt)
    m_n, l_n, acc_n = _online_update((m_ref[...], l_ref[...], acc_ref[...]), _dot_nt(a, ks) * SCALE, mk, vs)
    m_ref[...] = m_n
    l_ref[...] = l_n
    acc_ref[...] = acc_n

    @pl.when(j == pl.num_programs(1) - 1)
    def _():
        new = _pad_rows(new_ref[...], LANES)
        comp = lambda c: new[:, c * KV_W:(c + 1) * KV_W].astype(BF16)
        srow = jnp.bitwise_and(lax.broadcasted_iota(jnp.int32, (R, LANES), 0), Sd - 1)
        scol = lax.broadcasted_iota(jnp.int32, (R, LANES), 1)
        causal_new = (scol <= srow) & (scol < Sd)
        nblk = past // SEL_BLOCK
        in_new = ch_ref[:, nblk:nblk + 1].astype(F32) > 0.5
        _, l_s, acc_s = _online_update((m_n, l_n, acc_n), _dot_nt(a, comp(2)) * SCALE, in_new & causal_new, comp(3))
        o_s = diag(acc_s) / jnp.where(l_s > 0, l_s, 1.0)

        W = win_ref.shape[1]
        wrow = jnp.bitwise_and(lax.broadcasted_iota(jnp.int32, (R, W), 0), Sd - 1)
        rel = W + wrow - lax.broadcasted_iota(jnp.int32, (R, W), 1)
        carry = _online_update(_online_init(R, KV_W), _dot_nt(a, win_ref[0, :, pl.ds(0, KV_W)].astype(BF16)) * SCALE,
                               (rel >= 0) & (rel <= B_WINDOW), win_ref[0, :, pl.ds(KV_W, KV_W)].astype(BF16))
        _, l_w, acc_w = _online_update(carry, _dot_nt(a, comp(4)) * SCALE, causal_new, comp(5))
        o_w = diag(acc_w) / jnp.where(l_w > 0, l_w, 1.0)
        o_c = oc_ref[...]
        gt = gate_ref[...]
        outs = []
        for hd in range(B_HEADS):
            rs = slice(hd * Sd, (hd + 1) * Sd)
            c = 3 * hd
            outs.append(gt[:, c:c + 1] * o_c[rs] + gt[:, c + 1:c + 2] * o_s[rs] + gt[:, c + 2:c + 3] * o_w[rs])
        o_ref[...] = jnp.concatenate(outs, axis=1)
        st_ref[0, pl.ds(0, W - Sd), :] = win_ref[0, pl.ds(Sd, W - Sd), :]
        st_ref[0, pl.ds(W - Sd, Sd), :] = new_ref[:, pl.ds(4 * KV_W, 2 * KV_W)]


def nsa_step(q, gate, kc, vc, kv, win, cache, page_table, wsel, Tp, Bd, Sd, *, pages_per_step):
    nq = q.shape[1]
    n_pages = page_table.shape[1]
    page = cache.shape[1]
    past = n_pages * page
    W = win.shape[1]
    NC = kc.shape[1]
    P = pages_per_step
    r0 = Tp // Sd
    R = B_HEADS * Sd
    rows = lambda w: pl.BlockSpec((Sd, w), lambda b, j, pt: (r0 + b, 0))
    per_b = lambda s1, s2: pl.BlockSpec((1, s1, s2), lambda b, j, pt: (b, 0, 0))
    pspec = lambda p: pl.BlockSpec((1, page, 2 * KV_W), lambda b, j, pt: (pt[b, j * P + p], 0, 1))
    return pl.pallas_call(
        functools.partial(_nsa_step_kernel, past=past, n_cmp=past // CMP_STRIDE - 1, pages_per_step=P),
        out_shape=[jax.ShapeDtypeStruct((Bd * Sd, nq), F32), jax.ShapeDtypeStruct(win.shape, F32)],
        grid_spec=pltpu.PrefetchScalarGridSpec(
            num_scalar_prefetch=1,
            grid=(Bd, n_pages // P),
            in_specs=[rows(nq), rows(LANES), per_b(NC, KV_W), per_b(NC, KV_W), rows(kv.shape[1]),
                      per_b(W, 2 * KV_W), pl.BlockSpec(wsel.shape, lambda b, j, pt: (0, 0))]
                     + [pspec(p) for p in range(P)],
            out_specs=[pl.BlockSpec((Sd, nq), lambda b, j, pt: (b, 0)), per_b(W, 2 * KV_W)],
            scratch_shapes=[pltpu.VMEM((R, KV_W), BF16), pltpu.VMEM((R, wsel.shape[1]), BF16),
                            pltpu.VMEM((R, 1), F32), pltpu.VMEM((R, 1), F32), pltpu.VMEM((R, KV_W), F32),
                            pltpu.VMEM((R, HEAD_DIM), F32)],
        ),
        compiler_params=_cparams(("parallel", "arbitrary")),
        name="nsa_step",
    )(page_table, q, gate, kc, vc, kv, win, wsel, *([cache] * P))


def _rope_tables(pos, width):
    half = ROT_DIM // 2
    inv = jnp.exp(-jnp.log(jnp.float32(ROPE_THETA)) * jnp.arange(half, dtype=F32) * (2.0 / ROT_DIM))
    ang = pos.astype(F32)[:, None] * inv[None, :]
    cos, sin = jnp.cos(ang), jnp.sin(ang)
    T = pos.shape[0]
    z8 = jnp.zeros((T, half), F32)
    rest = HEAD_DIM - ROT_DIM
    c = jnp.concatenate([cos, cos, jnp.ones((T, rest), F32)], axis=1)
    sa = jnp.concatenate([-sin, z8, jnp.zeros((T, rest), F32)], axis=1)
    sb = jnp.concatenate([z8, sin, jnp.zeros((T, rest), F32)], axis=1)
    reps = width // HEAD_DIM
    return tuple(jnp.tile(t, (1, reps)) for t in (c, sa, sb))


def _overlap_weights(n_cmp, n_cmp_pad, n_sb, n_sb_pad):
    j = jnp.arange(n_sb)
    cidx = (SEL_BLOCK // CMP_STRIDE) * j[:, None] - 1 + jnp.arange(len(SEL_OVERLAP_W))[None, :]
    wts = jnp.where((cidx >= 0) & (cidx < n_cmp), jnp.asarray(SEL_OVERLAP_W, F32), 0.0)
    onehot = (jnp.clip(cidx, 0, n_cmp - 1)[:, :, None] == jnp.arange(n_cmp_pad)[None, None, :]).astype(F32)
    wsel = jnp.einsum('jo,jon->nj', wts, onehot)
    return jnp.pad(wsel, ((0, 0), (0, n_sb_pad - n_sb))).astype(BF16)


def _pick(n, cands):
    for c in cands:
        if n % c == 0:
            return c
    return n


def _pad_to(n, m):
    return -(-n // m) * m


def kernel(x_prompt, x_sample, cache_a_g1, cache_a_g2, cache_a_g3, cache_b_kv, cache_b_win, page_table, g_mix, g_ffn, w_in_a, w_o_a, g_kv, w_kv_b, w_cmp1, w_cmp2, pe_cmp, w_qg_b, b_gate_b, w_o_b, w_peer_q, peer_subkeys, peer_u, peer_v, g_final):
    B, S, D = x_prompt.shape
    Bd, Sd, _ = x_sample.shape
    Tp, Ts = B * S, Bd * Sd
    T = Tp + Ts
    n_phys, page = cache_b_kv.shape[:2]
    n_pages = page_table.shape[1]
    past = n_pages * page
    hw = A_HEADS * HEAD_DIM
    tm_big = _pick(T, (1280, 640, 256, 128, 64, 32, 16, 8))
    tm_mid = _pick(T, (640, 256, 128, 64, 32, 16, 8))
    tm_peer = _pick(T, (256, 128, 64, 32, 16, 8))
    tm_exp = _pick(T, (520, 512, 256, 128, 64, 32, 16, 8))
    tm_rows = _pick(math.gcd(Tp, Ts), (256, 128, 64, 32, 16, 8))
    pages_per_step = _pick(n_pages, (8, 4, 2, 1))

    x = jnp.concatenate([x_prompt.reshape(Tp, D), x_sample.reshape(Ts, D)], axis=0)
    pos_p = jnp.arange(S, dtype=jnp.int32)
    pos_s = past + jnp.arange(Sd, dtype=jnp.int32)
    pos = jnp.concatenate([jnp.tile(pos_p, B), jnp.tile(pos_s, Bd)])
    tab512 = _rope_tables(pos, 512)
    tab256 = tuple(t[:, :KV_W] for t in tab512)

    n_grp = len(A_GROUPS)
    proj = rms_matmul(x, g_mix[0], w_in_a[0].astype(BF16), tm=tm_big, tn=hw, epilogue="rope",
                      flags=jnp.asarray([1, 1, 0] * n_grp, jnp.int32), tables=tab512)
    caches_a = (cache_a_g1, cache_a_g2, cache_a_g3)
    op, lp, os_, ls_, a_prompt, a_sample = [], [], [], [], [], []
    for g, (win, dil) in enumerate(A_GROUPS):
        o_g, l_g = band_attention(proj, g, win, dil, B, S)
        cache = caches_a[g]
        o_s, l_s, state = a_step(proj, g, win, dil, cache[0].reshape(Bd, cache.shape[2], 2 * hw), Tp, Bd, Sd)
        a_sample.append(state.reshape(cache.shape))
        keep = min(win, S)
        c0 = g * 3 * hw + hw
        kv_cols = jnp.stack([proj[(b + 1) * S - keep:(b + 1) * S, c0:c0 + 2 * hw] for b in range(B)])
        a_prompt.append(kv_cols.reshape(1, B, keep, 2, A_HEADS, HEAD_DIM))
        op.append(o_g)
        lp.append(l_g)
        os_.append(o_s)
        ls_.append(l_s)
    x = merge_matmul_residual(op + lp, os_ + ls_, w_o_a[0].astype(BF16), x, tm=tm_rows)
    x = peer_layer(x, g_ffn[0], w_peer_q[0], peer_subkeys[0], peer_u[0], peer_v[0],
                   tm_route=tm_peer, tm_exp=tm_exp, chunk=1024)

    kv, kvb = rms_matmul(x, g_kv, w_kv_b.astype(BF16), tm=tm_big, tn=KV_W, epilogue="rope",
                         flags=jnp.asarray([0, 0, 1, 0, 1, 0], jnp.int32), tables=tab256, emit_bf16=True)
    bkv_prompt = kv[:Tp, :4 * KV_W].reshape(B, S, 4, B_KV_HEADS, HEAD_DIM)
    bkv_sample = kv[Tp:, :4 * KV_W].reshape(Bd, Sd, 4, B_KV_HEADS, HEAD_DIM)
    keep_w = min(B_WINDOW, S)
    bwin_prompt = jnp.stack([kv[(b + 1) * S - keep_w:(b + 1) * S, 4 * KV_W:] for b in range(B)])
    bwin_prompt = bwin_prompt.reshape(B, keep_w, 2, B_KV_HEADS, HEAD_DIM)
    cache2d = cache_b_kv.reshape(n_phys, page, 4 * KV_W)
    cw_p = _compress_weights(w_cmp1, w_cmp2, pe_cmp, S // CMP_STRIDE)
    cw_s = cw_p if past == S else _compress_weights(w_cmp1, w_cmp2, pe_cmp, past // CMP_STRIDE)
    kc_p, vc_p = compress_prompt(kv, cw_p, B, S)
    kc_s, vc_s = compress_sample(cache2d, page_table, cw_s, pages_per_step=pages_per_step)

    nq = B_HEADS * HEAD_DIM
    w_qg = w_qg_b[0]
    q = rms_matmul(x, g_mix[1], w_qg[:, :nq].astype(BF16), tm=tm_big, tn=512, epilogue="rope",
                   flags=jnp.ones((nq // 512,), jnp.int32), tables=tab512)
    ngate = 3 * B_HEADS
    w_gate = jnp.pad(w_qg[:, nq:], ((0, 0), (0, LANES - ngate))).astype(BF16)
    b_gate = jnp.pad(b_gate_b[0], (0, LANES - ngate))
    gate = rms_matmul(x, g_mix[1], w_gate, tm=tm_big, tn=LANES, epilogue="sigmoid", bias=b_gate)

    n_cmp_p = S // CMP_STRIDE - 1
    wsel_p = _overlap_weights(n_cmp_p, S // CMP_STRIDE, S // SEL_BLOCK, _pad_to(S // SEL_BLOCK, LANES))
    o_b_p = nsa_prompt(q, gate, kc_p, vc_p, kvb, wsel_p, B, S, kt_size=min(1024, S))
    n_cmp_s = past // CMP_STRIDE - 1
    n_sb_s = -(-(past + Sd) // SEL_BLOCK)
    wsel_s = _overlap_weights(n_cmp_s, past // CMP_STRIDE, n_sb_s, _pad_to(n_sb_s, LANES))
    Wb = cache_b_win.shape[1]
    o_b_s, bwin_sample = nsa_step(q, gate, kc_s, vc_s, kv, cache_b_win.reshape(Bd, Wb, 2 * KV_W), cache2d,
                                  page_table, wsel_s, Tp, Bd, Sd, pages_per_step=pages_per_step)
    x = matmul_residual(o_b_p, o_b_s, w_o_b[0].astype(BF16), x, tm=tm_rows)
    x = peer_layer(x, g_ffn[1], w_peer_q[1], peer_subkeys[1], peer_u[1], peer_v[1],
                   tm_route=tm_peer, tm_exp=tm_exp, chunk=1024)
    y = rmsnorm_rows(x, g_final, tm=tm_mid)

    return (y[:Tp].reshape(B, S, D), y[Tp:].reshape(Bd, Sd, D),
            a_prompt[0], a_sample[0], a_prompt[1], a_sample[1], a_prompt[2], a_sample[2],
            bkv_prompt, bkv_sample, bwin_prompt, bwin_sample.reshape(cache_b_win.shape))
```

```python
import functools
import math

import jax
import jax.numpy as jnp
from jax import lax
from jax.experimental import pallas as pl
from jax.experimental.pallas import tpu as pltpu

F32 = jnp.float32
BF16 = jnp.bfloat16

HEAD_DIM = 64
ROT_DIM = HEAD_DIM // 4
ROPE_THETA = 500000.0
NORM_EPS = 1e-6
A_GROUPS = ((128, 1), (512, 4), (2048, 16))
A_HEADS = 8
A_BLOCK = 128
B_HEADS = 16
B_KV_HEADS = 4
B_GROUP = B_HEADS // B_KV_HEADS
CMP_LEN = 32
CMP_STRIDE = 16
SEL_BLOCK = 64
N_SEL = 16
SEL_OVERLAP_W = (1.0, 2.0, 2.0, 2.0, 1.0)
B_WINDOW = 512
Q_BLOCK = 128
PEER_HEADS = 8
PEER_KEYS = 128
PEER_QDIM = 256
PEER_TOPK = 16

LANES = 128
NEG = -1e30
VMEM_LIMIT = 50 * 1024 * 1024
G_PITCH = 72
KV_W = B_KV_HEADS * HEAD_DIM
SCALE = HEAD_DIM ** -0.5
LOG2E = 1.4426950408889634
LN2 = 0.6931471805599453

_NT = (((1,), (1,)), ((), ()))


def _cparams(sem):
    return pltpu.CompilerParams(dimension_semantics=sem, vmem_limit_bytes=VMEM_LIMIT)


def _split2(a):
    hi = a.astype(BF16)
    lo = (a - hi.astype(F32)).astype(BF16)
    return hi, lo


def _dot(a, b):
    return jnp.dot(a, b, preferred_element_type=F32)


def _dot_nt(a, b):
    return lax.dot_general(a, b, _NT, preferred_element_type=F32)


def _dot_nt3(a, b):
    ah, al = _split2(a)
    bh, bl = _split2(b)
    return _dot_nt(ah, bh) + _dot_nt(ah, bl) + _dot_nt(al, bh)


def _softmax_rows(s, valid):
    s = jnp.where(valid, s, NEG)
    m = jnp.max(s, axis=-1, keepdims=True)
    e = jnp.where(valid, jnp.exp(s - m), 0.0)
    den = jnp.sum(e, axis=-1, keepdims=True)
    return e / jnp.where(den > 0, den, 1.0)


def _rope_lanes(y, cos, sa, sb):
    n = y.shape[-1]
    return y * cos + pltpu.roll(y, 8, 1) * sb + pltpu.roll(y, n - 8, 1) * sa


def _rms_norm_rows(x, g):
    ms = jnp.mean(x * x, axis=-1, keepdims=True)
    return x * lax.rsqrt(ms + NORM_EPS) * g


def _rms_mm_kernel(flags_ref, x_ref, g_ref, w_ref, *rest, epilogue, emit_bf16):
    n_out = 2 if emit_bf16 else 1
    extra, outs, xn_ref = rest[:len(rest) - n_out - 1], rest[len(rest) - n_out - 1:-1], rest[-1]
    j = pl.program_id(1)

    @pl.when(j == 0)
    def _():
        xn_ref[...] = _rms_norm_rows(x_ref[...], g_ref[...]).astype(BF16)

    def store(val):
        outs[0][...] = val
        if emit_bf16:
            outs[1][...] = val.astype(BF16)

    y = _dot(xn_ref[...], w_ref[...])
    if epilogue == "rope":
        cos_ref, sa_ref, sb_ref = extra
        f = flags_ref[j]

        @pl.when(f == 0)
        def _():
            store(y)

        @pl.when(f != 0)
        def _():
            store(_rope_lanes(y, cos_ref[...], sa_ref[...], sb_ref[...]))
    elif epilogue == "sigmoid":
        store(jax.nn.sigmoid(y + extra[0][...]))
    else:
        store(y)


def rms_matmul(x, g, w, *, tm, tn, epilogue="none", flags=None, tables=None, bias=None, emit_bf16=False):
    M, D = x.shape
    N = w.shape[1]
    assert M % tm == 0 and N % tn == 0
    nj = N // tn
    if flags is None:
        flags = jnp.zeros((nj,), jnp.int32)
    in_specs = [
        pl.BlockSpec((tm, D), lambda i, j, f: (i, 0)),
        pl.BlockSpec((1, D), lambda i, j, f: (0, 0)),
        pl.BlockSpec((D, tn), lambda i, j, f: (0, j)),
    ]
    args = [x, g.reshape(1, D), w]
    if epilogue == "rope":
        for t in tables:
            in_specs.append(pl.BlockSpec((tm, tn), lambda i, j, f: (i, 0)))
            args.append(t)
    elif epilogue == "sigmoid":
        in_specs.append(pl.BlockSpec((1, tn), lambda i, j, f: (0, j)))
        args.append(bias.reshape(1, N))
    out_spec = pl.BlockSpec((tm, tn), lambda i, j, f: (i, j))
    out_shape = [jax.ShapeDtypeStruct((M, N), F32)]
    if emit_bf16:
        out_shape.append(jax.ShapeDtypeStruct((M, N), BF16))
    res = pl.pallas_call(
        functools.partial(_rms_mm_kernel, epilogue=epilogue, emit_bf16=emit_bf16),
        out_shape=out_shape,
        grid_spec=pltpu.PrefetchScalarGridSpec(
            num_scalar_prefetch=1,
            grid=(M // tm, nj),
            in_specs=in_specs,
            out_specs=[out_spec] * len(out_shape),
            scratch_shapes=[pltpu.VMEM((tm, D), BF16)],
        ),
        compiler_params=_cparams(("parallel", "arbitrary")),
        name="rms_matmul_" + epilogue,
    )(flags, *args)
    return res if emit_bf16 else res[0]


def _two_source_specs(n_arr, tm, K, n_p):
    p_spec = pl.BlockSpec((tm, K), lambda i: (jnp.minimum(i, n_p - 1), 0))
    s_spec = pl.BlockSpec((tm, K), lambda i: (jnp.maximum(i - n_p, 0), 0))
    return [p_spec] * n_arr + [s_spec] * n_arr


def _mm_res_kernel(ap_ref, as_ref, w_ref, r_ref, o_ref, *, n_p):
    a = jnp.where(pl.program_id(0) >= n_p, as_ref[...], ap_ref[...])
    o_ref[...] = r_ref[...] + _dot(a.astype(BF16), w_ref[...])


def matmul_residual(a_p, a_s, w, res, *, tm):
    M, N = res.shape
    K = a_p.shape[1]
    n_p = a_p.shape[0] // tm
    assert a_p.shape[0] % tm == 0 and a_s.shape[0] % tm == 0
    return pl.pallas_call(
        functools.partial(_mm_res_kernel, n_p=n_p),
        out_shape=jax.ShapeDtypeStruct((M, N), F32),
        grid=(M // tm,),
        in_specs=_two_source_specs(1, tm, K, n_p) + [pl.BlockSpec((K, N), lambda i: (0, 0)),
                                                    pl.BlockSpec((tm, N), lambda i: (i, 0))],
        out_specs=pl.BlockSpec((tm, N), lambda i: (i, 0)),
        compiler_params=_cparams(("parallel",)),
        name="matmul_residual",
    )(a_p, a_s, w, res)


def _merge_mm_res_kernel(*refs, n_p):
    prompt, sample, (w_ref, r_ref, out_ref) = refs[:6], refs[6:12], refs[12:]
    is_s = pl.program_id(0) >= n_p
    o1, o2, o3, a1, a2, a3 = [jnp.where(is_s, s[...], p[...]) for p, s in zip(prompt, sample)]
    m = jnp.maximum(jnp.maximum(a1, a2), a3)
    e1, e2, e3 = jnp.exp(a1 - m), jnp.exp(a2 - m), jnp.exp(a3 - m)
    den = e1 + e2 + e3
    a = (e1 / den) * o1 + (e2 / den) * o2 + (e3 / den) * o3
    out_ref[...] = r_ref[...] + _dot(a.astype(BF16), w_ref[...])


def merge_matmul_residual(ol_p, ol_s, w, res, *, tm):
    M, N = res.shape
    K = ol_p[0].shape[1]
    n_p = ol_p[0].shape[0] // tm
    assert ol_p[0].shape[0] % tm == 0 and ol_s[0].shape[0] % tm == 0
    return pl.pallas_call(
        functools.partial(_merge_mm_res_kernel, n_p=n_p),
        out_shape=jax.ShapeDtypeStruct((M, N), F32),
        grid=(M // tm,),
        in_specs=_two_source_specs(6, tm, K, n_p) + [pl.BlockSpec((K, N), lambda i: (0, 0)),
                                                    pl.BlockSpec((tm, N), lambda i: (i, 0))],
        out_specs=pl.BlockSpec((tm, N), lambda i: (i, 0)),
        compiler_params=_cparams(("parallel",)),
        name="merge_matmul_residual",
    )(*ol_p, *ol_s, w, res)


def _rmsnorm_kernel(x_ref, g_ref, o_ref):
    o_ref[...] = _rms_norm_rows(x_ref[...], g_ref[...])


def rmsnorm_rows(x, g, *, tm):
    M, D = x.shape
    return pl.pallas_call(
        _rmsnorm_kernel,
        out_shape=jax.ShapeDtypeStruct((M, D), F32),
        grid=(M // tm,),
        in_specs=[pl.BlockSpec((tm, D), lambda i: (i, 0)), pl.BlockSpec((1, D), lambda i: (0, 0))],
        out_specs=pl.BlockSpec((tm, D), lambda i: (i, 0)),
        compiler_params=_cparams(("parallel",)),
        name="rmsnorm",
    )(x, g.reshape(1, D))


def _online_update(carry, s, mask, v):
    m_i, l_i, acc = carry
    s = jnp.where(mask, s, NEG)
    m_new = jnp.maximum(m_i, jnp.max(s, axis=-1, keepdims=True))
    alpha = jnp.exp(m_i - m_new)
    p = jnp.where(mask, jnp.exp(s - m_new), 0.0)
    return (m_new, alpha * l_i + jnp.sum(p, axis=-1, keepdims=True), alpha * acc + _dot(p.astype(BF16), v))


def _online_init(rows, width):
    return (jnp.full((rows, 1), NEG, F32), jnp.zeros((rows, 1), F32), jnp.zeros((rows, width), F32))


def _attend_heads(q, segments, n_heads):
    nq = q.shape[0]
    outs, lses = [], []
    for h in range(n_heads):
        qh = q[:, h * HEAD_DIM:(h + 1) * HEAD_DIM].astype(BF16)
        carry = _online_init(nq, HEAD_DIM)
        for k_rows, v_rows, mask in segments:
            carry = _online_update(carry, _dot_nt(qh, k_rows(h)) * SCALE, mask, v_rows(h))
        m, den, acc = carry
        outs.append(acc / den)
        lses.append(jnp.broadcast_to(m + jnp.log(den), (nq, HEAD_DIM)))
    return jnp.concatenate(outs, axis=1), jnp.concatenate(lses, axis=1)


def _pad_rows(x, n):
    return jnp.concatenate([x, jnp.zeros((n - x.shape[0], x.shape[1]), x.dtype)], axis=0)


def _band_attn_kernel(q_ref, kp_ref, kc_ref, vp_ref, vc_ref, o_ref, l_ref, *, nw):
    n = pl.program_id(2)
    kp, kc, vp, vc = kp_ref[...], kc_ref[...], vp_ref[...], vc_ref[...]
    i = lax.broadcasted_iota(jnp.int32, (A_BLOCK, 2 * A_BLOCK), 0)
    j = lax.broadcasted_iota(jnp.int32, (A_BLOCK, 2 * A_BLOCK), 1)
    rel = A_BLOCK + i - j
    bias = jnp.where((rel >= 0) & (rel <= nw) & ((j >= A_BLOCK) | (n > 0)), 0.0, NEG)
    q = q_ref[...] * (SCALE * LOG2E)
    outs, lses = [], []
    for h in range(A_HEADS):
        hs = slice(h * HEAD_DIM, (h + 1) * HEAD_DIM)
        kk = jnp.concatenate([kp[:, hs], kc[:, hs]], axis=0).astype(BF16)
        vv = jnp.concatenate([vp[:, hs], vc[:, hs]], axis=0).astype(BF16)
        s = _dot_nt(q[:, hs].astype(BF16), kk) + bias
        m = jnp.max(s, axis=-1, keepdims=True)
        e = jnp.exp2(s - m)
        den = jnp.sum(e, axis=-1, keepdims=True)
        outs.append(_dot(e.astype(BF16), vv) / den)
        lses.append(jnp.broadcast_to((m + jnp.log2(den)) * LN2, (A_BLOCK, HEAD_DIM)))
    o_ref[...] = jnp.concatenate(outs, axis=1)
    l_ref[...] = jnp.concatenate(lses, axis=1)


def band_attention(proj, g, window, dil, B, S):
    T, ncol = proj.shape
    L = S // dil
    nb = L // A_BLOCK
    hw = A_HEADS * HEAD_DIM
    per_res = ncol // hw
    view = proj.reshape(T // dil, dil * ncol)

    def spec(c, prev):
        def imap(b, r, n):
            nn = jnp.maximum(n - 1, 0) if prev else n
            return (b * nb + nn, r * per_res + g * 3 + c)
        return pl.BlockSpec((A_BLOCK, hw), imap)

    out_spec = pl.BlockSpec((A_BLOCK, hw), lambda b, r, n: (b * nb + n, r))
    o, l = pl.pallas_call(
        functools.partial(_band_attn_kernel, nw=window // dil),
        out_shape=[jax.ShapeDtypeStruct((B * L, dil * hw), F32)] * 2,
        grid=(B, dil, nb),
        in_specs=[spec(0, False), spec(1, True), spec(1, False), spec(2, True), spec(2, False)],
        out_specs=[out_spec, out_spec],
        compiler_params=_cparams(("parallel", "parallel", "arbitrary")),
        name=f"band_attention_d{dil}",
    )(view, view, view, view, view)
    return o.reshape(B * S, hw), l.reshape(B * S, hw)


def _a_step_kernel(q_ref, k_ref, v_ref, c_ref, o_ref, l_ref, s_ref, *, win, dil):
    W = c_ref.shape[1]
    Sd = q_ref.shape[0]
    hw = A_HEADS * HEAD_DIM
    knew, vnew = k_ref[...], v_ref[...]
    s_ref[0, pl.ds(0, W - Sd), :] = c_ref[0, pl.ds(Sd, W - Sd), :]
    s_ref[0, pl.ds(W - Sd, Sd), pl.ds(0, hw)] = knew
    s_ref[0, pl.ds(W - Sd, Sd), pl.ds(hw, hw)] = vnew

    def on_stride(dist):
        return (dist >= 0) & (dist <= win) & (jnp.bitwise_and(dist, dil - 1) == 0)

    dist_buf = (W + lax.broadcasted_iota(jnp.int32, (Sd, W), 0)) - lax.broadcasted_iota(jnp.int32, (Sd, W), 1)
    ncol = lax.broadcasted_iota(jnp.int32, (Sd, LANES), 1)
    dist_new = lax.broadcasted_iota(jnp.int32, (Sd, LANES), 0) - ncol
    knew_p, vnew_p = _pad_rows(knew, LANES), _pad_rows(vnew, LANES)
    buf_rows = lambda off: lambda h: c_ref[0, :, pl.ds(off + h * HEAD_DIM, HEAD_DIM)].astype(BF16)
    new_rows = lambda x: lambda h: x[:, h * HEAD_DIM:(h + 1) * HEAD_DIM].astype(BF16)
    segments = [(buf_rows(0), buf_rows(hw), on_stride(dist_buf)),
                (new_rows(knew_p), new_rows(vnew_p), on_stride(dist_new) & (ncol < Sd))]
    o, l = _attend_heads(q_ref[...], segments, A_HEADS)
    o_ref[...] = o
    l_ref[...] = l


def a_step(proj, g, window, dil, cache, Tp, Bd, Sd):
    assert dil & (dil - 1) == 0
    W = cache.shape[1]
    hw = A_HEADS * HEAD_DIM
    r0 = Tp // Sd
    col = lambda c: pl.BlockSpec((Sd, hw), lambda b: (r0 + b, g * 3 + c))
    row_out = pl.BlockSpec((Sd, hw), lambda b: (b, 0))
    return pl.pallas_call(
        functools.partial(_a_step_kernel, win=window, dil=dil),
        out_shape=[jax.ShapeDtypeStruct((Bd * Sd, hw), F32), jax.ShapeDtypeStruct((Bd * Sd, hw), F32),
                   jax.ShapeDtypeStruct(cache.shape, F32)],
        grid=(Bd,),
        in_specs=[col(0), col(1), col(2), pl.BlockSpec((1, W, 2 * hw), lambda b: (b, 0, 0))],
        out_specs=[row_out, row_out, pl.BlockSpec((1, W, 2 * hw), lambda b: (b, 0, 0))],
        compiler_params=_cparams(("parallel",)),
        name=f"a_step_d{dil}",
    )(proj, proj, proj, cache)


def _topk_cols(s, k, payload=None):
    N = s.shape[0]
    iota = lax.broadcasted_iota(jnp.int32, s.shape, 0).astype(F32)
    vals, picks = [], []
    for _ in range(k):
        m = jnp.max(s, axis=0, keepdims=True)
        idx = jnp.min(jnp.where(s == m, iota, float(N)), axis=0, keepdims=True)
        sel = iota == idx
        vals.append(m)
        if payload is None:
            picks.append(idx)
        else:
            picks.append(jnp.max(jnp.where(sel, payload, -1.0), axis=0, keepdims=True))
        s = jnp.where(sel, -jnp.inf, s)
    return vals, picks


def _topk_cols_many(arrs, k):
    N = arrs[0].shape[0]
    iota = lax.broadcasted_iota(jnp.int32, arrs[0].shape, 0).astype(F32)
    arrs = list(arrs)
    out = [([], []) for _ in arrs]
    for _ in range(k):
        for n, s in enumerate(arrs):
            m = jnp.max(s, axis=0, keepdims=True)
            idx = jnp.min(jnp.where(s == m, iota, float(N)), axis=0, keepdims=True)
            out[n][0].append(m)
            out[n][1].append(idx)
            arrs[n] = jnp.where(iota == idx, -jnp.inf, s)
    return out


def _topk_mask_cols(s, k):
    N = s.shape[0]
    iota = lax.broadcasted_iota(jnp.int32, s.shape, 0).astype(F32)
    chosen = jnp.zeros(s.shape, F32)
    for _ in range(k):
        m = jnp.max(s, axis=0, keepdims=True)
        idx = jnp.min(jnp.where(s == m, iota, float(N)), axis=0, keepdims=True)
        sel = iota == idx
        chosen = jnp.where(sel, 1.0, chosen)
        s = jnp.where(sel, -jnp.inf, s)
    return chosen


def _peer_route_kernel(x_ref, g_ref, wqh_ref, wql_ref, skh_ref, skl_ref,
                       hn_ref, e_ref, w_ref, xh_ref, xl_ref, et_ref, wt_ref, *, heads_per_step):
    h = pl.program_id(1)

    @pl.when(h == 0)
    def _():
        xn = _rms_norm_rows(x_ref[...], g_ref[...])
        hi, lo = _split2(xn)
        xh_ref[...] = hi
        xl_ref[...] = lo
        hn_ref[...] = hi

    xh, xl = xh_ref[...], xl_ref[...]
    K = PEER_TOPK
    b_iota = lax.broadcasted_iota(jnp.int32, (8, LANES), 0)
    jobs = []
    for hh in range(heads_per_step):
        cs = pl.ds(hh * PEER_QDIM, PEER_QDIM)
        wqh, wql = wqh_ref[:, cs], wql_ref[:, cs]
        q = _dot(xh, wqh) + _dot(xh, wql) + _dot(xl, wqh)
        sts = []
        for z in range(2):
            qh, ql = _split2(q[:, z * PEER_KEYS:(z + 1) * PEER_KEYS])
            skh, skl = skh_ref[z], skl_ref[z]
            sts.append(_dot_nt(skh, qh) + _dot_nt(skl, qh) + _dot_nt(skh, ql))
        row = pl.multiple_of((h * heads_per_step + hh) * K, K)
        jobs += [(sts, row, c0) for c0 in range(0, xh.shape[0], LANES)]
    for sts, row, c0 in jobs:
        (v1, i1), (v2, i2) = _topk_cols_many([st[:, c0:c0 + LANES] for st in sts], K)
        v2_lo, v2_hi = jnp.concatenate(v2[:8], axis=0), jnp.concatenate(v2[8:], axis=0)
        i2_lo, i2_hi = jnp.concatenate(i2[:8], axis=0), jnp.concatenate(i2[8:], axis=0)
        e_of = lambda a, i2rows: i1[a] * float(PEER_KEYS) + i2rows
        cand = [v1[0] + v2_lo, v1[0] + v2_hi, v1[1] + v2_lo]
        cand_e = [e_of(0, i2_lo), e_of(0, i2_hi), e_of(1, i2_lo)]
        for a in range(2, 8):
            cand.append(jnp.where(b_iota < K // (a + 1), v1[a] + v2_lo, -jnp.inf))
            cand_e.append(e_of(a, i2_lo))
        cand.append(jnp.concatenate(v1[8:], axis=0) + v2[0])
        cand_e.append(jnp.concatenate(i1[8:], axis=0) * float(PEER_KEYS) + i2[0])
        best, best_e = _topk_cols(jnp.concatenate(cand, axis=0), K, payload=jnp.concatenate(cand_e, axis=0))
        bs = jnp.concatenate(best, axis=0)
        ex = jnp.exp(bs - best[0])
        et_ref[pl.ds(row, K), pl.ds(c0, LANES)] = jnp.concatenate(best_e, axis=0)
        wt_ref[pl.ds(row, K), pl.ds(c0, LANES)] = ex / jnp.sum(ex, axis=0, keepdims=True)

    @pl.when(h == pl.num_programs(1) - 1)
    def _():
        e_ref[...] = et_ref[...].T.astype(jnp.int32)
        w_ref[...] = wt_ref[...].T


def peer_route(x, g, wq_hi, wq_lo, sk_hi, sk_lo, *, tm, heads_per_step=8):
    T, D = x.shape
    assert T % tm == 0 and tm % LANES == 0
    nslot = PEER_HEADS * PEER_TOPK
    wq_spec = pl.BlockSpec((D, heads_per_step * PEER_QDIM), lambda i, h: (0, h))
    return pl.pallas_call(
        functools.partial(_peer_route_kernel, heads_per_step=heads_per_step),
        out_shape=[jax.ShapeDtypeStruct((T, D), BF16),
                   jax.ShapeDtypeStruct((T, nslot), jnp.int32),
                   jax.ShapeDtypeStruct((T, nslot), F32)],
        grid=(T // tm, PEER_HEADS // heads_per_step),
        in_specs=[pl.BlockSpec((tm, D), lambda i, h: (i, 0)),
                  pl.BlockSpec((1, D), lambda i, h: (0, 0)),
                  wq_spec, wq_spec,
                  pl.BlockSpec((2, PEER_KEYS, PEER_QDIM // 2), lambda i, h: (0, 0, 0)),
                  pl.BlockSpec((2, PEER_KEYS, PEER_QDIM // 2), lambda i, h: (0, 0, 0))],
        out_specs=[pl.BlockSpec((tm, D), lambda i, h: (i, 0)),
                   pl.BlockSpec((tm, nslot), lambda i, h: (i, 0)),
                   pl.BlockSpec((tm, nslot), lambda i, h: (i, 0))],
        scratch_shapes=[pltpu.VMEM((tm, D), BF16), pltpu.VMEM((tm, D), BF16),
                        pltpu.VMEM((nslot, tm), F32), pltpu.VMEM((nslot, tm), F32)],
        compiler_params=_cparams(("parallel", "arbitrary")),
        name="peer_route",
    )(x, g.reshape(1, D), wq_hi, wq_lo, sk_hi, sk_lo)


HI16 = -65536


def _peer_expert_kernel(hn_ref, e_ref, w_ref, x_ref, ua_ref, ub_ref, va_ref, vb_ref, o_ref, gs_ref, acc_ref,
                        *, tm, rows_per_chunk):
    c = pl.program_id(1)
    nk = PEER_KEYS
    half = nk // 2

    @pl.when(c == 0)
    def _():
        acc_ref[...] = jnp.zeros_like(acc_ref)
        sub = lax.broadcasted_iota(jnp.int32, (nk, nk), 0)

        def body(t8, carry):
            r0 = pl.multiple_of(t8 * 8, 8)
            e8 = e_ref[pl.ds(r0, 8), :]
            w8 = w_ref[pl.ds(r0, 8), :]
            for r in range(8):
                e, w = e8[r:r + 1], w8[r:r + 1]
                i1 = jnp.right_shift(e, 7)
                i2 = jnp.bitwise_and(e, nk - 1)
                pt = jnp.where(sub == i1, w, 0.0).astype(BF16)
                qt = jnp.where(sub == i2, 1.0, 0.0).astype(BF16)
                gt = _dot_nt(pt, qt).astype(BF16).astype(F32)
                bits = lax.bitcast_convert_type(gt, jnp.int32)
                word = jnp.bitwise_or(lax.shift_right_logical(bits[:half], 16), bits[half:])
                gs_ref[pl.ds(pl.multiple_of((r0 + r) * G_PITCH, 8), half), :] = word
            return carry

        lax.fori_loop(0, tm // 8, body, 0)

    hn = hn_ref[...]
    act_a = jax.nn.gelu(_dot_nt(hn, ua_ref[...]))
    act_b = jax.nn.gelu(_dot_nt(hn, ub_ref[...]))
    pa, pb = [], []
    for r in range(rows_per_chunk):
        word = gs_ref[pl.ds(c * rows_per_chunk + r, tm, stride=G_PITCH), :]
        g_lo = lax.bitcast_convert_type(jnp.left_shift(word, 16), F32)
        g_hi = lax.bitcast_convert_type(jnp.bitwise_and(word, HI16), F32)
        pa.append((act_a[:, r * nk:(r + 1) * nk] * g_lo).astype(BF16))
        pb.append((act_b[:, r * nk:(r + 1) * nk] * g_hi).astype(BF16))
    acc_ref[...] += (_dot(jnp.concatenate(pa, axis=1), va_ref[...])
                     + _dot(jnp.concatenate(pb, axis=1), vb_ref[...]))

    @pl.when(c == pl.num_programs(1) - 1)
    def _():
        o_ref[...] = x_ref[...] + acc_ref[...]


def peer_experts(hn, e, w, x, u, v, *, tm, chunk):
    T, D = x.shape
    n_exp = v.shape[0]
    nslot = e.shape[1]
    blk = chunk // 2
    rows_per_chunk = blk // PEER_KEYS
    n_chunks = n_exp // chunk
    lo = pl.BlockSpec((blk, D), lambda i, c: (c, 0))
    hi = pl.BlockSpec((blk, D), lambda i, c: (c + n_chunks, 0))
    return pl.pallas_call(
        functools.partial(_peer_expert_kernel, tm=tm, rows_per_chunk=rows_per_chunk),
        out_shape=jax.ShapeDtypeStruct((T, D), F32),
        grid=(T // tm, n_chunks),
        in_specs=[pl.BlockSpec((tm, D), lambda i, c: (i, 0)),
                  pl.BlockSpec((tm, nslot), lambda i, c: (i, 0)),
                  pl.BlockSpec((tm, nslot), lambda i, c: (i, 0)),
                  pl.BlockSpec((tm, D), lambda i, c: (i, 0)),
                  lo, hi, lo, hi],
        out_specs=pl.BlockSpec((tm, D), lambda i, c: (i, 0)),
        scratch_shapes=[pltpu.VMEM((tm * G_PITCH, PEER_KEYS), jnp.int32), pltpu.VMEM((tm, D), F32)],
        compiler_params=_cparams(("parallel", "arbitrary")),
        name="peer_experts",
    )(hn, e, w, x, u, u, v, v)


def peer_layer(x, g, w_q, subkeys, u, v, *, tm_route, tm_exp, chunk):
    wq_hi, wq_lo = _split2(w_q)
    sk_hi, sk_lo = _split2(subkeys)
    hn, e, w = peer_route(x, g, wq_hi, wq_lo, sk_hi, sk_lo, tm=tm_route)
    return peer_experts(hn, e, w, x, u.astype(BF16), v.astype(BF16), tm=tm_exp, chunk=chunk)


def _compress_core(get_x, w1k_ref, w1v_ref, pe_ref, w1f_ref, w2k_ref, w2v_ref, cos_ref, sa_ref, sb_ref,
                   kc_ref, vc_ref, acc_ref, n_cmp):
    nch = acc_ref.shape[0]
    for l in range(CMP_STRIDE):
        x = get_x(l)
        yk = _dot(x[:, :KV_W].astype(BF16), w1k_ref[l])
        yv = _dot(x[:, KV_W:].astype(BF16), w1v_ref[l])
        if l == 0:
            acc_ref[:, pl.ds(0, 2 * KV_W)] = yk
            acc_ref[:, pl.ds(2 * KV_W, 2 * KV_W)] = yv
        else:
            acc_ref[:, pl.ds(0, 2 * KV_W)] += yk
            acc_ref[:, pl.ds(2 * KV_W, 2 * KV_W)] += yv
    keep = lax.broadcasted_iota(jnp.int32, (nch, KV_W), 0) < n_cmp
    for z, (w2_ref, out_ref) in enumerate(((w2k_ref, kc_ref), (w2v_ref, vc_ref))):
        ph, pl_ = _split2(pe_ref[z])
        wh, wl = _split2(w1f_ref[z])
        pe_t = _dot(ph, wh) + _dot(ph, wl) + _dot(pl_, wh)
        pe_t = jnp.concatenate([pe_t] * B_KV_HEADS, axis=1)
        lo = acc_ref[:, pl.ds(z * 2 * KV_W, KV_W)]
        hi = acc_ref[:, pl.ds(z * 2 * KV_W + KV_W, KV_W)]
        hid = jax.nn.gelu(lo + pltpu.roll(hi, nch - 1, 0) + pe_t)
        out = _dot(hid.astype(BF16), w2_ref[...])
        if z == 0:
            out = _rope_lanes(out, cos_ref[...], sa_ref[...], sb_ref[...])
        out_ref[0] = jnp.where(keep, out, 0.0)


def _compress_prompt_kernel(x0, x1, x2, x3, *rest, n_cmp):
    slabs = (x0, x1, x2, x3)
    nch = rest[-1].shape[0]

    def get_x(l):
        return jnp.concatenate([s[pl.ds(l, nch, stride=CMP_STRIDE), :] for s in slabs], axis=1)

    _compress_core(get_x, *rest, n_cmp=n_cmp)


def _compress_weights(w_cmp1, w_cmp2, pe_cmp, n_chunk):
    eye = jnp.eye(B_KV_HEADS, dtype=F32)

    def per_head(w):
        bd = jnp.einsum('hg,...df->...hdgf', eye, w)
        return bd.reshape(w.shape[:-2] + (KV_W, KV_W)).astype(BF16)

    w1 = per_head(w_cmp1.reshape(2, 2, CMP_STRIDE, HEAD_DIM, -1))
    w1 = jnp.concatenate([w1[:, 0], w1[:, 1]], axis=-1)
    w2 = per_head(w_cmp2)
    pe = pe_cmp.reshape(2, 1, CMP_LEN * HEAD_DIM)
    w1f = w_cmp1.reshape(2, CMP_LEN * HEAD_DIM, -1)
    end_pos = jnp.arange(n_chunk, dtype=jnp.int32) * CMP_STRIDE + (CMP_LEN - 1)
    tabs = _rope_tables(end_pos, KV_W)
    return (w1[0], w1[1], pe, w1f, w2[0], w2[1]) + tabs


def _const_specs(arrs):
    return [pl.BlockSpec(a.shape, (lambda *i, nd=a.ndim: (0,) * nd), pipeline_mode=pl.Buffered(1)) for a in arrs]


def compress_prompt(kv, cw, B, S):
    nch = S // CMP_STRIDE
    slab = lambda s: pl.BlockSpec((S, LANES), lambda b: (b, s), pipeline_mode=pl.Buffered(1))
    out = pl.BlockSpec((1, nch, KV_W), lambda b: (b, 0, 0))
    return pl.pallas_call(
        functools.partial(_compress_prompt_kernel, n_cmp=nch - 1),
        out_shape=[jax.ShapeDtypeStruct((B, nch, KV_W), F32)] * 2,
        grid=(B,),
        in_specs=[slab(s) for s in range(4)] + _const_specs(cw),
        out_specs=[out, out],
        scratch_shapes=[pltpu.VMEM((nch, 4 * KV_W), F32)],
        compiler_params=_cparams(("parallel",)),
        name="compress_prompt",
    )(kv, kv, kv, kv, *cw)


def _compress_sample_kernel(pt_ref, *refs, n_cmp, pages_per_step):
    n_pg = 4 * pages_per_step
    pages, rest, x_ref = refs[:n_pg], refs[n_pg:-1], refs[-1]
    j = pl.program_id(1)
    ch_per_page = pages[0].shape[1] // CMP_STRIDE
    for p in range(pages_per_step):
        r0 = pl.multiple_of((j * pages_per_step + p) * ch_per_page, ch_per_page)
        for s in range(4):
            for l in range(CMP_STRIDE):
                x_ref[l, pl.ds(r0, ch_per_page), pl.ds(s * LANES, LANES)] = (
                    pages[p * 4 + s][0, pl.ds(l, ch_per_page, stride=CMP_STRIDE), :])

    @pl.when(j == pl.num_programs(1) - 1)
    def _():
        _compress_core(lambda l: x_ref[l], *rest, n_cmp=n_cmp)


def compress_sample(cache, page_table, cw, *, pages_per_step):
    Bd, n_pages = page_table.shape
    page = cache.shape[1]
    nch = n_pages * page // CMP_STRIDE
    P = pages_per_step

    def pspec(p, s):
        return pl.BlockSpec((1, page, LANES), lambda b, j, pt: (pt[b, j * P + p], 0, s))

    out = pl.BlockSpec((1, nch, KV_W), lambda b, j, pt: (b, 0, 0))
    return pl.pallas_call(
        functools.partial(_compress_sample_kernel, n_cmp=nch - 1, pages_per_step=P),
        out_shape=[jax.ShapeDtypeStruct((Bd, nch, KV_W), F32)] * 2,
        grid_spec=pltpu.PrefetchScalarGridSpec(
            num_scalar_prefetch=1,
            grid=(Bd, n_pages // P),
            in_specs=[pspec(p, s) for p in range(P) for s in range(4)] + _const_specs(cw),
            out_specs=[out, out],
            scratch_shapes=[pltpu.VMEM((nch, 4 * KV_W), F32),
                            pltpu.VMEM((CMP_STRIDE, nch, 2 * KV_W), F32)],
        ),
        compiler_params=_cparams(("parallel", "arbitrary")),
        name="compress_sample",
    )(page_table, *([cache] * (4 * P)), *cw)


def _select_blocks(imp, wsel, blk, cur):
    i_hi = imp.astype(BF16)
    r1 = imp - i_hi.astype(F32)
    i_mid = r1.astype(BF16)
    i_lo = (r1 - i_mid.astype(F32)).astype(BF16)
    imp_sel = _dot(i_hi, wsel) + _dot(i_mid, wsel) + _dot(i_lo, wsel)
    forced = (blk == 0) | (blk == cur) | (blk == cur - 1)
    imp_sel = jnp.where(blk > cur, -jnp.inf, jnp.where(forced, jnp.inf, imp_sel))
    return _topk_mask_cols(imp_sel.T, N_SEL).T


def _tile_block_hits(chosen_q, k0, kt_size):
    n_sb = chosen_q.shape[1]
    blk = lax.broadcasted_iota(jnp.int32, (n_sb, kt_size), 0)
    key_blk = jnp.right_shift(k0 + lax.broadcasted_iota(jnp.int32, (n_sb, kt_size), 1), 6)
    expand = jnp.where(blk == key_blk, 1.0, 0.0).astype(BF16)
    return _dot(chosen_q, expand)


def _tile_block_mask(chosen_q, k0, kt_size):
    return _tile_block_hits(chosen_q, k0, kt_size) > 0.5


def _nsa_prompt_kernel(q_ref, gate_ref, kc_ref, vc_ref, ks_ref, vs_ref, kw_ref, vw_ref, wsel_ref,
                       o_ref, *, n_cmp, kt_size):
    n = pl.program_id(1)
    t0 = n * Q_BLOCK
    G, TQ = B_GROUP, Q_BLOCK
    R = G * TQ
    NC = kc_ref.shape[1]
    NSB = wsel_ref.shape[1]
    wsel = wsel_ref[...]
    gt = gate_ref[...]
    row_c = lax.broadcasted_iota(jnp.int32, (R, NC), 0)
    col_c = lax.broadcasted_iota(jnp.int32, (R, NC), 1)
    valid_c = (col_c * CMP_STRIDE + (CMP_LEN - 1) <= t0 + jnp.bitwise_and(row_c, TQ - 1)) & (col_c < n_cmp)
    bias_c = jnp.where(valid_c, 0.0, NEG)
    sees_cmp = t0 + jnp.bitwise_and(lax.broadcasted_iota(jnp.int32, (R, 1), 0), TQ - 1) >= CMP_LEN - 1
    blk = lax.broadcasted_iota(jnp.int32, (TQ, NSB), 1)
    cur = jnp.right_shift(t0 + lax.broadcasted_iota(jnp.int32, (TQ, NSB), 0), 6)
    WK = B_WINDOW + TQ
    w0 = pl.multiple_of(jnp.maximum(t0 - B_WINDOW, 0), TQ)
    wpos = w0 + lax.broadcasted_iota(jnp.int32, (R, WK), 1)
    rel = t0 + jnp.bitwise_and(lax.broadcasted_iota(jnp.int32, (R, WK), 0), TQ - 1) - wpos
    bias_w = jnp.where((rel >= 0) & (rel <= B_WINDOW), 0.0, NEG)
    n_kt = (t0 + TQ + kt_size - 1) // kt_size
    k_last = pl.multiple_of((n_kt - 1) * kt_size, kt_size)
    causal_last = jnp.where(k_last + lax.broadcasted_iota(jnp.int32, (TQ, kt_size), 1)
                            <= t0 + lax.broadcasted_iota(jnp.int32, (TQ, kt_size), 0), 0.0, NEG)
    ones_col = jnp.where(lax.broadcasted_iota(jnp.int32, (kt_size, HEAD_DIM), 1) == 0, 1.0, 0.0).astype(BF16)

    head_cols = [pl.ds(h * HEAD_DIM, HEAD_DIM) for h in range(B_KV_HEADS)]

    qbs, o_cs, chosens = [], [], []
    for h, hs in enumerate(head_cols):
        qs = jnp.concatenate([q_ref[:, pl.ds((h * G + g) * HEAD_DIM, HEAD_DIM)] for g in range(G)], axis=0)
        qs = qs * (SCALE * LOG2E)
        s_c = _dot_nt3(qs, kc_ref[0, :, hs]) + bias_c
        e_c = jnp.exp2(s_c - jnp.max(s_c, axis=-1, keepdims=True))
        p_c = e_c * jnp.where(sees_cmp, 1.0 / jnp.sum(e_c, axis=-1, keepdims=True), 0.0)
        imp = p_c[0:TQ]
        for g in range(1, G):
            imp = imp + p_c[g * TQ:(g + 1) * TQ]
        qbs.append(qs.astype(BF16))
        o_cs.append(_dot(p_c.astype(BF16), vc_ref[0, :, hs].astype(BF16)))
        chosens.append(_select_blocks(imp, wsel, blk, cur).astype(BF16))

    def sel_tile(h, k0, carry, extra_bias):
        m_i, acc = carry
        hs = head_cols[h]
        bias = (_tile_block_hits(chosens[h], k0, kt_size) - 1.0) * (-NEG)
        if extra_bias is not None:
            bias = bias + extra_bias
        s = _dot_nt(qbs[h], ks_ref[pl.ds(k0, kt_size), hs]).reshape(G, TQ, kt_size) + bias[None]
        s = s.reshape(R, kt_size)
        m_new = jnp.maximum(m_i, jnp.max(s, axis=-1, keepdims=True))
        p = jnp.exp2(s - m_new).astype(BF16)
        v1 = jnp.concatenate([vs_ref[pl.ds(k0, kt_size), hs], ones_col], axis=1)
        return m_new, jnp.exp2(m_i - m_new) * acc + _dot(p, v1)

    def sel_body(kt, carries):
        k0 = pl.multiple_of(kt * kt_size, kt_size)
        return tuple(sel_tile(h, k0, carries[h], None) for h in range(B_KV_HEADS))

    init = tuple((jnp.full((R, 1), NEG, F32), jnp.zeros((R, 2 * HEAD_DIM), F32)) for _ in range(B_KV_HEADS))
    carries = lax.fori_loop(0, n_kt - 1, sel_body, init)

    outs = []
    for h, hs in enumerate(head_cols):
        _, acc_s = sel_tile(h, k_last, carries[h], causal_last)
        o_s = acc_s[:, :HEAD_DIM] / acc_s[:, HEAD_DIM:HEAD_DIM + 1]
        s_w = _dot_nt(qbs[h], kw_ref[pl.ds(w0, WK), hs]) + bias_w
        e_w = jnp.exp2(s_w - jnp.max(s_w, axis=-1, keepdims=True))
        p_w = e_w * (1.0 / jnp.sum(e_w, axis=-1, keepdims=True))
        o_w = _dot(p_w.astype(BF16), vw_ref[pl.ds(w0, WK), hs])
        o_c = o_cs[h]
        for g in range(G):
            rs = slice(g * TQ, (g + 1) * TQ)
            c = 3 * (h * G + g)
            outs.append(gt[:, c:c + 1] * o_c[rs] + gt[:, c + 1:c + 2] * o_s[rs] + gt[:, c + 2:c + 3] * o_w[rs])
    o_ref[...] = jnp.concatenate(outs, axis=1)


def nsa_prompt(q, gate, kc, vc, kvb, wsel, B, S, *, kt_size):
    nq = q.shape[1]
    nqb = S // Q_BLOCK
    NC = kc.shape[1]
    comp = lambda c: pl.BlockSpec((S, KV_W), lambda b, n: (b, c))
    cmp_spec = pl.BlockSpec((1, NC, KV_W), lambda b, n: (b, 0, 0))
    return pl.pallas_call(
        functools.partial(_nsa_prompt_kernel, n_cmp=S // CMP_STRIDE - 1, kt_size=kt_size),
        out_shape=jax.ShapeDtypeStruct((B * S, nq), F32),
        grid=(B, nqb),
        in_specs=[pl.BlockSpec((Q_BLOCK, nq), lambda b, n: (b * nqb + n, 0)),
                  pl.BlockSpec((Q_BLOCK, LANES), lambda b, n: (b * nqb + n, 0)),
                  cmp_spec, cmp_spec, comp(2), comp(3), comp(4), comp(5),
                  pl.BlockSpec(wsel.shape, lambda b, n: (0, 0))],
        out_specs=pl.BlockSpec((Q_BLOCK, nq), lambda b, n: (b * nqb + n, 0)),
        compiler_params=_cparams(("parallel", "arbitrary")),
        name="nsa_prompt",
    )(q, gate, kc, vc, kvb, kvb, kvb, kvb, wsel)


def _nsa_step_kernel(pt_ref, q_ref, gate_ref, kc_ref, vc_ref, new_ref, win_ref, wsel_ref, *rest,
                     past, n_cmp, pages_per_step):
    P = pages_per_step
    pages = rest[:P]
    o_ref, st_ref, a_ref, ch_ref, m_ref, l_ref, acc_ref, oc_ref = rest[P:]
    del pt_ref
    j = pl.program_id(1)
    G, KVH = B_GROUP, B_KV_HEADS
    Sd = q_ref.shape[0]
    R = B_HEADS * Sd
    page = pages[0].shape[1]
    kt = P * page
    row_head = lambda shape: lax.broadcasted_iota(jnp.int32, shape, 0) // (G * Sd)
    own = lambda width: row_head((R, width)) == lax.broadcasted_iota(jnp.int32, (R, width), 1) // HEAD_DIM

    def diag(x):
        rows = G * Sd
        return jnp.concatenate([x[h * rows:(h + 1) * rows, h * HEAD_DIM:(h + 1) * HEAD_DIM]
                                for h in range(KVH)], axis=0)

    @pl.when(j == 0)
    def _():
        q = q_ref[...]
        a = jnp.concatenate([q[:, hd * HEAD_DIM:(hd + 1) * HEAD_DIM] for hd in range(B_HEADS)], axis=0)
        a = jnp.where(own(KV_W), jnp.concatenate([a] * KVH, axis=1), 0.0)
        a_ref[...] = a.astype(BF16)
        NC = kc_ref.shape[1]
        valid_c = lax.broadcasted_iota(jnp.int32, (R, NC), 1) < n_cmp
        p_c = _softmax_rows(_dot_nt3(a, kc_ref[0]) * SCALE, valid_c)
        oc_ref[...] = diag(_dot(p_c.astype(BF16), vc_ref[0].astype(BF16)))
        pooled = jnp.sum(p_c.reshape(KVH, G, Sd, NC), axis=1, keepdims=True)
        imp = jnp.broadcast_to(pooled, (KVH, G, Sd, NC)).reshape(R, NC)
        NSB = wsel_ref.shape[1]
        blk = lax.broadcasted_iota(jnp.int32, (R, NSB), 1)
        srow = jnp.bitwise_and(lax.broadcasted_iota(jnp.int32, (R, NSB), 0), Sd - 1)
        cur = jnp.right_shift(past + srow, 6)
        ch_ref[...] = _select_blocks(imp, wsel_ref[...], blk, cur).astype(BF16)
        m_ref[...] = jnp.full(m_ref.shape, NEG, F32)
        l_ref[...] = jnp.zeros(l_ref.shape, F32)
        acc_ref[...] = jnp.zeros(acc_ref.shape, F32)

    a = a_ref[...]
    ks = jnp.concatenate([pg[0, :, pl.ds(0, KV_W)] for pg in pages], axis=0).astype(BF16)
    vs = jnp.concatenate([pg[0, :, pl.ds(KV_W, KV_W)] for pg in pages], axis=0).astype(BF16)
    mk = _tile_block_mask(ch_ref[...], j * kt, kt)
    m_n, l_n, acc_n = _online_update((m_ref[...], l_ref[...], acc_ref[...]), _dot_nt(a, ks) * SCALE, mk, vs)
    m_ref[...] = m_n
    l_ref[...] = l_n
    acc_ref[...] = acc_n

    @pl.when(j == pl.num_programs(1) - 1)
    def _():
        new = _pad_rows(new_ref[...], LANES)
        comp = lambda c: new[:, c * KV_W:(c + 1) * KV_W].astype(BF16)
        srow = jnp.bitwise_and(lax.broadcasted_iota(jnp.int32, (R, LANES), 0), Sd - 1)
        scol = lax.broadcasted_iota(jnp.int32, (R, LANES), 1)
        causal_new = (scol <= srow) & (scol < Sd)
        nblk = past // SEL_BLOCK
        in_new = ch_ref[:, nblk:nblk + 1].astype(F32) > 0.5
        _, l_s, acc_s = _online_update((m_n, l_n, acc_n), _dot_nt(a, comp(2)) * SCALE, in_new & causal_new, comp(3))
        o_s = diag(acc_s) / jnp.where(l_s > 0, l_s, 1.0)

        W = win_ref.shape[1]
        wrow = jnp.bitwise_and(lax.broadcasted_iota(jnp.int32, (R, W), 0), Sd - 1)
        rel = W + wrow - lax.broadcasted_iota(jnp.int32, (R, W), 1)
        carry = _online_update(_online_init(R, KV_W), _dot_nt(a, win_ref[0, :, pl.ds(0, KV_W)].astype(BF16)) * SCALE,
                               (rel >= 0) & (rel <= B_WINDOW), win_ref[0, :, pl.ds(KV_W, KV_W)].astype(BF16))
        _, l_w, acc_w = _online_update(carry, _dot_nt(a, comp(4)) * SCALE, causal_new, comp(5))
        o_w = diag(acc_w) / jnp.where(l_w > 0, l_w, 1.0)
        o_c = oc_ref[...]
        gt = gate_ref[...]
        outs = []
        for hd in range(B_HEADS):
            rs = slice(hd * Sd, (hd + 1) * Sd)
            c = 3 * hd
            outs.append(gt[:, c:c + 1] * o_c[rs] + gt[:, c + 1:c + 2] * o_s[rs] + gt[:, c + 2:c + 3] * o_w[rs])
        o_ref[...] = jnp.concatenate(outs, axis=1)
        st_ref[0, pl.ds(0, W - Sd), :] = win_ref[0, pl.ds(Sd, W - Sd), :]
        st_ref[0, pl.ds(W - Sd, Sd), :] = new_ref[:, pl.ds(4 * KV_W, 2 * KV_W)]


def nsa_step(q, gate, kc, vc, kv, win, cache, page_table, wsel, Tp, Bd, Sd, *, pages_per_step):
    nq = q.shape[1]
    n_pages = page_table.shape[1]
    page = cache.shape[1]
    past = n_pages * page
    W = win.shape[1]
    NC = kc.shape[1]
    P = pages_per_step
    r0 = Tp // Sd
    R = B_HEADS * Sd
    rows = lambda w: pl.BlockSpec((Sd, w), lambda b, j, pt: (r0 + b, 0))
    per_b = lambda s1, s2: pl.BlockSpec((1, s1, s2), lambda b, j, pt: (b, 0, 0))
    pspec = lambda p: pl.BlockSpec((1, page, 2 * KV_W), lambda b, j, pt: (pt[b, j * P + p], 0, 1))
    return pl.pallas_call(
        functools.partial(_nsa_step_kernel, past=past, n_cmp=past // CMP_STRIDE - 1, pages_per_step=P),
        out_shape=[jax.ShapeDtypeStruct((Bd * Sd, nq), F32), jax.ShapeDtypeStruct(win.shape, F32)],
        grid_spec=pltpu.PrefetchScalarGridSpec(
            num_scalar_prefetch=1,
            grid=(Bd, n_pages // P),
            in_specs=[rows(nq), rows(LANES), per_b(NC, KV_W), per_b(NC, KV_W), rows(kv.shape[1]),
                      per_b(W, 2 * KV_W), pl.BlockSpec(wsel.shape, lambda b, j, pt: (0, 0))]
                     + [pspec(p) for p in range(P)],
            out_specs=[pl.BlockSpec((Sd, nq), lambda b, j, pt: (b, 0)), per_b(W, 2 * KV_W)],
            scratch_shapes=[pltpu.VMEM((R, KV_W), BF16), pltpu.VMEM((R, wsel.shape[1]), BF16),
                            pltpu.VMEM((R, 1), F32), pltpu.VMEM((R, 1), F32), pltpu.VMEM((R, KV_W), F32),
                            pltpu.VMEM((R, HEAD_DIM), F32)],
        ),
        compiler_params=_cparams(("parallel", "arbitrary")),
        name="nsa_step",
    )(page_table, q, gate, kc, vc, kv, win, wsel, *([cache] * P))


def _rope_tables(pos, width):
    half = ROT_DIM // 2
    inv = jnp.exp(-jnp.log(jnp.float32(ROPE_THETA)) * jnp.arange(half, dtype=F32) * (2.0 / ROT_DIM))
    ang = pos.astype(F32)[:, None] * inv[None, :]
    cos, sin = jnp.cos(ang), jnp.sin(ang)
    T = pos.shape[0]
    z8 = jnp.zeros((T, half), F32)
    rest = HEAD_DIM - ROT_DIM
    c = jnp.concatenate([cos, cos, jnp.ones((T, rest), F32)], axis=1)
    sa = jnp.concatenate([-sin, z8, jnp.zeros((T, rest), F32)], axis=1)
    sb = jnp.concatenate([z8, sin, jnp.zeros((T, rest), F32)], axis=1)
    reps = width // HEAD_DIM
    return tuple(jnp.tile(t, (1, reps)) for t in (c, sa, sb))


def _overlap_weights(n_cmp, n_cmp_pad, n_sb, n_sb_pad):
    j = jnp.arange(n_sb)
    cidx = (SEL_BLOCK // CMP_STRIDE) * j[:, None] - 1 + jnp.arange(len(SEL_OVERLAP_W))[None, :]
    wts = jnp.where((cidx >= 0) & (cidx < n_cmp), jnp.asarray(SEL_OVERLAP_W, F32), 0.0)
    onehot = (jnp.clip(cidx, 0, n_cmp - 1)[:, :, None] == jnp.arange(n_cmp_pad)[None, None, :]).astype(F32)
    wsel = jnp.einsum('jo,jon->nj', wts, onehot)
    return jnp.pad(wsel, ((0, 0), (0, n_sb_pad - n_sb))).astype(BF16)


def _pick(n, cands):
    for c in cands:
        if n % c == 0:
            return c
    return n


def _pad_to(n, m):
    return -(-n // m) * m


def kernel(x_prompt, x_sample, cache_a_g1, cache_a_g2, cache_a_g3, cache_b_kv, cache_b_win, page_table, g_mix, g_ffn, w_in_a, w_o_a, g_kv, w_kv_b, w_cmp1, w_cmp2, pe_cmp, w_qg_b, b_gate_b, w_o_b, w_peer_q, peer_subkeys, peer_u, peer_v, g_final):
    B, S, D = x_prompt.shape
    Bd, Sd, _ = x_sample.shape
    Tp, Ts = B * S, Bd * Sd
    T = Tp + Ts
    n_phys, page = cache_b_kv.shape[:2]
    n_pages = page_table.shape[1]
    past = n_pages * page
    hw = A_HEADS * HEAD_DIM
    tm_big = _pick(T, (1280, 640, 256, 128, 64, 32, 16, 8))
    tm_mid = _pick(T, (640, 256, 128, 64, 32, 16, 8))
    tm_peer = _pick(T, (256, 128))
    tm_exp = _pick(T, (520, 512, 256, 128, 64, 32, 16, 8))
    tm_rows = _pick(math.gcd(Tp, Ts), (256, 128, 64, 32, 16, 8))
    pages_per_step = _pick(n_pages, (16, 8, 4, 2, 1))

    x = jnp.concatenate([x_prompt.reshape(Tp, D), x_sample.reshape(Ts, D)], axis=0)
    pos_p = jnp.arange(S, dtype=jnp.int32)
    pos_s = past + jnp.arange(Sd, dtype=jnp.int32)
    pos = jnp.concatenate([jnp.tile(pos_p, B), jnp.tile(pos_s, Bd)])
    tab512 = _rope_tables(pos, 512)
    tab256 = tuple(t[:, :KV_W] for t in tab512)

    n_grp = len(A_GROUPS)
    proj = rms_matmul(x, g_mix[0], w_in_a[0].astype(BF16), tm=tm_big, tn=hw, epilogue="rope",
                      flags=jnp.asarray([1, 1, 0] * n_grp, jnp.int32), tables=tab512)
    caches_a = (cache_a_g1, cache_a_g2, cache_a_g3)
    op, lp, os_, ls_, a_prompt, a_sample = [], [], [], [], [], []
    for g, (win, dil) in enumerate(A_GROUPS):
        o_g, l_g = band_attention(proj, g, win, dil, B, S)
        cache = caches_a[g]
        o_s, l_s, state = a_step(proj, g, win, dil, cache[0].reshape(Bd, cache.shape[2], 2 * hw), Tp, Bd, Sd)
        a_sample.append(state.reshape(cache.shape))
        keep = min(win, S)
        c0 = g * 3 * hw + hw
        kv_cols = jnp.stack([proj[(b + 1) * S - keep:(b + 1) * S, c0:c0 + 2 * hw] for b in range(B)])
        a_prompt.append(kv_cols.reshape(1, B, keep, 2, A_HEADS, HEAD_DIM))
        op.append(o_g)
        lp.append(l_g)
        os_.append(o_s)
        ls_.append(l_s)
    x = merge_matmul_residual(op + lp, os_ + ls_, w_o_a[0].astype(BF16), x, tm=tm_rows)
    x = peer_layer(x, g_ffn[0], w_peer_q[0], peer_subkeys[0], peer_u[0], peer_v[0],
                   tm_route=tm_peer, tm_exp=tm_exp, chunk=1024)

    kv, kvb = rms_matmul(x, g_kv, w_kv_b.astype(BF16), tm=tm_big, tn=KV_W, epilogue="rope",
                         flags=jnp.asarray([0, 0, 1, 0, 1, 0], jnp.int32), tables=tab256, emit_bf16=True)
    bkv_prompt = kv[:Tp, :4 * KV_W].reshape(B, S, 4, B_KV_HEADS, HEAD_DIM)
    bkv_sample = kv[Tp:, :4 * KV_W].reshape(Bd, Sd, 4, B_KV_HEADS, HEAD_DIM)
    keep_w = min(B_WINDOW, S)
    bwin_prompt = jnp.stack([kv[(b + 1) * S - keep_w:(b + 1) * S, 4 * KV_W:] for b in range(B)])
    bwin_prompt = bwin_prompt.reshape(B, keep_w, 2, B_KV_HEADS, HEAD_DIM)
    cache2d = cache_b_kv.reshape(n_phys, page, 4 * KV_W)
    cw_p = _compress_weights(w_cmp1, w_cmp2, pe_cmp, S // CMP_STRIDE)
    cw_s = cw_p if past == S else _compress_weights(w_cmp1, w_cmp2, pe_cmp, past // CMP_STRIDE)
    kc_p, vc_p = compress_prompt(kv, cw_p, B, S)
    kc_s, vc_s = compress_sample(cache2d, page_table, cw_s, pages_per_step=pages_per_step)

    nq = B_HEADS * HEAD_DIM
    w_qg = w_qg_b[0]
    q = rms_matmul(x, g_mix[1], w_qg[:, :nq].astype(BF16), tm=tm_big, tn=512, epilogue="rope",
                   flags=jnp.ones((nq // 512,), jnp.int32), tables=tab512)
    ngate = 3 * B_HEADS
    w_gate = jnp.pad(w_qg[:, nq:], ((0, 0), (0, LANES - ngate))).astype(BF16)
    b_gate = jnp.pad(b_gate_b[0], (0, LANES - ngate))
    gate = rms_matmul(x, g_mix[1], w_gate, tm=tm_big, tn=LANES, epilogue="sigmoid", bias=b_gate)

    n_cmp_p = S // CMP_STRIDE - 1
    wsel_p = _overlap_weights(n_cmp_p, S // CMP_STRIDE, S // SEL_BLOCK, _pad_to(S // SEL_BLOCK, LANES))
    o_b_p = nsa_prompt(q, gate, kc_p, vc_p, kvb, wsel_p, B, S, kt_size=min(1024, S))
    n_cmp_s = past // CMP_STRIDE - 1
    n_sb_s = -(-(past + Sd) // SEL_BLOCK)
    wsel_s = _overlap_weights(n_cmp_s, past // CMP_STRIDE, n_sb_s, _pad_to(n_sb_s, LANES))
    Wb = cache_b_win.shape[1]
    o_b_s, bwin_sample = nsa_step(q, gate, kc_s, vc_s, kv, cache_b_win.reshape(Bd, Wb, 2 * KV_W), cache2d,
                                  page_table, wsel_s, Tp, Bd, Sd, pages_per_step=pages_per_step)
    x = matmul_residual(o_b_p, o_b_s, w_o_b[0].astype(BF16), x, tm=tm_rows)
    x = peer_layer(x, g_ffn[1], w_peer_q[1], peer_subkeys[1], peer_u[1], peer_v[1],
                   tm_route=tm_peer, tm_exp=tm_exp, chunk=1024)
    y = rmsnorm_rows(x, g_final, tm=tm_mid)

    return (y[:Tp].reshape(B, S, D), y[Tp:].reshape(Bd, Sd, D),
            a_prompt[0], a_sample[0], a_prompt[1], a_sample[1], a_prompt[2], a_sample[2],
            bkv_prompt, bkv_sample, bwin_prompt, bwin_sample.reshape(cache_b_win.shape))
```

```python
import functools
import math

import jax
import jax.numpy as jnp
from jax import lax
from jax.experimental import pallas as pl
from jax.experimental.pallas import tpu as pltpu

F32 = jnp.float32
BF16 = jnp.bfloat16

HEAD_DIM = 64
ROT_DIM = HEAD_DIM // 4
ROPE_THETA = 500000.0
NORM_EPS = 1e-6
A_GROUPS = ((128, 1), (512, 4), (2048, 16))
A_HEADS = 8
A_BLOCK = 128
B_HEADS = 16
B_KV_HEADS = 4
B_GROUP = B_HEADS // B_KV_HEADS
CMP_LEN = 32
CMP_STRIDE = 16
SEL_BLOCK = 64
N_SEL = 16
SEL_OVERLAP_W = (1.0, 2.0, 2.0, 2.0, 1.0)
B_WINDOW = 512
Q_BLOCK = 128
PEER_HEADS = 8
PEER_KEYS = 128
PEER_QDIM = 256
PEER_TOPK = 16

LANES = 128
NEG = -1e30
VMEM_LIMIT = 50 * 1024 * 1024
G_PITCH = 136
KV_W = B_KV_HEADS * HEAD_DIM
SCALE = HEAD_DIM ** -0.5
LOG2E = 1.4426950408889634
LN2 = 0.6931471805599453

_NT = (((1,), (1,)), ((), ()))


def _cparams(sem):
    return pltpu.CompilerParams(dimension_semantics=sem, vmem_limit_bytes=VMEM_LIMIT)


def _split2(a):
    hi = a.astype(BF16)
    lo = (a - hi.astype(F32)).astype(BF16)
    return hi, lo


def _dot(a, b):
    return jnp.dot(a, b, preferred_element_type=F32)


def _dot_nt(a, b):
    return lax.dot_general(a, b, _NT, preferred_element_type=F32)


def _dot_nt3(a, b):
    ah, al = _split2(a)
    bh, bl = _split2(b)
    return _dot_nt(ah, bh) + _dot_nt(ah, bl) + _dot_nt(al, bh)


def _softmax_rows(s, valid):
    s = jnp.where(valid, s, NEG)
    m = jnp.max(s, axis=-1, keepdims=True)
    e = jnp.where(valid, jnp.exp(s - m), 0.0)
    den = jnp.sum(e, axis=-1, keepdims=True)
    return e / jnp.where(den > 0, den, 1.0)


def _rope_lanes(y, cos, sa, sb):
    n = y.shape[-1]
    return y * cos + pltpu.roll(y, 8, 1) * sb + pltpu.roll(y, n - 8, 1) * sa


def _rms_norm_rows(x, g):
    ms = jnp.mean(x * x, axis=-1, keepdims=True)
    return x * lax.rsqrt(ms + NORM_EPS) * g


def _rms_mm_kernel(flags_ref, x_ref, g_ref, w_ref, *rest, epilogue, emit_bf16):
    n_out = 2 if emit_bf16 else 1
    extra, outs, xn_ref = rest[:len(rest) - n_out - 1], rest[len(rest) - n_out - 1:-1], rest[-1]
    j = pl.program_id(1)

    @pl.when(j == 0)
    def _():
        xn_ref[...] = _rms_norm_rows(x_ref[...], g_ref[...]).astype(BF16)

    def store(val):
        outs[0][...] = val
        if emit_bf16:
            outs[1][...] = val.astype(BF16)

    y = _dot(xn_ref[...], w_ref[...])
    if epilogue == "rope":
        cos_ref, sa_ref, sb_ref = extra
        f = flags_ref[j]

        @pl.when(f == 0)
        def _():
            store(y)

        @pl.when(f != 0)
        def _():
            store(_rope_lanes(y, cos_ref[...], sa_ref[...], sb_ref[...]))
    elif epilogue == "sigmoid":
        store(jax.nn.sigmoid(y + extra[0][...]))
    else:
        store(y)


def rms_matmul(x, g, w, *, tm, tn, epilogue="none", flags=None, tables=None, bias=None, emit_bf16=False):
    M, D = x.shape
    N = w.shape[1]
    assert M % tm == 0 and N % tn == 0
    nj = N // tn
    if flags is None:
        flags = jnp.zeros((nj,), jnp.int32)
    in_specs = [
        pl.BlockSpec((tm, D), lambda i, j, f: (i, 0)),
        pl.BlockSpec((1, D), lambda i, j, f: (0, 0)),
        pl.BlockSpec((D, tn), lambda i, j, f: (0, j)),
    ]
    args = [x, g.reshape(1, D), w]
    if epilogue == "rope":
        for t in tables:
            in_specs.append(pl.BlockSpec((tm, tn), lambda i, j, f: (i, 0)))
            args.append(t)
    elif epilogue == "sigmoid":
        in_specs.append(pl.BlockSpec((1, tn), lambda i, j, f: (0, j)))
        args.append(bias.reshape(1, N))
    out_spec = pl.BlockSpec((tm, tn), lambda i, j, f: (i, j))
    out_shape = [jax.ShapeDtypeStruct((M, N), F32)]
    if emit_bf16:
        out_shape.append(jax.ShapeDtypeStruct((M, N), BF16))
    res = pl.pallas_call(
        functools.partial(_rms_mm_kernel, epilogue=epilogue, emit_bf16=emit_bf16),
        out_shape=out_shape,
        grid_spec=pltpu.PrefetchScalarGridSpec(
            num_scalar_prefetch=1,
            grid=(M // tm, nj),
            in_specs=in_specs,
            out_specs=[out_spec] * len(out_shape),
            scratch_shapes=[pltpu.VMEM((tm, D), BF16)],
        ),
        compiler_params=_cparams(("parallel", "arbitrary")),
        name="rms_matmul_" + epilogue,
    )(flags, *args)
    return res if emit_bf16 else res[0]


def _two_source_specs(n_arr, tm, K, n_p):
    p_spec = pl.BlockSpec((tm, K), lambda i: (jnp.minimum(i, n_p - 1), 0))
    s_spec = pl.BlockSpec((tm, K), lambda i: (jnp.maximum(i - n_p, 0), 0))
    return [p_spec] * n_arr + [s_spec] * n_arr


def _mm_res_kernel(ap_ref, as_ref, w_ref, r_ref, o_ref, *, n_p):
    a = jnp.where(pl.program_id(0) >= n_p, as_ref[...], ap_ref[...])
    o_ref[...] = r_ref[...] + _dot(a.astype(BF16), w_ref[...])


def matmul_residual(a_p, a_s, w, res, *, tm):
    M, N = res.shape
    K = a_p.shape[1]
    n_p = a_p.shape[0] // tm
    assert a_p.shape[0] % tm == 0 and a_s.shape[0] % tm == 0
    return pl.pallas_call(
        functools.partial(_mm_res_kernel, n_p=n_p),
        out_shape=jax.ShapeDtypeStruct((M, N), F32),
        grid=(M // tm,),
        in_specs=_two_source_specs(1, tm, K, n_p) + [pl.BlockSpec((K, N), lambda i: (0, 0)),
                                                    pl.BlockSpec((tm, N), lambda i: (i, 0))],
        out_specs=pl.BlockSpec((tm, N), lambda i: (i, 0)),
        compiler_params=_cparams(("parallel",)),
        name="matmul_residual",
    )(a_p, a_s, w, res)


def _merge_mm_res_kernel(*refs, n_p):
    prompt, sample, (w_ref, r_ref, out_ref) = refs[:6], refs[6:12], refs[12:]
    is_s = pl.program_id(0) >= n_p
    o1, o2, o3, a1, a2, a3 = [jnp.where(is_s, s[...], p[...]) for p, s in zip(prompt, sample)]
    m = jnp.maximum(jnp.maximum(a1, a2), a3)
    e1, e2, e3 = jnp.exp(a1 - m), jnp.exp(a2 - m), jnp.exp(a3 - m)
    den = e1 + e2 + e3
    a = (e1 / den) * o1 + (e2 / den) * o2 + (e3 / den) * o3
    out_ref[...] = r_ref[...] + _dot(a.astype(BF16), w_ref[...])


def merge_matmul_residual(ol_p, ol_s, w, res, *, tm):
    M, N = res.shape
    K = ol_p[0].shape[1]
    n_p = ol_p[0].shape[0] // tm
    assert ol_p[0].shape[0] % tm == 0 and ol_s[0].shape[0] % tm == 0
    return pl.pallas_call(
        functools.partial(_merge_mm_res_kernel, n_p=n_p),
        out_shape=jax.ShapeDtypeStruct((M, N), F32),
        grid=(M // tm,),
        in_specs=_two_source_specs(6, tm, K, n_p) + [pl.BlockSpec((K, N), lambda i: (0, 0)),
                                                    pl.BlockSpec((tm, N), lambda i: (i, 0))],
        out_specs=pl.BlockSpec((tm, N), lambda i: (i, 0)),
        compiler_params=_cparams(("parallel",)),
        name="merge_matmul_residual",
    )(*ol_p, *ol_s, w, res)


def _rmsnorm_kernel(x_ref, g_ref, o_ref):
    o_ref[...] = _rms_norm_rows(x_ref[...], g_ref[...])


def rmsnorm_rows(x, g, *, tm):
    M, D = x.shape
    return pl.pallas_call(
        _rmsnorm_kernel,
        out_shape=jax.ShapeDtypeStruct((M, D), F32),
        grid=(M // tm,),
        in_specs=[pl.BlockSpec((tm, D), lambda i: (i, 0)), pl.BlockSpec((1, D), lambda i: (0, 0))],
        out_specs=pl.BlockSpec((tm, D), lambda i: (i, 0)),
        compiler_params=_cparams(("parallel",)),
        name="rmsnorm",
    )(x, g.reshape(1, D))


def _online_update(carry, s, mask, v):
    m_i, l_i, acc = carry
    s = jnp.where(mask, s, NEG)
    m_new = jnp.maximum(m_i, jnp.max(s, axis=-1, keepdims=True))
    alpha = jnp.exp(m_i - m_new)
    p = jnp.where(mask, jnp.exp(s - m_new), 0.0)
    return (m_new, alpha * l_i + jnp.sum(p, axis=-1, keepdims=True), alpha * acc + _dot(p.astype(BF16), v))


def _online_init(rows, width):
    return (jnp.full((rows, 1), NEG, F32), jnp.zeros((rows, 1), F32), jnp.zeros((rows, width), F32))


def _attend_heads(q, segments, n_heads):
    nq = q.shape[0]
    outs, lses = [], []
    for h in range(n_heads):
        qh = q[:, h * HEAD_DIM:(h + 1) * HEAD_DIM].astype(BF16)
        carry = _online_init(nq, HEAD_DIM)
        for k_rows, v_rows, mask in segments:
            carry = _online_update(carry, _dot_nt(qh, k_rows(h)) * SCALE, mask, v_rows(h))
        m, den, acc = carry
        outs.append(acc / den)
        lses.append(jnp.broadcast_to(m + jnp.log(den), (nq, HEAD_DIM)))
    return jnp.concatenate(outs, axis=1), jnp.concatenate(lses, axis=1)


def _pad_rows(x, n):
    return jnp.concatenate([x, jnp.zeros((n - x.shape[0], x.shape[1]), x.dtype)], axis=0)


def _band_attn_kernel(q_ref, kp_ref, kc_ref, vp_ref, vc_ref, o_ref, l_ref, *, nw):
    n = pl.program_id(2)
    kp, kc, vp, vc = kp_ref[...], kc_ref[...], vp_ref[...], vc_ref[...]
    i = lax.broadcasted_iota(jnp.int32, (A_BLOCK, 2 * A_BLOCK), 0)
    j = lax.broadcasted_iota(jnp.int32, (A_BLOCK, 2 * A_BLOCK), 1)
    rel = A_BLOCK + i - j
    bias = jnp.where((rel >= 0) & (rel <= nw) & ((j >= A_BLOCK) | (n > 0)), 0.0, NEG)
    q = q_ref[...] * (SCALE * LOG2E)
    outs, lses = [], []
    for h in range(A_HEADS):
        hs = slice(h * HEAD_DIM, (h + 1) * HEAD_DIM)
        kk = jnp.concatenate([kp[:, hs], kc[:, hs]], axis=0).astype(BF16)
        vv = jnp.concatenate([vp[:, hs], vc[:, hs]], axis=0).astype(BF16)
        s = _dot_nt(q[:, hs].astype(BF16), kk) + bias
        m = jnp.max(s, axis=-1, keepdims=True)
        e = jnp.exp2(s - m)
        den = jnp.sum(e, axis=-1, keepdims=True)
        outs.append(_dot(e.astype(BF16), vv) / den)
        lses.append(jnp.broadcast_to((m + jnp.log2(den)) * LN2, (A_BLOCK, HEAD_DIM)))
    o_ref[...] = jnp.concatenate(outs, axis=1)
    l_ref[...] = jnp.concatenate(lses, axis=1)


def band_attention(proj, g, window, dil, B, S):
    T, ncol = proj.shape
    L = S // dil
    nb = L // A_BLOCK
    hw = A_HEADS * HEAD_DIM
    per_res = ncol // hw
    view = proj.reshape(T // dil, dil * ncol)

    def spec(c, prev):
        def imap(b, r, n):
            nn = jnp.maximum(n - 1, 0) if prev else n
            return (b * nb + nn, r * per_res + g * 3 + c)
        return pl.BlockSpec((A_BLOCK, hw), imap)

    out_spec = pl.BlockSpec((A_BLOCK, hw), lambda b, r, n: (b * nb + n, r))
    o, l = pl.pallas_call(
        functools.partial(_band_attn_kernel, nw=window // dil),
        out_shape=[jax.ShapeDtypeStruct((B * L, dil * hw), F32)] * 2,
        grid=(B, dil, nb),
        in_specs=[spec(0, False), spec(1, True), spec(1, False), spec(2, True), spec(2, False)],
        out_specs=[out_spec, out_spec],
        compiler_params=_cparams(("parallel", "parallel", "arbitrary")),
        name=f"band_attention_d{dil}",
    )(view, view, view, view, view)
    return o.reshape(B * S, hw), l.reshape(B * S, hw)


def _a_step_kernel(q_ref, k_ref, v_ref, c_ref, o_ref, l_ref, s_ref, *, win, dil):
    W = c_ref.shape[1]
    Sd = q_ref.shape[0]
    hw = A_HEADS * HEAD_DIM
    knew, vnew = k_ref[...], v_ref[...]
    s_ref[0, pl.ds(0, W - Sd), :] = c_ref[0, pl.ds(Sd, W - Sd), :]
    s_ref[0, pl.ds(W - Sd, Sd), pl.ds(0, hw)] = knew
    s_ref[0, pl.ds(W - Sd, Sd), pl.ds(hw, hw)] = vnew

    def on_stride(dist):
        return (dist >= 0) & (dist <= win) & (jnp.bitwise_and(dist, dil - 1) == 0)

    dist_buf = (W + lax.broadcasted_iota(jnp.int32, (Sd, W), 0)) - lax.broadcasted_iota(jnp.int32, (Sd, W), 1)
    ncol = lax.broadcasted_iota(jnp.int32, (Sd, LANES), 1)
    dist_new = lax.broadcasted_iota(jnp.int32, (Sd, LANES), 0) - ncol
    knew_p, vnew_p = _pad_rows(knew, LANES), _pad_rows(vnew, LANES)
    buf_rows = lambda off: lambda h: c_ref[0, :, pl.ds(off + h * HEAD_DIM, HEAD_DIM)].astype(BF16)
    new_rows = lambda x: lambda h: x[:, h * HEAD_DIM:(h + 1) * HEAD_DIM].astype(BF16)
    segments = [(buf_rows(0), buf_rows(hw), on_stride(dist_buf)),
                (new_rows(knew_p), new_rows(vnew_p), on_stride(dist_new) & (ncol < Sd))]
    o, l = _attend_heads(q_ref[...], segments, A_HEADS)
    o_ref[...] = o
    l_ref[...] = l


def a_step(proj, g, window, dil, cache, Tp, Bd, Sd):
    assert dil & (dil - 1) == 0
    W = cache.shape[1]
    hw = A_HEADS * HEAD_DIM
    r0 = Tp // Sd
    col = lambda c: pl.BlockSpec((Sd, hw), lambda b: (r0 + b, g * 3 + c))
    row_out = pl.BlockSpec((Sd, hw), lambda b: (b, 0))
    return pl.pallas_call(
        functools.partial(_a_step_kernel, win=window, dil=dil),
        out_shape=[jax.ShapeDtypeStruct((Bd * Sd, hw), F32), jax.ShapeDtypeStruct((Bd * Sd, hw), F32),
                   jax.ShapeDtypeStruct(cache.shape, F32)],
        grid=(Bd,),
        in_specs=[col(0), col(1), col(2), pl.BlockSpec((1, W, 2 * hw), lambda b: (b, 0, 0))],
        out_specs=[row_out, row_out, pl.BlockSpec((1, W, 2 * hw), lambda b: (b, 0, 0))],
        compiler_params=_cparams(("parallel",)),
        name=f"a_step_d{dil}",
    )(proj, proj, proj, cache)


def _topk_cols(s, k, payload=None):
    N = s.shape[0]
    iota = lax.broadcasted_iota(jnp.int32, s.shape, 0).astype(F32)
    vals, picks = [], []
    for _ in range(k):
        m = jnp.max(s, axis=0, keepdims=True)
        idx = jnp.min(jnp.where(s == m, iota, float(N)), axis=0, keepdims=True)
        sel = iota == idx
        vals.append(m)
        if payload is None:
            picks.append(idx)
        else:
            picks.append(jnp.max(jnp.where(sel, payload, -1.0), axis=0, keepdims=True))
        s = jnp.where(sel, -jnp.inf, s)
    return vals, picks


def _topk_cols_many(arrs, k):
    N = arrs[0].shape[0]
    iota = lax.broadcasted_iota(jnp.int32, arrs[0].shape, 0).astype(F32)
    arrs = list(arrs)
    out = [([], []) for _ in arrs]
    for _ in range(k):
        for n, s in enumerate(arrs):
            m = jnp.max(s, axis=0, keepdims=True)
            idx = jnp.min(jnp.where(s == m, iota, float(N)), axis=0, keepdims=True)
            out[n][0].append(m)
            out[n][1].append(idx)
            arrs[n] = jnp.where(iota == idx, -jnp.inf, s)
    return out


def _topk_mask_cols(s, k):
    N = s.shape[0]
    iota = lax.broadcasted_iota(jnp.int32, s.shape, 0).astype(F32)
    chosen = jnp.zeros(s.shape, F32)
    for _ in range(k):
        m = jnp.max(s, axis=0, keepdims=True)
        idx = jnp.min(jnp.where(s == m, iota, float(N)), axis=0, keepdims=True)
        sel = iota == idx
        chosen = jnp.where(sel, 1.0, chosen)
        s = jnp.where(sel, -jnp.inf, s)
    return chosen


def _peer_route_kernel(x_ref, g_ref, wqh_ref, wql_ref, skh_ref, skl_ref,
                       hn_ref, e_ref, w_ref, xh_ref, xl_ref, et_ref, wt_ref, *, heads_per_step):
    h = pl.program_id(1)

    @pl.when(h == 0)
    def _():
        xn = _rms_norm_rows(x_ref[...], g_ref[...])
        hi, lo = _split2(xn)
        xh_ref[...] = hi
        xl_ref[...] = lo
        hn_ref[...] = hi

    xh, xl = xh_ref[...], xl_ref[...]
    K = PEER_TOPK
    b_iota = lax.broadcasted_iota(jnp.int32, (8, LANES), 0)
    jobs = []
    for hh in range(heads_per_step):
        cs = pl.ds(hh * PEER_QDIM, PEER_QDIM)
        wqh, wql = wqh_ref[:, cs], wql_ref[:, cs]
        q = _dot(xh, wqh) + _dot(xh, wql) + _dot(xl, wqh)
        sts = []
        for z in range(2):
            qh, ql = _split2(q[:, z * PEER_KEYS:(z + 1) * PEER_KEYS])
            skh, skl = skh_ref[z], skl_ref[z]
            sts.append(_dot_nt(skh, qh) + _dot_nt(skl, qh) + _dot_nt(skh, ql))
        row = pl.multiple_of((h * heads_per_step + hh) * K, K)
        jobs += [(sts, row, c0) for c0 in range(0, xh.shape[0], LANES)]
    for sts, row, c0 in jobs:
        (v1, i1), (v2, i2) = _topk_cols_many([st[:, c0:c0 + LANES] for st in sts], K)
        v2_lo, v2_hi = jnp.concatenate(v2[:8], axis=0), jnp.concatenate(v2[8:], axis=0)
        i2_lo, i2_hi = jnp.concatenate(i2[:8], axis=0), jnp.concatenate(i2[8:], axis=0)
        e_of = lambda a, i2rows: i1[a] * float(PEER_KEYS) + i2rows
        cand = [v1[0] + v2_lo, v1[0] + v2_hi, v1[1] + v2_lo]
        cand_e = [e_of(0, i2_lo), e_of(0, i2_hi), e_of(1, i2_lo)]
        for a in range(2, 8):
            cand.append(jnp.where(b_iota < K // (a + 1), v1[a] + v2_lo, -jnp.inf))
            cand_e.append(e_of(a, i2_lo))
        cand.append(jnp.concatenate(v1[8:], axis=0) + v2[0])
        cand_e.append(jnp.concatenate(i1[8:], axis=0) * float(PEER_KEYS) + i2[0])
        best, best_e = _topk_cols(jnp.concatenate(cand, axis=0), K, payload=jnp.concatenate(cand_e, axis=0))
        bs = jnp.concatenate(best, axis=0)
        ex = jnp.exp(bs - best[0])
        et_ref[pl.ds(row, K), pl.ds(c0, LANES)] = jnp.concatenate(best_e, axis=0)
        wt_ref[pl.ds(row, K), pl.ds(c0, LANES)] = ex / jnp.sum(ex, axis=0, keepdims=True)

    @pl.when(h == pl.num_programs(1) - 1)
    def _():
        e_ref[...] = et_ref[...].T.astype(jnp.int32)
        w_ref[...] = wt_ref[...].T


def peer_route(x, g, wq_hi, wq_lo, sk_hi, sk_lo, *, tm, heads_per_step=8):
    T, D = x.shape
    assert T % tm == 0 and tm % LANES == 0
    nslot = PEER_HEADS * PEER_TOPK
    wq_spec = pl.BlockSpec((D, heads_per_step * PEER_QDIM), lambda i, h: (0, h))
    return pl.pallas_call(
        functools.partial(_peer_route_kernel, heads_per_step=heads_per_step),
        out_shape=[jax.ShapeDtypeStruct((T, D), BF16),
                   jax.ShapeDtypeStruct((T, nslot), jnp.int32),
                   jax.ShapeDtypeStruct((T, nslot), F32)],
        grid=(T // tm, PEER_HEADS // heads_per_step),
        in_specs=[pl.BlockSpec((tm, D), lambda i, h: (i, 0)),
                  pl.BlockSpec((1, D), lambda i, h: (0, 0)),
                  wq_spec, wq_spec,
                  pl.BlockSpec((2, PEER_KEYS, PEER_QDIM // 2), lambda i, h: (0, 0, 0)),
                  pl.BlockSpec((2, PEER_KEYS, PEER_QDIM // 2), lambda i, h: (0, 0, 0))],
        out_specs=[pl.BlockSpec((tm, D), lambda i, h: (i, 0)),
                   pl.BlockSpec((tm, nslot), lambda i, h: (i, 0)),
                   pl.BlockSpec((tm, nslot), lambda i, h: (i, 0))],
        scratch_shapes=[pltpu.VMEM((tm, D), BF16), pltpu.VMEM((tm, D), BF16),
                        pltpu.VMEM((nslot, tm), F32), pltpu.VMEM((nslot, tm), F32)],
        compiler_params=_cparams(("parallel", "arbitrary")),
        name="peer_route",
    )(x, g.reshape(1, D), wq_hi, wq_lo, sk_hi, sk_lo)


def _peer_expert_kernel(hn_ref, e_ref, w_ref, x_ref, ua_ref, ub_ref, va_ref, vb_ref, o_ref, gs_ref, acc_ref,
                        *, tm, rows_per_chunk):
    c = pl.program_id(1)
    nk = PEER_KEYS
    half = nk // 2

    @pl.when(c == 0)
    def _():
        acc_ref[...] = jnp.zeros_like(acc_ref)
        sub = lax.broadcasted_iota(jnp.int32, (nk, nk), 0)

        def body(t8, carry):
            r0 = pl.multiple_of(t8 * 8, 8)
            e8 = e_ref[pl.ds(r0, 8), :]
            w8 = w_ref[pl.ds(r0, 8), :]
            for r in range(8):
                e, w = e8[r:r + 1], w8[r:r + 1]
                i1 = jnp.right_shift(e, 7)
                i2 = jnp.bitwise_and(e, nk - 1)
                pt = jnp.where(sub == i1, w, 0.0).astype(BF16)
                qt = jnp.where(sub == i2, 1.0, 0.0).astype(BF16)
                row = pl.multiple_of((r0 + r) * G_PITCH, 8)
                gs_ref[pl.ds(row, nk), :] = _dot_nt(pt, qt)
            return carry

        lax.fori_loop(0, tm // 8, body, 0)

    hn = hn_ref[...]
    act_a = jax.nn.gelu(_dot_nt(hn, ua_ref[...]))
    act_b = jax.nn.gelu(_dot_nt(hn, ub_ref[...]))
    pa, pb = [], []
    for r in range(rows_per_chunk):
        g_lo = gs_ref[pl.ds(c * rows_per_chunk + r, tm, stride=G_PITCH), :]
        g_hi = gs_ref[pl.ds(half + c * rows_per_chunk + r, tm, stride=G_PITCH), :]
        pa.append((act_a[:, r * nk:(r + 1) * nk] * g_lo).astype(BF16))
        pb.append((act_b[:, r * nk:(r + 1) * nk] * g_hi).astype(BF16))
    acc_ref[...] += (_dot(jnp.concatenate(pa, axis=1), va_ref[...])
                     + _dot(jnp.concatenate(pb, axis=1), vb_ref[...]))

    @pl.when(c == pl.num_programs(1) - 1)
    def _():
        o_ref[...] = x_ref[...] + acc_ref[...]


def peer_experts(hn, e, w, x, u, v, *, tm, chunk):
    T, D = x.shape
    n_exp = v.shape[0]
    nslot = e.shape[1]
    blk = chunk // 2
    rows_per_chunk = blk // PEER_KEYS
    n_chunks = n_exp // chunk
    lo = pl.BlockSpec((blk, D), lambda i, c: (c, 0))
    hi = pl.BlockSpec((blk, D), lambda i, c: (c + n_chunks, 0))
    return pl.pallas_call(
        functools.partial(_peer_expert_kernel, tm=tm, rows_per_chunk=rows_per_chunk),
        out_shape=jax.ShapeDtypeStruct((T, D), F32),
        grid=(T // tm, n_chunks),
        in_specs=[pl.BlockSpec((tm, D), lambda i, c: (i, 0)),
                  pl.BlockSpec((tm, nslot), lambda i, c: (i, 0)),
                  pl.BlockSpec((tm, nslot), lambda i, c: (i, 0)),
                  pl.BlockSpec((tm, D), lambda i, c: (i, 0)),
                  lo, hi, lo, hi],
        out_specs=pl.BlockSpec((tm, D), lambda i, c: (i, 0)),
        scratch_shapes=[pltpu.VMEM((tm * G_PITCH, PEER_KEYS), F32), pltpu.VMEM((tm, D), F32)],
        compiler_params=_cparams(("parallel", "arbitrary")),
        name="peer_experts",
    )(hn, e, w, x, u, u, v, v)


def peer_layer(x, g, w_q, subkeys, u, v, *, tm_route, tm_exp, chunk):
    wq_hi, wq_lo = _split2(w_q)
    sk_hi, sk_lo = _split2(subkeys)
    hn, e, w = peer_route(x, g, wq_hi, wq_lo, sk_hi, sk_lo, tm=tm_route)
    return peer_experts(hn, e, w, x, u.astype(BF16), v.astype(BF16), tm=tm_exp, chunk=chunk)


def _compress_core(get_x, w1k_ref, w1v_ref, pe_ref, w1f_ref, w2k_ref, w2v_ref, cos_ref, sa_ref, sb_ref,
                   kc_ref, vc_ref, acc_ref, n_cmp):
    nch = acc_ref.shape[0]
    for l in range(CMP_STRIDE):
        x = get_x(l)
        yk = _dot(x[:, :KV_W].astype(BF16), w1k_ref[l])
        yv = _dot(x[:, KV_W:].astype(BF16), w1v_ref[l])
        if l == 0:
            acc_ref[:, pl.ds(0, 2 * KV_W)] = yk
            acc_ref[:, pl.ds(2 * KV_W, 2 * KV_W)] = yv
        else:
            acc_ref[:, pl.ds(0, 2 * KV_W)] += yk
            acc_ref[:, pl.ds(2 * KV_W, 2 * KV_W)] += yv
    keep = lax.broadcasted_iota(jnp.int32, (nch, KV_W), 0) < n_cmp
    for z, (w2_ref, out_ref) in enumerate(((w2k_ref, kc_ref), (w2v_ref, vc_ref))):
        ph, pl_ = _split2(pe_ref[z])
        wh, wl = _split2(w1f_ref[z])
        pe_t = _dot(ph, wh) + _dot(ph, wl) + _dot(pl_, wh)
        pe_t = jnp.concatenate([pe_t] * B_KV_HEADS, axis=1)
        lo = acc_ref[:, pl.ds(z * 2 * KV_W, KV_W)]
        hi = acc_ref[:, pl.ds(z * 2 * KV_W + KV_W, KV_W)]
        hid = jax.nn.gelu(lo + pltpu.roll(hi, nch - 1, 0) + pe_t)
        out = _dot(hid.astype(BF16), w2_ref[...])
        if z == 0:
            out = _rope_lanes(out, cos_ref[...], sa_ref[...], sb_ref[...])
        out_ref[0] = jnp.where(keep, out, 0.0)


def _compress_prompt_kernel(x0, x1, x2, x3, *rest, n_cmp):
    slabs = (x0, x1, x2, x3)
    nch = rest[-1].shape[0]

    def get_x(l):
        return jnp.concatenate([s[pl.ds(l, nch, stride=CMP_STRIDE), :] for s in slabs], axis=1)

    _compress_core(get_x, *rest, n_cmp=n_cmp)


def _compress_weights(w_cmp1, w_cmp2, pe_cmp, n_chunk):
    eye = jnp.eye(B_KV_HEADS, dtype=F32)

    def per_head(w):
        bd = jnp.einsum('hg,...df->...hdgf', eye, w)
        return bd.reshape(w.shape[:-2] + (KV_W, KV_W)).astype(BF16)

    w1 = per_head(w_cmp1.reshape(2, 2, CMP_STRIDE, HEAD_DIM, -1))
    w1 = jnp.concatenate([w1[:, 0], w1[:, 1]], axis=-1)
    w2 = per_head(w_cmp2)
    pe = pe_cmp.reshape(2, 1, CMP_LEN * HEAD_DIM)
    w1f = w_cmp1.reshape(2, CMP_LEN * HEAD_DIM, -1)
    end_pos = jnp.arange(n_chunk, dtype=jnp.int32) * CMP_STRIDE + (CMP_LEN - 1)
    tabs = _rope_tables(end_pos, KV_W)
    return (w1[0], w1[1], pe, w1f, w2[0], w2[1]) + tabs


def _const_specs(arrs):
    return [pl.BlockSpec(a.shape, (lambda *i, nd=a.ndim: (0,) * nd), pipeline_mode=pl.Buffered(1)) for a in arrs]


def compress_prompt(kv, cw, B, S):
    nch = S // CMP_STRIDE
    slab = lambda s: pl.BlockSpec((S, LANES), lambda b: (b, s), pipeline_mode=pl.Buffered(1))
    out = pl.BlockSpec((1, nch, KV_W), lambda b: (b, 0, 0))
    return pl.pallas_call(
        functools.partial(_compress_prompt_kernel, n_cmp=nch - 1),
        out_shape=[jax.ShapeDtypeStruct((B, nch, KV_W), F32)] * 2,
        grid=(B,),
        in_specs=[slab(s) for s in range(4)] + _const_specs(cw),
        out_specs=[out, out],
        scratch_shapes=[pltpu.VMEM((nch, 4 * KV_W), F32)],
        compiler_params=_cparams(("parallel",)),
        name="compress_prompt",
    )(kv, kv, kv, kv, *cw)


def _compress_sample_kernel(pt_ref, *refs, n_cmp, pages_per_step):
    n_pg = 4 * pages_per_step
    pages, rest, x_ref = refs[:n_pg], refs[n_pg:-1], refs[-1]
    j = pl.program_id(1)
    ch_per_page = pages[0].shape[1] // CMP_STRIDE
    for p in range(pages_per_step):
        r0 = pl.multiple_of((j * pages_per_step + p) * ch_per_page, ch_per_page)
        for s in range(4):
            for l in range(CMP_STRIDE):
                x_ref[l, pl.ds(r0, ch_per_page), pl.ds(s * LANES, LANES)] = (
                    pages[p * 4 + s][0, pl.ds(l, ch_per_page, stride=CMP_STRIDE), :])

    @pl.when(j == pl.num_programs(1) - 1)
    def _():
        _compress_core(lambda l: x_ref[l], *rest, n_cmp=n_cmp)


def compress_sample(cache, page_table, cw, *, pages_per_step):
    Bd, n_pages = page_table.shape
    page = cache.shape[1]
    nch = n_pages * page // CMP_STRIDE
    P = pages_per_step

    def pspec(p, s):
        return pl.BlockSpec((1, page, LANES), lambda b, j, pt: (pt[b, j * P + p], 0, s))

    out = pl.BlockSpec((1, nch, KV_W), lambda b, j, pt: (b, 0, 0))
    return pl.pallas_call(
        functools.partial(_compress_sample_kernel, n_cmp=nch - 1, pages_per_step=P),
        out_shape=[jax.ShapeDtypeStruct((Bd, nch, KV_W), F32)] * 2,
        grid_spec=pltpu.PrefetchScalarGridSpec(
            num_scalar_prefetch=1,
            grid=(Bd, n_pages // P),
            in_specs=[pspec(p, s) for p in range(P) for s in range(4)] + _const_specs(cw),
            out_specs=[out, out],
            scratch_shapes=[pltpu.VMEM((nch, 4 * KV_W), F32),
                            pltpu.VMEM((CMP_STRIDE, nch, 2 * KV_W), F32)],
        ),
        compiler_params=_cparams(("parallel", "arbitrary")),
        name="compress_sample",
    )(page_table, *([cache] * (4 * P)), *cw)


def _select_blocks(imp, wsel, blk, cur):
    i_hi = imp.astype(BF16)
    r1 = imp - i_hi.astype(F32)
    i_mid = r1.astype(BF16)
    i_lo = (r1 - i_mid.astype(F32)).astype(BF16)
    imp_sel = _dot(i_hi, wsel) + _dot(i_mid, wsel) + _dot(i_lo, wsel)
    forced = (blk == 0) | (blk == cur) | (blk == cur - 1)
    imp_sel = jnp.where(blk > cur, -jnp.inf, jnp.where(forced, jnp.inf, imp_sel))
    return _topk_mask_cols(imp_sel.T, N_SEL).T


def _tile_block_hits(chosen_q, k0, kt_size):
    n_sb = chosen_q.shape[1]
    blk = lax.broadcasted_iota(jnp.int32, (n_sb, kt_size), 0)
    key_blk = jnp.right_shift(k0 + lax.broadcasted_iota(jnp.int32, (n_sb, kt_size), 1), 6)
    expand = jnp.where(blk == key_blk, 1.0, 0.0).astype(BF16)
    return _dot(chosen_q, expand)


def _tile_block_mask(chosen_q, k0, kt_size):
    return _tile_block_hits(chosen_q, k0, kt_size) > 0.5


def _nsa_prompt_kernel(q_ref, gate_ref, kc_ref, vc_ref, ks_ref, vs_ref, kw_ref, vw_ref, wsel_ref,
                       o_ref, *, n_cmp, kt_size):
    n = pl.program_id(1)
    t0 = n * Q_BLOCK
    G, TQ = B_GROUP, Q_BLOCK
    R = G * TQ
    NC = kc_ref.shape[1]
    NSB = wsel_ref.shape[1]
    wsel = wsel_ref[...]
    gt = gate_ref[...]
    row_c = lax.broadcasted_iota(jnp.int32, (R, NC), 0)
    col_c = lax.broadcasted_iota(jnp.int32, (R, NC), 1)
    valid_c = (col_c * CMP_STRIDE + (CMP_LEN - 1) <= t0 + jnp.bitwise_and(row_c, TQ - 1)) & (col_c < n_cmp)
    bias_c = jnp.where(valid_c, 0.0, NEG)
    sees_cmp = t0 + jnp.bitwise_and(lax.broadcasted_iota(jnp.int32, (R, 1), 0), TQ - 1) >= CMP_LEN - 1
    blk = lax.broadcasted_iota(jnp.int32, (TQ, NSB), 1)
    cur = jnp.right_shift(t0 + lax.broadcasted_iota(jnp.int32, (TQ, NSB), 0), 6)
    WK = B_WINDOW + TQ
    w0 = pl.multiple_of(jnp.maximum(t0 - B_WINDOW, 0), TQ)
    wpos = w0 + lax.broadcasted_iota(jnp.int32, (R, WK), 1)
    rel = t0 + jnp.bitwise_and(lax.broadcasted_iota(jnp.int32, (R, WK), 0), TQ - 1) - wpos
    bias_w = jnp.where((rel >= 0) & (rel <= B_WINDOW), 0.0, NEG)
    n_kt = (t0 + TQ + kt_size - 1) // kt_size
    k_last = pl.multiple_of((n_kt - 1) * kt_size, kt_size)
    causal_last = jnp.where(k_last + lax.broadcasted_iota(jnp.int32, (TQ, kt_size), 1)
                            <= t0 + lax.broadcasted_iota(jnp.int32, (TQ, kt_size), 0), 0.0, NEG)
    ones_col = jnp.where(lax.broadcasted_iota(jnp.int32, (kt_size, HEAD_DIM), 1) == 0, 1.0, 0.0).astype(BF16)

    head_cols = [pl.ds(h * HEAD_DIM, HEAD_DIM) for h in range(B_KV_HEADS)]

    qbs, o_cs, chosens = [], [], []
    for h, hs in enumerate(head_cols):
        qs = jnp.concatenate([q_ref[:, pl.ds((h * G + g) * HEAD_DIM, HEAD_DIM)] for g in range(G)], axis=0)
        qs = qs * (SCALE * LOG2E)
        s_c = _dot_nt3(qs, kc_ref[0, :, hs]) + bias_c
        e_c = jnp.exp2(s_c - jnp.max(s_c, axis=-1, keepdims=True))
        p_c = e_c * jnp.where(sees_cmp, 1.0 / jnp.sum(e_c, axis=-1, keepdims=True), 0.0)
        imp = p_c[0:TQ]
        for g in range(1, G):
            imp = imp + p_c[g * TQ:(g + 1) * TQ]
        qbs.append(qs.astype(BF16))
        o_cs.append(_dot(p_c.astype(BF16), vc_ref[0, :, hs].astype(BF16)))
        chosens.append(_select_blocks(imp, wsel, blk, cur).astype(BF16))

    def sel_tile(h, k0, carry, extra_bias):
        m_i, acc = carry
        hs = head_cols[h]
        bias = (_tile_block_hits(chosens[h], k0, kt_size) - 1.0) * (-NEG)
        if extra_bias is not None:
            bias = bias + extra_bias
        s = _dot_nt(qbs[h], ks_ref[pl.ds(k0, kt_size), hs]).reshape(G, TQ, kt_size) + bias[None]
        s = s.reshape(R, kt_size)
        m_new = jnp.maximum(m_i, jnp.max(s, axis=-1, keepdims=True))
        p = jnp.exp2(s - m_new).astype(BF16)
        v1 = jnp.concatenate([vs_ref[pl.ds(k0, kt_size), hs], ones_col], axis=1)
        return m_new, jnp.exp2(m_i - m_new) * acc + _dot(p, v1)

    def sel_body(kt, carries):
        k0 = pl.multiple_of(kt * kt_size, kt_size)
        return tuple(sel_tile(h, k0, carries[h], None) for h in range(B_KV_HEADS))

    init = tuple((jnp.full((R, 1), NEG, F32), jnp.zeros((R, 2 * HEAD_DIM), F32)) for _ in range(B_KV_HEADS))
    carries = lax.fori_loop(0, n_kt - 1, sel_body, init)

    outs = []
    for h, hs in enumerate(head_cols):
        _, acc_s = sel_tile(h, k_last, carries[h], causal_last)
        o_s = acc_s[:, :HEAD_DIM] / acc_s[:, HEAD_DIM:HEAD_DIM + 1]
        s_w = _dot_nt(qbs[h], kw_ref[pl.ds(w0, WK), hs]) + bias_w
        e_w = jnp.exp2(s_w - jnp.max(s_w, axis=-1, keepdims=True))
        p_w = e_w * (1.0 / jnp.sum(e_w, axis=-1, keepdims=True))
        o_w = _dot(p_w.astype(BF16), vw_ref[pl.ds(w0, WK), hs])
        o_c = o_cs[h]
        for g in range(G):
            rs = slice(g * TQ, (g + 1) * TQ)
            c = 3 * (h * G + g)
            outs.append(gt[:, c:c + 1] * o_c[rs] + gt[:, c + 1:c + 2] * o_s[rs] + gt[:, c + 2:c + 3] * o_w[rs])
    o_ref[...] = jnp.concatenate(outs, axis=1)


def nsa_prompt(q, gate, kc, vc, kvb, wsel, B, S, *, kt_size):
    nq = q.shape[1]
    nqb = S // Q_BLOCK
    NC = kc.shape[1]
    comp = lambda c: pl.BlockSpec((S, KV_W), lambda b, n: (b, c))
    cmp_spec = pl.BlockSpec((1, NC, KV_W), lambda b, n: (b, 0, 0))
    return pl.pallas_call(
        functools.partial(_nsa_prompt_kernel, n_cmp=S // CMP_STRIDE - 1, kt_size=kt_size),
        out_shape=jax.ShapeDtypeStruct((B * S, nq), F32),
        grid=(B, nqb),
        in_specs=[pl.BlockSpec((Q_BLOCK, nq), lambda b, n: (b * nqb + n, 0)),
                  pl.BlockSpec((Q_BLOCK, LANES), lambda b, n: (b * nqb + n, 0)),
                  cmp_spec, cmp_spec, comp(2), comp(3), comp(4), comp(5),
                  pl.BlockSpec(wsel.shape, lambda b, n: (0, 0))],
        out_specs=pl.BlockSpec((Q_BLOCK, nq), lambda b, n: (b * nqb + n, 0)),
        compiler_params=_cparams(("parallel", "arbitrary")),
        name="nsa_prompt",
    )(q, gate, kc, vc, kvb, kvb, kvb, kvb, wsel)


def _nsa_step_kernel(pt_ref, q_ref, gate_ref, kc_ref, vc_ref, new_ref, win_ref, wsel_ref, *rest,
                     past, n_cmp, pages_per_step):
    P = pages_per_step
    pages = rest[:P]
    o_ref, st_ref, a_ref, ch_ref, m_ref, l_ref, acc_ref, oc_ref = rest[P:]
    del pt_ref
    j = pl.program_id(1)
    G, KVH = B_GROUP, B_KV_HEADS
    Sd = q_ref.shape[0]
    R = B_HEADS * Sd
    page = pages[0].shape[1]
    kt = P * page
    row_head = lambda shape: lax.broadcasted_iota(jnp.int32, shape, 0) // (G * Sd)
    own = lambda width: row_head((R, width)) == lax.broadcasted_iota(jnp.int32, (R, width), 1) // HEAD_DIM

    def diag(x):
        rows = G * Sd
        return jnp.concatenate([x[h * rows:(h + 1) * rows, h * HEAD_DIM:(h + 1) * HEAD_DIM]
                                for h in range(KVH)], axis=0)

    @pl.when(j == 0)
    def _():
        q = q_ref[...]
        a = jnp.concatenate([q[:, hd * HEAD_DIM:(hd + 1) * HEAD_DIM] for hd in range(B_HEADS)], axis=0)
        a = jnp.where(own(KV_W), jnp.concatenate([a] * KVH, axis=1), 0.0)
        a_ref[...] = a.astype(BF16)
        NC = kc_ref.shape[1]
        valid_c = lax.broadcasted_iota(jnp.int32, (R, NC), 1) < n_cmp
        p_c = _softmax_rows(_dot_nt3(a, kc_ref[0]) * SCALE, valid_c)
        oc_ref[...] = diag(_dot(p_c.astype(BF16), vc_ref[0].astype(BF16)))
        pooled = jnp.sum(p_c.reshape(KVH, G, Sd, NC), axis=1, keepdims=True)
        imp = jnp.broadcast_to(pooled, (KVH, G, Sd, NC)).reshape(R, NC)
        NSB = wsel_ref.shape[1]
        blk = lax.broadcasted_iota(jnp.int32, (R, NSB), 1)
        srow = jnp.bitwise_and(lax.broadcasted_iota(jnp.int32, (R, NSB), 0), Sd - 1)
        cur = jnp.right_shift(past + srow, 6)
        ch_ref[...] = _select_blocks(imp, wsel_ref[...], blk, cur).astype(BF16)
        m_ref[...] = jnp.full(m_ref.shape, NEG, F32)
        l_ref[...] = jnp.zeros(l_ref.shape, F32)
        acc_ref[...] = jnp.zeros(acc_ref.shape, F32)

    a = a_ref[...]
    ks = jnp.concatenate([pg[0, :, pl.ds(0, KV_W)] for pg in pages], axis=0).astype(BF16)
    vs = jnp.concatenate([pg[0, :, pl.ds(KV_W, KV_W)] for pg in pages], axis=0).astype(BF16)
    mk = _tile_block_mask(ch_ref[...], j * kt, kt)
    m_n, l_n, acc_n = _online_update((m_ref[...], l_ref[...], acc_ref[...]), _dot_nt(a, ks) * SCALE, mk, vs)
    m_ref[...] = m_n
    l_ref[...] = l_n
    acc_ref[...] = acc_n

    @pl.when(j == pl.num_programs(1) - 1)
    def _():
        new = _pad_rows(new_ref[...], LANES)
        comp = lambda c: new[:, c * KV_W:(c + 1) * KV_W].astype(BF16)
        srow = jnp.bitwise_and(lax.broadcasted_iota(jnp.int32, (R, LANES), 0), Sd - 1)
        scol = lax.broadcasted_iota(jnp.int32, (R, LANES), 1)
        causal_new = (scol <= srow) & (scol < Sd)
        nblk = past // SEL_BLOCK
        in_new = ch_ref[:, nblk:nblk + 1].astype(F32) > 0.5
        _, l_s, acc_s = _online_update((m_n, l_n, acc_n), _dot_nt(a, comp(2)) * SCALE, in_new & causal_new, comp(3))
        o_s = diag(acc_s) / jnp.where(l_s > 0, l_s, 1.0)

        W = win_ref.shape[1]
        wrow = jnp.bitwise_and(lax.broadcasted_iota(jnp.int32, (R, W), 0), Sd - 1)
        rel = W + wrow - lax.broadcasted_iota(jnp.int32, (R, W), 1)
        carry = _online_update(_online_init(R, KV_W), _dot_nt(a, win_ref[0, :, pl.ds(0, KV_W)].astype(BF16)) * SCALE,
                               (rel >= 0) & (rel <= B_WINDOW), win_ref[0, :, pl.ds(KV_W, KV_W)].astype(BF16))
        _, l_w, acc_w = _online_update(carry, _dot_nt(a, comp(4)) * SCALE, causal_new, comp(5))
        o_w = diag(acc_w) / jnp.where(l_w > 0, l_w, 1.0)
        o_c = oc_ref[...]
        gt = gate_ref[...]
        outs = []
        for hd in range(B_HEADS):
            rs = slice(hd * Sd, (hd + 1) * Sd)
            c = 3 * hd
            outs.append(gt[:, c:c + 1] * o_c[rs] + gt[:, c + 1:c + 2] * o_s[rs] + gt[:, c + 2:c + 3] * o_w[rs])
        o_ref[...] = jnp.concatenate(outs, axis=1)
        st_ref[0, pl.ds(0, W - Sd), :] = win_ref[0, pl.ds(Sd, W - Sd), :]
        st_ref[0, pl.ds(W - Sd, Sd), :] = new_ref[:, pl.ds(4 * KV_W, 2 * KV_W)]


def nsa_step(q, gate, kc, vc, kv, win, cache, page_table, wsel, Tp, Bd, Sd, *, pages_per_step):
    nq = q.shape[1]
    n_pages = page_table.shape[1]
    page = cache.shape[1]
    past = n_pages * page
    W = win.shape[1]
    NC = kc.shape[1]
    P = pages_per_step
    r0 = Tp // Sd
    R = B_HEADS * Sd
    rows = lambda w: pl.BlockSpec((Sd, w), lambda b, j, pt: (r0 + b, 0))
    per_b = lambda s1, s2: pl.BlockSpec((1, s1, s2), lambda b, j, pt: (b, 0, 0))
    pspec = lambda p: pl.BlockSpec((1, page, 2 * KV_W), lambda b, j, pt: (pt[b, j * P + p], 0, 1))
    return pl.pallas_call(
        functools.partial(_nsa_step_kernel, past=past, n_cmp=past // CMP_STRIDE - 1, pages_per_step=P),
        out_shape=[jax.ShapeDtypeStruct((Bd * Sd, nq), F32), jax.ShapeDtypeStruct(win.shape, F32)],
        grid_spec=pltpu.PrefetchScalarGridSpec(
            num_scalar_prefetch=1,
            grid=(Bd, n_pages // P),
            in_specs=[rows(nq), rows(LANES), per_b(NC, KV_W), per_b(NC, KV_W), rows(kv.shape[1]),
                      per_b(W, 2 * KV_W), pl.BlockSpec(wsel.shape, lambda b, j, pt: (0, 0))]
                     + [pspec(p) for p in range(P)],
            out_specs=[pl.BlockSpec((Sd, nq), lambda b, j, pt: (b, 0)), per_b(W, 2 * KV_W)],
            scratch_shapes=[pltpu.VMEM((R, KV_W), BF16), pltpu.VMEM((R, wsel.shape[1]), BF16),
                            pltpu.VMEM((R, 1), F32), pltpu.VMEM((R, 1), F32), pltpu.VMEM((R, KV_W), F32),
                            pltpu.VMEM((R, HEAD_DIM), F32)],
        ),
        compiler_params=_cparams(("parallel", "arbitrary")),
        name="nsa_step",
    )(page_table, q, gate, kc, vc, kv, win, wsel, *([cache] * P))


def _rope_tables(pos, width):
    half = ROT_DIM // 2
    inv = jnp.exp(-jnp.log(jnp.float32(ROPE_THETA)) * jnp.arange(half, dtype=F32) * (2.0 / ROT_DIM))
    ang = pos.astype(F32)[:, None] * inv[None, :]
    cos, sin = jnp.cos(ang), jnp.sin(ang)
    T = pos.shape[0]
    z8 = jnp.zeros((T, half), F32)
    rest = HEAD_DIM - ROT_DIM
    c = jnp.concatenate([cos, cos, jnp.ones((T, rest), F32)], axis=1)
    sa = jnp.concatenate([-sin, z8, jnp.zeros((T, rest), F32)], axis=1)
    sb = jnp.concatenate([z8, sin, jnp.zeros((T, rest), F32)], axis=1)
    reps = width // HEAD_DIM
    return tuple(jnp.tile(t, (1, reps)) for t in (c, sa, sb))


def _overlap_weights(n_cmp, n_cmp_pad, n_sb, n_sb_pad):
    j = jnp.arange(n_sb)
    cidx = (SEL_BLOCK // CMP_STRIDE) * j[:, None] - 1 + jnp.arange(len(SEL_OVERLAP_W))[None, :]
    wts = jnp.where((cidx >= 0) & (cidx < n_cmp), jnp.asarray(SEL_OVERLAP_W, F32), 0.0)
    onehot = (jnp.clip(cidx, 0, n_cmp - 1)[:, :, None] == jnp.arange(n_cmp_pad)[None, None, :]).astype(F32)
    wsel = jnp.einsum('jo,jon->nj', wts, onehot)
    return jnp.pad(wsel, ((0, 0), (0, n_sb_pad - n_sb))).astype(BF16)


def _pick(n, cands):
    for c in cands:
        if n % c == 0:
            return c
    return n


def _pad_to(n, m):
    return -(-n // m) * m


def kernel(x_prompt, x_sample, cache_a_g1, cache_a_g2, cache_a_g3, cache_b_kv, cache_b_win, page_table, g_mix, g_ffn, w_in_a, w_o_a, g_kv, w_kv_b, w_cmp1, w_cmp2, pe_cmp, w_qg_b, b_gate_b, w_o_b, w_peer_q, peer_subkeys, peer_u, peer_v, g_final):
    B, S, D = x_prompt.shape
    Bd, Sd, _ = x_sample.shape
    Tp, Ts = B * S, Bd * Sd
    T = Tp + Ts
    n_phys, page = cache_b_kv.shape[:2]
    n_pages = page_table.shape[1]
    past = n_pages * page
    hw = A_HEADS * HEAD_DIM
    tm_big = _pick(T, (1280, 640, 256, 128, 64, 32, 16, 8))
    tm_mid = _pick(T, (640, 256, 128, 64, 32, 16, 8))
    tm_peer = _pick(T, (256, 128))
    tm_exp = _pick(T, (320, 256, 128, 64, 32, 16, 8))
    tm_rows = _pick(math.gcd(Tp, Ts), (256, 128, 64, 32, 16, 8))
    pages_per_step = _pick(n_pages, (16, 8, 4, 2, 1))

    x = jnp.concatenate([x_prompt.reshape(Tp, D), x_sample.reshape(Ts, D)], axis=0)
    pos_p = jnp.arange(S, dtype=jnp.int32)
    pos_s = past + jnp.arange(Sd, dtype=jnp.int32)
    pos = jnp.concatenate([jnp.tile(pos_p, B), jnp.tile(pos_s, Bd)])
    tab512 = _rope_tables(pos, 512)
    tab256 = tuple(t[:, :KV_W] for t in tab512)

    n_grp = len(A_GROUPS)
    proj = rms_matmul(x, g_mix[0], w_in_a[0].astype(BF16), tm=tm_big, tn=hw, epilogue="rope",
                      flags=jnp.asarray([1, 1, 0] * n_grp, jnp.int32), tables=tab512)
    caches_a = (cache_a_g1, cache_a_g2, cache_a_g3)
    op, lp, os_, ls_, a_prompt, a_sample = [], [], [], [], [], []
    for g, (win, dil) in enumerate(A_GROUPS):
        o_g, l_g = band_attention(proj, g, win, dil, B, S)
        cache = caches_a[g]
        o_s, l_s, state = a_step(proj, g, win, dil, cache[0].reshape(Bd, cache.shape[2], 2 * hw), Tp, Bd, Sd)
        a_sample.append(state.reshape(cache.shape))
        keep = min(win, S)
        c0 = g * 3 * hw + hw
        kv_cols = jnp.stack([proj[(b + 1) * S - keep:(b + 1) * S, c0:c0 + 2 * hw] for b in range(B)])
        a_prompt.append(kv_cols.reshape(1, B, keep, 2, A_HEADS, HEAD_DIM))
        op.append(o_g)
        lp.append(l_g)
        os_.append(o_s)
        ls_.append(l_s)
    x = merge_matmul_residual(op + lp, os_ + ls_, w_o_a[0].astype(BF16), x, tm=tm_rows)
    x = peer_layer(x, g_ffn[0], w_peer_q[0], peer_subkeys[0], peer_u[0], peer_v[0],
                   tm_route=tm_peer, tm_exp=tm_exp, chunk=1024)

    kv, kvb = rms_matmul(x, g_kv, w_kv_b.astype(BF16), tm=tm_big, tn=KV_W, epilogue="rope",
                         flags=jnp.asarray([0, 0, 1, 0, 1, 0], jnp.int32), tables=tab256, emit_bf16=True)
    bkv_prompt = kv[:Tp, :4 * KV_W].reshape(B, S, 4, B_KV_HEADS, HEAD_DIM)
    bkv_sample = kv[Tp:, :4 * KV_W].reshape(Bd, Sd, 4, B_KV_HEADS, HEAD_DIM)
    keep_w = min(B_WINDOW, S)
    bwin_prompt = jnp.stack([kv[(b + 1) * S - keep_w:(b + 1) * S, 4 * KV_W:] for b in range(B)])
    bwin_prompt = bwin_prompt.reshape(B, keep_w, 2, B_KV_HEADS, HEAD_DIM)
    cache2d = cache_b_kv.reshape(n_phys, page, 4 * KV_W)
    cw_p = _compress_weights(w_cmp1, w_cmp2, pe_cmp, S // CMP_STRIDE)
    cw_s = cw_p if past == S else _compress_weights(w_cmp1, w_cmp2, pe_cmp, past // CMP_STRIDE)
    kc_p, vc_p = compress_prompt(kv, cw_p, B, S)
    kc_s, vc_s = compress_sample(cache2d, page_table, cw_s, pages_per_step=pages_per_step)

    nq = B_HEADS * HEAD_DIM
    w_qg = w_qg_b[0]
    q = rms_matmul(x, g_mix[1], w_qg[:, :nq].astype(BF16), tm=tm_big, tn=512, epilogue="rope",
                   flags=jnp.ones((nq // 512,), jnp.int32), tables=tab512)
    ngate = 3 * B_HEADS
    w_gate = jnp.pad(w_qg[:, nq:], ((0, 0), (0, LANES - ngate))).astype(BF16)
    b_gate = jnp.pad(b_gate_b[0], (0, LANES - ngate))
    gate = rms_matmul(x, g_mix[1], w_gate, tm=tm_big, tn=LANES, epilogue="sigmoid", bias=b_gate)

    n_cmp_p = S // CMP_STRIDE - 1
    wsel_p = _overlap_weights(n_cmp_p, S // CMP_STRIDE, S // SEL_BLOCK, _pad_to(S // SEL_BLOCK, LANES))
    o_b_p = nsa_prompt(q, gate, kc_p, vc_p, kvb, wsel_p, B, S, kt_size=min(1024, S))
    n_cmp_s = past // CMP_STRIDE - 1
    n_sb_s = -(-(past + Sd) // SEL_BLOCK)
    wsel_s = _overlap_weights(n_cmp_s, past // CMP_STRIDE, n_sb_s, _pad_to(n_sb_s, LANES))
    Wb = cache_b_win.shape[1]
    o_b_s, bwin_sample = nsa_step(q, gate, kc_s, vc_s, kv, cache_b_win.reshape(Bd, Wb, 2 * KV_W), cache2d,
                                  page_table, wsel_s, Tp, Bd, Sd, pages_per_step=pages_per_step)
    x = matmul_residual(o_b_p, o_b_s, w_o_b[0].astype(BF16), x, tm=tm_rows)
    x = peer_layer(x, g_ffn[1], w_peer_q[1], peer_subkeys[1], peer_u[1], peer_v[1],
                   tm_route=tm_peer, tm_exp=tm_exp, chunk=1024)
    y = rmsnorm_rows(x, g_final, tm=tm_mid)

    return (y[:Tp].reshape(B, S, D), y[Tp:].reshape(Bd, Sd, D),
            a_prompt[0], a_sample[0], a_prompt[1], a_sample[1], a_prompt[2], a_sample[2],
            bkv_prompt, bkv_sample, bwin_prompt, bwin_sample.reshape(cache_b_win.shape))
```

```python
import functools
import math

import jax
import jax.numpy as jnp
from jax import lax
from jax.experimental import pallas as pl
from jax.experimental.pallas import tpu as pltpu

F32 = jnp.float32
BF16 = jnp.bfloat16

HEAD_DIM = 64
ROT_DIM = HEAD_DIM // 4
ROPE_THETA = 500000.0
NORM_EPS = 1e-6
A_GROUPS = ((128, 1), (512, 4), (2048, 16))
A_HEADS = 8
A_BLOCK = 128
B_HEADS = 16
B_KV_HEADS = 4
B_GROUP = B_HEADS // B_KV_HEADS
CMP_LEN = 32
CMP_STRIDE = 16
SEL_BLOCK = 64
N_SEL = 16
SEL_OVERLAP_W = (1.0, 2.0, 2.0, 2.0, 1.0)
B_WINDOW = 512
Q_BLOCK = 128
PEER_HEADS = 8
PEER_KEYS = 128
PEER_QDIM = 256
PEER_TOPK = 16

LANES = 128
NEG = -1e30
VMEM_LIMIT = 50 * 1024 * 1024
G_PITCH = 136
KV_W = B_KV_HEADS * HEAD_DIM
SCALE = HEAD_DIM ** -0.5
LOG2E = 1.4426950408889634
LN2 = 0.6931471805599453

_NT = (((1,), (1,)), ((), ()))


def _cparams(sem):
    return pltpu.CompilerParams(dimension_semantics=sem, vmem_limit_bytes=VMEM_LIMIT)


def _split2(a):
    hi = a.astype(BF16)
    lo = (a - hi.astype(F32)).astype(BF16)
    return hi, lo


def _dot(a, b):
    return jnp.dot(a, b, preferred_element_type=F32)


def _dot_nt(a, b):
    return lax.dot_general(a, b, _NT, preferred_element_type=F32)


def _dot_nt3(a, b):
    ah, al = _split2(a)
    bh, bl = _split2(b)
    return _dot_nt(ah, bh) + _dot_nt(ah, bl) + _dot_nt(al, bh)


def _softmax_rows(s, valid):
    s = jnp.where(valid, s, NEG)
    m = jnp.max(s, axis=-1, keepdims=True)
    e = jnp.where(valid, jnp.exp(s - m), 0.0)
    den = jnp.sum(e, axis=-1, keepdims=True)
    return e / jnp.where(den > 0, den, 1.0)


def _rope_lanes(y, cos, sa, sb):
    n = y.shape[-1]
    return y * cos + pltpu.roll(y, 8, 1) * sb + pltpu.roll(y, n - 8, 1) * sa


def _rms_norm_rows(x, g):
    ms = jnp.mean(x * x, axis=-1, keepdims=True)
    return x * lax.rsqrt(ms + NORM_EPS) * g


def _rms_mm_kernel(flags_ref, x_ref, g_ref, w_ref, *rest, epilogue, emit_bf16):
    n_out = 2 if emit_bf16 else 1
    extra, outs, xn_ref = rest[:len(rest) - n_out - 1], rest[len(rest) - n_out - 1:-1], rest[-1]
    j = pl.program_id(1)

    @pl.when(j == 0)
    def _():
        xn_ref[...] = _rms_norm_rows(x_ref[...], g_ref[...]).astype(BF16)

    def store(val):
        outs[0][...] = val
        if emit_bf16:
            outs[1][...] = val.astype(BF16)

    y = _dot(xn_ref[...], w_ref[...])
    if epilogue == "rope":
        cos_ref, sa_ref, sb_ref = extra
        f = flags_ref[j]

        @pl.when(f == 0)
        def _():
            store(y)

        @pl.when(f != 0)
        def _():
            store(_rope_lanes(y, cos_ref[...], sa_ref[...], sb_ref[...]))
    elif epilogue == "sigmoid":
        store(jax.nn.sigmoid(y + extra[0][...]))
    else:
        store(y)


def rms_matmul(x, g, w, *, tm, tn, epilogue="none", flags=None, tables=None, bias=None, emit_bf16=False):
    M, D = x.shape
    N = w.shape[1]
    assert M % tm == 0 and N % tn == 0
    nj = N // tn
    if flags is None:
        flags = jnp.zeros((nj,), jnp.int32)
    in_specs = [
        pl.BlockSpec((tm, D), lambda i, j, f: (i, 0)),
        pl.BlockSpec((1, D), lambda i, j, f: (0, 0)),
        pl.BlockSpec((D, tn), lambda i, j, f: (0, j)),
    ]
    args = [x, g.reshape(1, D), w]
    if epilogue == "rope":
        for t in tables:
            in_specs.append(pl.BlockSpec((tm, tn), lambda i, j, f: (i, 0)))
            args.append(t)
    elif epilogue == "sigmoid":
        in_specs.append(pl.BlockSpec((1, tn), lambda i, j, f: (0, j)))
        args.append(bias.reshape(1, N))
    out_spec = pl.BlockSpec((tm, tn), lambda i, j, f: (i, j))
    out_shape = [jax.ShapeDtypeStruct((M, N), F32)]
    if emit_bf16:
        out_shape.append(jax.ShapeDtypeStruct((M, N), BF16))
    res = pl.pallas_call(
        functools.partial(_rms_mm_kernel, epilogue=epilogue, emit_bf16=emit_bf16),
        out_shape=out_shape,
        grid_spec=pltpu.PrefetchScalarGridSpec(
            num_scalar_prefetch=1,
            grid=(M // tm, nj),
            in_specs=in_specs,
            out_specs=[out_spec] * len(out_shape),
            scratch_shapes=[pltpu.VMEM((tm, D), BF16)],
        ),
        compiler_params=_cparams(("parallel", "arbitrary")),
        name="rms_matmul_" + epilogue,
    )(flags, *args)
    return res if emit_bf16 else res[0]


def _two_source_specs(n_arr, tm, K, n_p):
    p_spec = pl.BlockSpec((tm, K), lambda i: (jnp.minimum(i, n_p - 1), 0))
    s_spec = pl.BlockSpec((tm, K), lambda i: (jnp.maximum(i - n_p, 0), 0))
    return [p_spec] * n_arr + [s_spec] * n_arr


def _mm_res_kernel(ap_ref, as_ref, w_ref, r_ref, o_ref, *, n_p):
    a = jnp.where(pl.program_id(0) >= n_p, as_ref[...], ap_ref[...])
    o_ref[...] = r_ref[...] + _dot(a.astype(BF16), w_ref[...])


def matmul_residual(a_p, a_s, w, res, *, tm):
    M, N = res.shape
    K = a_p.shape[1]
    n_p = a_p.shape[0] // tm
    assert a_p.shape[0] % tm == 0 and a_s.shape[0] % tm == 0
    return pl.pallas_call(
        functools.partial(_mm_res_kernel, n_p=n_p),
        out_shape=jax.ShapeDtypeStruct((M, N), F32),
        grid=(M // tm,),
        in_specs=_two_source_specs(1, tm, K, n_p) + [pl.BlockSpec((K, N), lambda i: (0, 0)),
                                                    pl.BlockSpec((tm, N), lambda i: (i, 0))],
        out_specs=pl.BlockSpec((tm, N), lambda i: (i, 0)),
        compiler_params=_cparams(("parallel",)),
        name="matmul_residual",
    )(a_p, a_s, w, res)


def _merge_mm_res_kernel(*refs, n_p):
    prompt, sample, (w_ref, r_ref, out_ref) = refs[:6], refs[6:12], refs[12:]
    is_s = pl.program_id(0) >= n_p
    o1, o2, o3, a1, a2, a3 = [jnp.where(is_s, s[...], p[...]) for p, s in zip(prompt, sample)]
    m = jnp.maximum(jnp.maximum(a1, a2), a3)
    e1, e2, e3 = jnp.exp(a1 - m), jnp.exp(a2 - m), jnp.exp(a3 - m)
    den = e1 + e2 + e3
    a = (e1 / den) * o1 + (e2 / den) * o2 + (e3 / den) * o3
    out_ref[...] = r_ref[...] + _dot(a.astype(BF16), w_ref[...])


def merge_matmul_residual(ol_p, ol_s, w, res, *, tm):
    M, N = res.shape
    K = ol_p[0].shape[1]
    n_p = ol_p[0].shape[0] // tm
    assert ol_p[0].shape[0] % tm == 0 and ol_s[0].shape[0] % tm == 0
    return pl.pallas_call(
        functools.partial(_merge_mm_res_kernel, n_p=n_p),
        out_shape=jax.ShapeDtypeStruct((M, N), F32),
        grid=(M // tm,),
        in_specs=_two_source_specs(6, tm, K, n_p) + [pl.BlockSpec((K, N), lambda i: (0, 0)),
                                                    pl.BlockSpec((tm, N), lambda i: (i, 0))],
        out_specs=pl.BlockSpec((tm, N), lambda i: (i, 0)),
        compiler_params=_cparams(("parallel",)),
        name="merge_matmul_residual",
    )(*ol_p, *ol_s, w, res)


def _rmsnorm_kernel(x_ref, g_ref, o_ref):
    o_ref[...] = _rms_norm_rows(x_ref[...], g_ref[...])


def rmsnorm_rows(x, g, *, tm):
    M, D = x.shape
    return pl.pallas_call(
        _rmsnorm_kernel,
        out_shape=jax.ShapeDtypeStruct((M, D), F32),
        grid=(M // tm,),
        in_specs=[pl.BlockSpec((tm, D), lambda i: (i, 0)), pl.BlockSpec((1, D), lambda i: (0, 0))],
        out_specs=pl.BlockSpec((tm, D), lambda i: (i, 0)),
        compiler_params=_cparams(("parallel",)),
        name="rmsnorm",
    )(x, g.reshape(1, D))


def _online_update(carry, s, mask, v):
    m_i, l_i, acc = carry
    s = jnp.where(mask, s, NEG)
    m_new = jnp.maximum(m_i, jnp.max(s, axis=-1, keepdims=True))
    alpha = jnp.exp(m_i - m_new)
    p = jnp.where(mask, jnp.exp(s - m_new), 0.0)
    return (m_new, alpha * l_i + jnp.sum(p, axis=-1, keepdims=True), alpha * acc + _dot(p.astype(BF16), v))


def _online_init(rows, width):
    return (jnp.full((rows, 1), NEG, F32), jnp.zeros((rows, 1), F32), jnp.zeros((rows, width), F32))


def _attend_heads(q, segments, n_heads):
    nq = q.shape[0]
    outs, lses = [], []
    for h in range(n_heads):
        qh = q[:, h * HEAD_DIM:(h + 1) * HEAD_DIM].astype(BF16)
        carry = _online_init(nq, HEAD_DIM)
        for k_rows, v_rows, mask in segments:
            carry = _online_update(carry, _dot_nt(qh, k_rows(h)) * SCALE, mask, v_rows(h))
        m, den, acc = carry
        outs.append(acc / den)
        lses.append(jnp.broadcast_to(m + jnp.log(den), (nq, HEAD_DIM)))
    return jnp.concatenate(outs, axis=1), jnp.concatenate(lses, axis=1)


def _pad_rows(x, n):
    return jnp.concatenate([x, jnp.zeros((n - x.shape[0], x.shape[1]), x.dtype)], axis=0)


def _band_attn_kernel(q_ref, kp_ref, kc_ref, vp_ref, vc_ref, o_ref, l_ref, *, nw):
    n = pl.program_id(2)
    kp, kc, vp, vc = kp_ref[...], kc_ref[...], vp_ref[...], vc_ref[...]
    i = lax.broadcasted_iota(jnp.int32, (A_BLOCK, 2 * A_BLOCK), 0)
    j = lax.broadcasted_iota(jnp.int32, (A_BLOCK, 2 * A_BLOCK), 1)
    rel = A_BLOCK + i - j
    bias = jnp.where((rel >= 0) & (rel <= nw) & ((j >= A_BLOCK) | (n > 0)), 0.0, NEG)
    q = q_ref[...] * (SCALE * LOG2E)
    outs, lses = [], []
    for h in range(A_HEADS):
        hs = slice(h * HEAD_DIM, (h + 1) * HEAD_DIM)
        kk = jnp.concatenate([kp[:, hs], kc[:, hs]], axis=0).astype(BF16)
        vv = jnp.concatenate([vp[:, hs], vc[:, hs]], axis=0).astype(BF16)
        s = _dot_nt(q[:, hs].astype(BF16), kk) + bias
        m = jnp.max(s, axis=-1, keepdims=True)
        e = jnp.exp2(s - m)
        den = jnp.sum(e, axis=-1, keepdims=True)
        outs.append(_dot(e.astype(BF16), vv) / den)
        lses.append(jnp.broadcast_to((m + jnp.log2(den)) * LN2, (A_BLOCK, HEAD_DIM)))
    o_ref[...] = jnp.concatenate(outs, axis=1)
    l_ref[...] = jnp.concatenate(lses, axis=1)


def band_attention(proj, g, window, dil, B, S):
    T, ncol = proj.shape
    L = S // dil
    nb = L // A_BLOCK
    hw = A_HEADS * HEAD_DIM
    per_res = ncol // hw
    view = proj.reshape(T // dil, dil * ncol)

    def spec(c, prev):
        def imap(b, r, n):
            nn = jnp.maximum(n - 1, 0) if prev else n
            return (b * nb + nn, r * per_res + g * 3 + c)
        return pl.BlockSpec((A_BLOCK, hw), imap)

    out_spec = pl.BlockSpec((A_BLOCK, hw), lambda b, r, n: (b * nb + n, r))
    o, l = pl.pallas_call(
        functools.partial(_band_attn_kernel, nw=window // dil),
        out_shape=[jax.ShapeDtypeStruct((B * L, dil * hw), F32)] * 2,
        grid=(B, dil, nb),
        in_specs=[spec(0, False), spec(1, True), spec(1, False), spec(2, True), spec(2, False)],
        out_specs=[out_spec, out_spec],
        compiler_params=_cparams(("parallel", "parallel", "arbitrary")),
        name=f"band_attention_d{dil}",
    )(view, view, view, view, view)
    return o.reshape(B * S, hw), l.reshape(B * S, hw)


def _a_step_kernel(q_ref, k_ref, v_ref, c_ref, o_ref, l_ref, s_ref, *, win, dil):
    W = c_ref.shape[1]
    Sd = q_ref.shape[0]
    hw = A_HEADS * HEAD_DIM
    knew, vnew = k_ref[...], v_ref[...]
    s_ref[0, pl.ds(0, W - Sd), :] = c_ref[0, pl.ds(Sd, W - Sd), :]
    s_ref[0, pl.ds(W - Sd, Sd), pl.ds(0, hw)] = knew
    s_ref[0, pl.ds(W - Sd, Sd), pl.ds(hw, hw)] = vnew

    def on_stride(dist):
        return (dist >= 0) & (dist <= win) & (jnp.bitwise_and(dist, dil - 1) == 0)

    dist_buf = (W + lax.broadcasted_iota(jnp.int32, (Sd, W), 0)) - lax.broadcasted_iota(jnp.int32, (Sd, W), 1)
    ncol = lax.broadcasted_iota(jnp.int32, (Sd, LANES), 1)
    dist_new = lax.broadcasted_iota(jnp.int32, (Sd, LANES), 0) - ncol
    knew_p, vnew_p = _pad_rows(knew, LANES), _pad_rows(vnew, LANES)
    buf_rows = lambda off: lambda h: c_ref[0, :, pl.ds(off + h * HEAD_DIM, HEAD_DIM)].astype(BF16)
    new_rows = lambda x: lambda h: x[:, h * HEAD_DIM:(h + 1) * HEAD_DIM].astype(BF16)
    segments = [(buf_rows(0), buf_rows(hw), on_stride(dist_buf)),
                (new_rows(knew_p), new_rows(vnew_p), on_stride(dist_new) & (ncol < Sd))]
    o, l = _attend_heads(q_ref[...], segments, A_HEADS)
    o_ref[...] = o
    l_ref[...] = l


def a_step(proj, g, window, dil, cache, Tp, Bd, Sd):
    assert dil & (dil - 1) == 0
    W = cache.shape[1]
    hw = A_HEADS * HEAD_DIM
    r0 = Tp // Sd
    col = lambda c: pl.BlockSpec((Sd, hw), lambda b: (r0 + b, g * 3 + c))
    row_out = pl.BlockSpec((Sd, hw), lambda b: (b, 0))
    return pl.pallas_call(
        functools.partial(_a_step_kernel, win=window, dil=dil),
        out_shape=[jax.ShapeDtypeStruct((Bd * Sd, hw), F32), jax.ShapeDtypeStruct((Bd * Sd, hw), F32),
                   jax.ShapeDtypeStruct(cache.shape, F32)],
        grid=(Bd,),
        in_specs=[col(0), col(1), col(2), pl.BlockSpec((1, W, 2 * hw), lambda b: (b, 0, 0))],
        out_specs=[row_out, row_out, pl.BlockSpec((1, W, 2 * hw), lambda b: (b, 0, 0))],
        compiler_params=_cparams(("parallel",)),
        name=f"a_step_d{dil}",
    )(proj, proj, proj, cache)


def _topk_cols(s, k, payload=None):
    N = s.shape[0]
    iota = lax.broadcasted_iota(jnp.int32, s.shape, 0).astype(F32)
    vals, picks = [], []
    for _ in range(k):
        m = jnp.max(s, axis=0, keepdims=True)
        idx = jnp.min(jnp.where(s == m, iota, float(N)), axis=0, keepdims=True)
        sel = iota == idx
        vals.append(m)
        if payload is None:
            picks.append(idx)
        else:
            picks.append(jnp.max(jnp.where(sel, payload, -1.0), axis=0, keepdims=True))
        s = jnp.where(sel, -jnp.inf, s)
    return vals, picks


def _topk_cols_many(arrs, k):
    N = arrs[0].shape[0]
    iota = lax.broadcasted_iota(jnp.int32, arrs[0].shape, 0).astype(F32)
    arrs = list(arrs)
    out = [([], []) for _ in arrs]
    for _ in range(k):
        for n, s in enumerate(arrs):
            m = jnp.max(s, axis=0, keepdims=True)
            idx = jnp.min(jnp.where(s == m, iota, float(N)), axis=0, keepdims=True)
            out[n][0].append(m)
            out[n][1].append(idx)
            arrs[n] = jnp.where(iota == idx, -jnp.inf, s)
    return out


def _topk_mask_cols(s, k):
    N = s.shape[0]
    iota = lax.broadcasted_iota(jnp.int32, s.shape, 0).astype(F32)
    chosen = jnp.zeros(s.shape, F32)
    for _ in range(k):
        m = jnp.max(s, axis=0, keepdims=True)
        idx = jnp.min(jnp.where(s == m, iota, float(N)), axis=0, keepdims=True)
        sel = iota == idx
        chosen = jnp.where(sel, 1.0, chosen)
        s = jnp.where(sel, -jnp.inf, s)
    return chosen


def _peer_route_kernel(x_ref, g_ref, wqh_ref, wql_ref, skh_ref, skl_ref,
                       hn_ref, e_ref, w_ref, xh_ref, xl_ref, et_ref, wt_ref, *, heads_per_step):
    h = pl.program_id(1)

    @pl.when(h == 0)
    def _():
        xn = _rms_norm_rows(x_ref[...], g_ref[...])
        hi, lo = _split2(xn)
        xh_ref[...] = hi
        xl_ref[...] = lo
        hn_ref[...] = hi

    xh, xl = xh_ref[...], xl_ref[...]
    K = PEER_TOPK
    b_iota = lax.broadcasted_iota(jnp.int32, (8, LANES), 0)
    jobs = []
    for hh in range(heads_per_step):
        cs = pl.ds(hh * PEER_QDIM, PEER_QDIM)
        wqh, wql = wqh_ref[:, cs], wql_ref[:, cs]
        q = _dot(xh, wqh) + _dot(xh, wql) + _dot(xl, wqh)
        sts = []
        for z in range(2):
            qh, ql = _split2(q[:, z * PEER_KEYS:(z + 1) * PEER_KEYS])
            skh, skl = skh_ref[z], skl_ref[z]
            sts.append(_dot_nt(skh, qh) + _dot_nt(skl, qh) + _dot_nt(skh, ql))
        row = pl.multiple_of((h * heads_per_step + hh) * K, K)
        jobs += [(sts, row, c0) for c0 in range(0, xh.shape[0], LANES)]
    for sts, row, c0 in jobs:
        (v1, i1), (v2, i2) = _topk_cols_many([st[:, c0:c0 + LANES] for st in sts], K)
        v2_lo, v2_hi = jnp.concatenate(v2[:8], axis=0), jnp.concatenate(v2[8:], axis=0)
        i2_lo, i2_hi = jnp.concatenate(i2[:8], axis=0), jnp.concatenate(i2[8:], axis=0)
        e_of = lambda a, i2rows: i1[a] * float(PEER_KEYS) + i2rows
        cand = [v1[0] + v2_lo, v1[0] + v2_hi, v1[1] + v2_lo]
        cand_e = [e_of(0, i2_lo), e_of(0, i2_hi), e_of(1, i2_lo)]
        for a in range(2, 8):
            cand.append(jnp.where(b_iota < K // (a + 1), v1[a] + v2_lo, -jnp.inf))
            cand_e.append(e_of(a, i2_lo))
        cand.append(jnp.concatenate(v1[8:], axis=0) + v2[0])
        cand_e.append(jnp.concatenate(i1[8:], axis=0) * float(PEER_KEYS) + i2[0])
        best, best_e = _topk_cols(jnp.concatenate(cand, axis=0), K, payload=jnp.concatenate(cand_e, axis=0))
        bs = jnp.concatenate(best, axis=0)
        ex = jnp.exp(bs - best[0])
        et_ref[pl.ds(row, K), pl.ds(c0, LANES)] = jnp.concatenate(best_e, axis=0)
        wt_ref[pl.ds(row, K), pl.ds(c0, LANES)] = ex / jnp.sum(ex, axis=0, keepdims=True)

    @pl.when(h == pl.num_programs(1) - 1)
    def _():
        e_ref[...] = et_ref[...].T.astype(jnp.int32)
        w_ref[...] = wt_ref[...].T


def peer_route(x, g, wq_hi, wq_lo, sk_hi, sk_lo, *, tm, heads_per_step=8):
    T, D = x.shape
    assert T % tm == 0 and tm % LANES == 0
    nslot = PEER_HEADS * PEER_TOPK
    wq_spec = pl.BlockSpec((D, heads_per_step * PEER_QDIM), lambda i, h: (0, h))
    return pl.pallas_call(
        functools.partial(_peer_route_kernel, heads_per_step=heads_per_step),
        out_shape=[jax.ShapeDtypeStruct((T, D), BF16),
                   jax.ShapeDtypeStruct((T, nslot), jnp.int32),
                   jax.ShapeDtypeStruct((T, nslot), F32)],
        grid=(T // tm, PEER_HEADS // heads_per_step),
        in_specs=[pl.BlockSpec((tm, D), lambda i, h: (i, 0)),
                  pl.BlockSpec((1, D), lambda i, h: (0, 0)),
                  wq_spec, wq_spec,
                  pl.BlockSpec((2, PEER_KEYS, PEER_QDIM // 2), lambda i, h: (0, 0, 0)),
                  pl.BlockSpec((2, PEER_KEYS, PEER_QDIM // 2), lambda i, h: (0, 0, 0))],
        out_specs=[pl.BlockSpec((tm, D), lambda i, h: (i, 0)),
                   pl.BlockSpec((tm, nslot), lambda i, h: (i, 0)),
                   pl.BlockSpec((tm, nslot), lambda i, h: (i, 0))],
        scratch_shapes=[pltpu.VMEM((tm, D), BF16), pltpu.VMEM((tm, D), BF16),
                        pltpu.VMEM((nslot, tm), F32), pltpu.VMEM((nslot, tm), F32)],
        compiler_params=_cparams(("parallel", "arbitrary")),
        name="peer_route",
    )(x, g.reshape(1, D), wq_hi, wq_lo, sk_hi, sk_lo)


def _peer_expert_kernel(hn_ref, e_ref, w_ref, x_ref, ua_ref, ub_ref, va_ref, vb_ref, o_ref, gs_ref, acc_ref,
                        *, tm, rows_per_chunk):
    c = pl.program_id(1)
    nk = PEER_KEYS
    half = nk // 2

    @pl.when(c == 0)
    def _():
        acc_ref[...] = jnp.zeros_like(acc_ref)
        sub = lax.broadcasted_iota(jnp.int32, (nk, nk), 0)

        def body(t8, carry):
            r0 = pl.multiple_of(t8 * 8, 8)
            e8 = e_ref[pl.ds(r0, 8), :]
            w8 = w_ref[pl.ds(r0, 8), :]
            for r in range(8):
                e, w = e8[r:r + 1], w8[r:r + 1]
                i1 = jnp.right_shift(e, 7)
                i2 = jnp.bitwise_and(e, nk - 1)
                pt = jnp.where(sub == i1, w, 0.0).astype(BF16)
                qt = jnp.where(sub == i2, 1.0, 0.0).astype(BF16)
                row = pl.multiple_of((r0 + r) * G_PITCH, 8)
                gs_ref[pl.ds(row, nk), :] = _dot_nt(pt, qt)
            return carry

        lax.fori_loop(0, tm // 8, body, 0)

    hn = hn_ref[...]
    act_a = jax.nn.gelu(_dot_nt(hn, ua_ref[...]))
    act_b = jax.nn.gelu(_dot_nt(hn, ub_ref[...]))
    pa, pb = [], []
    for r in range(rows_per_chunk):
        g_lo = gs_ref[pl.ds(c * rows_per_chunk + r, tm, stride=G_PITCH), :]
        g_hi = gs_ref[pl.ds(half + c * rows_per_chunk + r, tm, stride=G_PITCH), :]
        pa.append((act_a[:, r * nk:(r + 1) * nk] * g_lo).astype(BF16))
        pb.append((act_b[:, r * nk:(r + 1) * nk] * g_hi).astype(BF16))
    acc_ref[...] += (_dot(jnp.concatenate(pa, axis=1), va_ref[...])
                     + _dot(jnp.concatenate(pb, axis=1), vb_ref[...]))

    @pl.when(c == pl.num_programs(1) - 1)
    def _():
        o_ref[...] = x_ref[...] + acc_ref[...]


def peer_experts(hn, e, w, x, u, v, *, tm, chunk):
    T, D = x.shape
    n_exp = v.shape[0]
    nslot = e.shape[1]
    blk = chunk // 2
    rows_per_chunk = blk // PEER_KEYS
    n_chunks = n_exp // chunk
    lo = pl.BlockSpec((blk, D), lambda i, c: (c, 0))
    hi = pl.BlockSpec((blk, D), lambda i, c: (c + n_chunks, 0))
    return pl.pallas_call(
        functools.partial(_peer_expert_kernel, tm=tm, rows_per_chunk=rows_per_chunk),
        out_shape=jax.ShapeDtypeStruct((T, D), F32),
        grid=(T // tm, n_chunks),
        in_specs=[pl.BlockSpec((tm, D), lambda i, c: (i, 0)),
                  pl.BlockSpec((tm, nslot), lambda i, c: (i, 0)),
                  pl.BlockSpec((tm, nslot), lambda i, c: (i, 0)),
                  pl.BlockSpec((tm, D), lambda i, c: (i, 0)),
                  lo, hi, lo, hi],
        out_specs=pl.BlockSpec((tm, D), lambda i, c: (i, 0)),
        scratch_shapes=[pltpu.VMEM((tm * G_PITCH, PEER_KEYS), F32), pltpu.VMEM((tm, D), F32)],
        compiler_params=_cparams(("parallel", "arbitrary")),
        name="peer_experts",
    )(hn, e, w, x, u, u, v, v)


def peer_layer(x, g, w_q, subkeys, u, v, *, tm_route, tm_exp, chunk):
    wq_hi, wq_lo = _split2(w_q)
    sk_hi, sk_lo = _split2(subkeys)
    hn, e, w = peer_route(x, g, wq_hi, wq_lo, sk_hi, sk_lo, tm=tm_route)
    return peer_experts(hn, e, w, x, u.astype(BF16), v.astype(BF16), tm=tm_exp, chunk=chunk)


def _compress_core(get_x, w1k_ref, w1v_ref, pe_ref, w1f_ref, w2k_ref, w2v_ref, cos_ref, sa_ref, sb_ref,
                   kc_ref, vc_ref, acc_ref, n_cmp):
    nch = acc_ref.shape[0]
    for l in range(CMP_STRIDE):
        x = get_x(l)
        yk = _dot(x[:, :KV_W].astype(BF16), w1k_ref[l])
        yv = _dot(x[:, KV_W:].astype(BF16), w1v_ref[l])
        if l == 0:
            acc_ref[:, pl.ds(0, 2 * KV_W)] = yk
            acc_ref[:, pl.ds(2 * KV_W, 2 * KV_W)] = yv
        else:
            acc_ref[:, pl.ds(0, 2 * KV_W)] += yk
            acc_ref[:, pl.ds(2 * KV_W, 2 * KV_W)] += yv
    keep = lax.broadcasted_iota(jnp.int32, (nch, KV_W), 0) < n_cmp
    for z, (w2_ref, out_ref) in enumerate(((w2k_ref, kc_ref), (w2v_ref, vc_ref))):
        ph, pl_ = _split2(pe_ref[z])
        wh, wl = _split2(w1f_ref[z])
        pe_t = _dot(ph, wh) + _dot(ph, wl) + _dot(pl_, wh)
        pe_t = jnp.concatenate([pe_t] * B_KV_HEADS, axis=1)
        lo = acc_ref[:, pl.ds(z * 2 * KV_W, KV_W)]
        hi = acc_ref[:, pl.ds(z * 2 * KV_W + KV_W, KV_W)]
        hid = jax.nn.gelu(lo + pltpu.roll(hi, nch - 1, 0) + pe_t)
        out = _dot(hid.astype(BF16), w2_ref[...])
        if z == 0:
            out = _rope_lanes(out, cos_ref[...], sa_ref[...], sb_ref[...])
        out_ref[0] = jnp.where(keep, out, 0.0)


def _compress_prompt_kernel(x0, x1, x2, x3, *rest, n_cmp):
    slabs = (x0, x1, x2, x3)
    nch = rest[-1].shape[0]

    def get_x(l):
        return jnp.concatenate([s[pl.ds(l, nch, stride=CMP_STRIDE), :] for s in slabs], axis=1)

    _compress_core(get_x, *rest, n_cmp=n_cmp)


def _compress_weights(w_cmp1, w_cmp2, pe_cmp, n_chunk):
    eye = jnp.eye(B_KV_HEADS, dtype=F32)

    def per_head(w):
        bd = jnp.einsum('hg,...df->...hdgf', eye, w)
        return bd.reshape(w.shape[:-2] + (KV_W, KV_W)).astype(BF16)

    w1 = per_head(w_cmp1.reshape(2, 2, CMP_STRIDE, HEAD_DIM, -1))
    w1 = jnp.concatenate([w1[:, 0], w1[:, 1]], axis=-1)
    w2 = per_head(w_cmp2)
    pe = pe_cmp.reshape(2, 1, CMP_LEN * HEAD_DIM)
    w1f = w_cmp1.reshape(2, CMP_LEN * HEAD_DIM, -1)
    end_pos = jnp.arange(n_chunk, dtype=jnp.int32) * CMP_STRIDE + (CMP_LEN - 1)
    tabs = _rope_tables(end_pos, KV_W)
    return (w1[0], w1[1], pe, w1f, w2[0], w2[1]) + tabs


def _const_specs(arrs):
    return [pl.BlockSpec(a.shape, (lambda *i, nd=a.ndim: (0,) * nd), pipeline_mode=pl.Buffered(1)) for a in arrs]


def compress_prompt(kv, cw, B, S):
    nch = S // CMP_STRIDE
    slab = lambda s: pl.BlockSpec((S, LANES), lambda b: (b, s), pipeline_mode=pl.Buffered(1))
    out = pl.BlockSpec((1, nch, KV_W), lambda b: (b, 0, 0))
    return pl.pallas_call(
        functools.partial(_compress_prompt_kernel, n_cmp=nch - 1),
        out_shape=[jax.ShapeDtypeStruct((B, nch, KV_W), F32)] * 2,
        grid=(B,),
        in_specs=[slab(s) for s in range(4)] + _const_specs(cw),
        out_specs=[out, out],
        scratch_shapes=[pltpu.VMEM((nch, 4 * KV_W), F32)],
        compiler_params=_cparams(("parallel",)),
        name="compress_prompt",
    )(kv, kv, kv, kv, *cw)


def _compress_sample_kernel(pt_ref, *refs, n_cmp, pages_per_step):
    n_pg = 4 * pages_per_step
    pages, rest, x_ref = refs[:n_pg], refs[n_pg:-1], refs[-1]
    j = pl.program_id(1)
    ch_per_page = pages[0].shape[1] // CMP_STRIDE
    for p in range(pages_per_step):
        r0 = pl.multiple_of((j * pages_per_step + p) * ch_per_page, ch_per_page)
        for s in range(4):
            for l in range(CMP_STRIDE):
                x_ref[l, pl.ds(r0, ch_per_page), pl.ds(s * LANES, LANES)] = (
                    pages[p * 4 + s][0, pl.ds(l, ch_per_page, stride=CMP_STRIDE), :])

    @pl.when(j == pl.num_programs(1) - 1)
    def _():
        _compress_core(lambda l: x_ref[l], *rest, n_cmp=n_cmp)


def compress_sample(cache, page_table, cw, *, pages_per_step):
    Bd, n_pages = page_table.shape
    page = cache.shape[1]
    nch = n_pages * page // CMP_STRIDE
    P = pages_per_step

    def pspec(p, s):
        return pl.BlockSpec((1, page, LANES), lambda b, j, pt: (pt[b, j * P + p], 0, s))

    out = pl.BlockSpec((1, nch, KV_W), lambda b, j, pt: (b, 0, 0))
    return pl.pallas_call(
        functools.partial(_compress_sample_kernel, n_cmp=nch - 1, pages_per_step=P),
        out_shape=[jax.ShapeDtypeStruct((Bd, nch, KV_W), F32)] * 2,
        grid_spec=pltpu.PrefetchScalarGridSpec(
            num_scalar_prefetch=1,
            grid=(Bd, n_pages // P),
            in_specs=[pspec(p, s) for p in range(P) for s in range(4)] + _const_specs(cw),
            out_specs=[out, out],
            scratch_shapes=[pltpu.VMEM((nch, 4 * KV_W), F32),
                            pltpu.VMEM((CMP_STRIDE, nch, 2 * KV_W), F32)],
        ),
        compiler_params=_cparams(("parallel", "arbitrary")),
        name="compress_sample",
    )(page_table, *([cache] * (4 * P)), *cw)


def _select_blocks(imp, wsel, blk, cur):
    i_hi = imp.astype(BF16)
    r1 = imp - i_hi.astype(F32)
    i_mid = r1.astype(BF16)
    i_lo = (r1 - i_mid.astype(F32)).astype(BF16)
    imp_sel = _dot(i_hi, wsel) + _dot(i_mid, wsel) + _dot(i_lo, wsel)
    forced = (blk == 0) | (blk == cur) | (blk == cur - 1)
    imp_sel = jnp.where(blk > cur, -jnp.inf, jnp.where(forced, jnp.inf, imp_sel))
    return _topk_mask_cols(imp_sel.T, N_SEL).T


def _tile_block_hits(chosen_q, k0, kt_size):
    n_sb = chosen_q.shape[1]
    blk = lax.broadcasted_iota(jnp.int32, (n_sb, kt_size), 0)
    key_blk = jnp.right_shift(k0 + lax.broadcasted_iota(jnp.int32, (n_sb, kt_size), 1), 6)
    expand = jnp.where(blk == key_blk, 1.0, 0.0).astype(BF16)
    return _dot(chosen_q, expand)


def _tile_block_mask(chosen_q, k0, kt_size):
    return _tile_block_hits(chosen_q, k0, kt_size) > 0.5


def _nsa_prompt_kernel(q_ref, gate_ref, kc_ref, vc_ref, ks_ref, vs_ref, kw_ref, vw_ref, wsel_ref,
                       o_ref, *, n_cmp, kt_size):
    n = pl.program_id(1)
    t0 = n * Q_BLOCK
    G, TQ = B_GROUP, Q_BLOCK
    R = G * TQ
    NC = kc_ref.shape[1]
    NSB = wsel_ref.shape[1]
    wsel = wsel_ref[...]
    gt = gate_ref[...]
    row_c = lax.broadcasted_iota(jnp.int32, (R, NC), 0)
    col_c = lax.broadcasted_iota(jnp.int32, (R, NC), 1)
    valid_c = (col_c * CMP_STRIDE + (CMP_LEN - 1) <= t0 + jnp.bitwise_and(row_c, TQ - 1)) & (col_c < n_cmp)
    bias_c = jnp.where(valid_c, 0.0, NEG)
    sees_cmp = t0 + jnp.bitwise_and(lax.broadcasted_iota(jnp.int32, (R, 1), 0), TQ - 1) >= CMP_LEN - 1
    blk = lax.broadcasted_iota(jnp.int32, (TQ, NSB), 1)
    cur = jnp.right_shift(t0 + lax.broadcasted_iota(jnp.int32, (TQ, NSB), 0), 6)
    WK = B_WINDOW + TQ
    w0 = pl.multiple_of(jnp.maximum(t0 - B_WINDOW, 0), TQ)
    wpos = w0 + lax.broadcasted_iota(jnp.int32, (R, WK), 1)
    rel = t0 + jnp.bitwise_and(lax.broadcasted_iota(jnp.int32, (R, WK), 0), TQ - 1) - wpos
    bias_w = jnp.where((rel >= 0) & (rel <= B_WINDOW), 0.0, NEG)
    n_kt = (t0 + TQ + kt_size - 1) // kt_size
    k_last = pl.multiple_of((n_kt - 1) * kt_size, kt_size)
    causal_last = jnp.where(k_last + lax.broadcasted_iota(jnp.int32, (TQ, kt_size), 1)
                            <= t0 + lax.broadcasted_iota(jnp.int32, (TQ, kt_size), 0), 0.0, NEG)
    ones_col = jnp.where(lax.broadcasted_iota(jnp.int32, (kt_size, HEAD_DIM), 1) == 0, 1.0, 0.0).astype(BF16)

    head_cols = [pl.ds(h * HEAD_DIM, HEAD_DIM) for h in range(B_KV_HEADS)]

    qbs, o_cs, chosens = [], [], []
    for h, hs in enumerate(head_cols):
        qs = jnp.concatenate([q_ref[:, pl.ds((h * G + g) * HEAD_DIM, HEAD_DIM)] for g in range(G)], axis=0)
        qs = qs * (SCALE * LOG2E)
        s_c = _dot_nt3(qs, kc_ref[0, :, hs]) + bias_c
        e_c = jnp.exp2(s_c - jnp.max(s_c, axis=-1, keepdims=True))
        p_c = e_c * jnp.where(sees_cmp, 1.0 / jnp.sum(e_c, axis=-1, keepdims=True), 0.0)
        imp = p_c[0:TQ]
        for g in range(1, G):
            imp = imp + p_c[g * TQ:(g + 1) * TQ]
        qbs.append(qs.astype(BF16))
        o_cs.append(_dot(p_c.astype(BF16), vc_ref[0, :, hs].astype(BF16)))
        chosens.append(_select_blocks(imp, wsel, blk, cur).astype(BF16))

    def sel_tile(h, k0, carry, extra_bias):
        m_i, acc = carry
        hs = head_cols[h]
        bias = (_tile_block_hits(chosens[h], k0, kt_size) - 1.0) * (-NEG)
        if extra_bias is not None:
            bias = bias + extra_bias
        s = _dot_nt(qbs[h], ks_ref[pl.ds(k0, kt_size), hs]).reshape(G, TQ, kt_size) + bias[None]
        s = s.reshape(R, kt_size)
        m_new = jnp.maximum(m_i, jnp.max(s, axis=-1, keepdims=True))
        p = jnp.exp2(s - m_new).astype(BF16)
        v1 = jnp.concatenate([vs_ref[pl.ds(k0, kt_size), hs], ones_col], axis=1)
        return m_new, jnp.exp2(m_i - m_new) * acc + _dot(p, v1)

    def sel_body(kt, carries):
        k0 = pl.multiple_of(kt * kt_size, kt_size)
        return tuple(sel_tile(h, k0, carries[h], None) for h in range(B_KV_HEADS))

    init = tuple((jnp.full((R, 1), NEG, F32), jnp.zeros((R, 2 * HEAD_DIM), F32)) for _ in range(B_KV_HEADS))
    carries = lax.fori_loop(0, n_kt - 1, sel_body, init)

    outs = []
    for h, hs in enumerate(head_cols):
        _, acc_s = sel_tile(h, k_last, carries[h], causal_last)
        o_s = acc_s[:, :HEAD_DIM] / acc_s[:, HEAD_DIM:HEAD_DIM + 1]
        s_w = _dot_nt(qbs[h], kw_ref[pl.ds(w0, WK), hs]) + bias_w
        e_w = jnp.exp2(s_w - jnp.max(s_w, axis=-1, keepdims=True))
        p_w = e_w * (1.0 / jnp.sum(e_w, axis=-1, keepdims=True))
        o_w = _dot(p_w.astype(BF16), vw_ref[pl.ds(w0, WK), hs])
        o_c = o_cs[h]
        for g in range(G):
            rs = slice(g * TQ, (g + 1) * TQ)
            c = 3 * (h * G + g)
            outs.append(gt[:, c:c + 1] * o_c[rs] + gt[:, c + 1:c + 2] * o_s[rs] + gt[:, c + 2:c + 3] * o_w[rs])
    o_ref[...] = jnp.concatenate(outs, axis=1)


def nsa_prompt(q, gate, kc, vc, kvb, wsel, B, S, *, kt_size):
    nq = q.shape[1]
    nqb = S // Q_BLOCK
    NC = kc.shape[1]
    comp = lambda c: pl.BlockSpec((S, KV_W), lambda b, n: (b, c))
    cmp_spec = pl.BlockSpec((1, NC, KV_W), lambda b, n: (b, 0, 0))
    return pl.pallas_call(
        functools.partial(_nsa_prompt_kernel, n_cmp=S // CMP_STRIDE - 1, kt_size=kt_size),
        out_shape=jax.ShapeDtypeStruct((B * S, nq), F32),
        grid=(B, nqb),
        in_specs=[pl.BlockSpec((Q_BLOCK, nq), lambda b, n: (b * nqb + n, 0)),
                  pl.BlockSpec((Q_BLOCK, LANES), lambda b, n: (b * nqb + n, 0)),
                  cmp_spec, cmp_spec, comp(2), comp(3), comp(4), comp(5),
                  pl.BlockSpec(wsel.shape, lambda b, n: (0, 0))],
        out_specs=pl.BlockSpec((Q_BLOCK, nq), lambda b, n: (b * nqb + n, 0)),
        compiler_params=_cparams(("parallel", "arbitrary")),
        name="nsa_prompt",
    )(q, gate, kc, vc, kvb, kvb, kvb, kvb, wsel)


def _nsa_step_kernel(pt_ref, q_ref, gate_ref, kc_ref, vc_ref, new_ref, win_ref, wsel_ref, *rest,
                     past, n_cmp, pages_per_step):
    P = pages_per_step
    pages = rest[:P]
    o_ref, st_ref, a_ref, ch_ref, m_ref, l_ref, acc_ref, oc_ref = rest[P:]
    del pt_ref
    j = pl.program_id(1)
    G, KVH = B_GROUP, B_KV_HEADS
    Sd = q_ref.shape[0]
    R = B_HEADS * Sd
    page = pages[0].shape[1]
    kt = P * page
    row_head = lambda shape: lax.broadcasted_iota(jnp.int32, shape, 0) // (G * Sd)
    own = lambda width: row_head((R, width)) == lax.broadcasted_iota(jnp.int32, (R, width), 1) // HEAD_DIM

    def diag(x):
        rows = G * Sd
        return jnp.concatenate([x[h * rows:(h + 1) * rows, h * HEAD_DIM:(h + 1) * HEAD_DIM]
                                for h in range(KVH)], axis=0)

    @pl.when(j == 0)
    def _():
        q = q_ref[...]
        a = jnp.concatenate([q[:, hd * HEAD_DIM:(hd + 1) * HEAD_DIM] for hd in range(B_HEADS)], axis=0)
        a = jnp.where(own(KV_W), jnp.concatenate([a] * KVH, axis=1), 0.0)
        a_ref[...] = a.astype(BF16)
        NC = kc_ref.shape[1]
        valid_c = lax.broadcasted_iota(jnp.int32, (R, NC), 1) < n_cmp
        p_c = _softmax_rows(_dot_nt3(a, kc_ref[0]) * SCALE, valid_c)
        oc_ref[...] = diag(_dot(p_c.astype(BF16), vc_ref[0].astype(BF16)))
        pooled = jnp.sum(p_c.reshape(KVH, G, Sd, NC), axis=1, keepdims=True)
        imp = jnp.broadcast_to(pooled, (KVH, G, Sd, NC)).reshape(R, NC)
        NSB = wsel_ref.shape[1]
        blk = lax.broadcasted_iota(jnp.int32, (R, NSB), 1)
        srow = jnp.bitwise_and(lax.broadcasted_iota(jnp.int32, (R, NSB), 0), Sd - 1)
        cur = jnp.right_shift(past + srow, 6)
        ch_ref[...] = _select_blocks(imp, wsel_ref[...], blk, cur).astype(BF16)
        m_ref[...] = jnp.full(m_ref.shape, NEG, F32)
        l_ref[...] = jnp.zeros(l_ref.shape, F32)
        acc_ref[...] = jnp.zeros(acc_ref.shape, F32)

    a = a_ref[...]
    ks = jnp.concatenate([pg[0, :, pl.ds(0, KV_W)] for pg in pages], axis=0).astype(BF16)
    vs = jnp.concatenate([pg[0, :, pl.ds(KV_W, KV_W)] for pg in pages], axis=0).astype(BF16)
    mk = _tile_block_mask(ch_ref[...], j * kt, kt)
    m_n, l_n, acc_n = _online_update((m_ref[...], l_ref[...], acc_ref[...]), _dot_nt(a, ks) * SCALE, mk, vs)
    m_ref[...] = m_n
    l_ref[...] = l_n
    acc_ref[...] = acc_n

    @pl.when(j == pl.num_programs(1) - 1)
    def _():
        new = _pad_rows(new_ref[...], LANES)
        comp = lambda c: new[:, c * KV_W:(c + 1) * KV_W].astype(BF16)
        srow = jnp.bitwise_and(lax.broadcasted_iota(jnp.int32, (R, LANES), 0), Sd - 1)
        scol = lax.broadcasted_iota(jnp.int32, (R, LANES), 1)
        causal_new = (scol <= srow) & (scol < Sd)
        nblk = past // SEL_BLOCK
        in_new = ch_ref[:, nblk:nblk + 1].astype(F32) > 0.5
        _, l_s, acc_s = _online_update((m_n, l_n, acc_n), _dot_nt(a, comp(2)) * SCALE, in_new & causal_new, comp(3))
        o_s = diag(acc_s) / jnp.where(l_s > 0, l_s, 1.0)

        W = win_ref.shape[1]
        wrow = jnp.bitwise_and(lax.broadcasted_iota(jnp.int32, (R, W), 0), Sd - 1)
        rel = W + wrow - lax.broadcasted_iota(jnp.int32, (R, W), 1)
        carry = _online_update(_online_init(R, KV_W), _dot_nt(a, win_ref[0, :, pl.ds(0, KV_W)].astype(BF16)) * SCALE,
                               (rel >= 0) & (rel <= B_WINDOW), win_ref[0, :, pl.ds(KV_W, KV_W)].astype(BF16))
        _, l_w, acc_w = _online_update(carry, _dot_nt(a, comp(4)) * SCALE, causal_new, comp(5))
        o_w = diag(acc_w) / jnp.where(l_w > 0, l_w, 1.0)
        o_c = oc_ref[...]
        gt = gate_ref[...]
        outs = []
        for hd in range(B_HEADS):
            rs = slice(hd * Sd, (hd + 1) * Sd)
            c = 3 * hd
            outs.append(gt[:, c:c + 1] * o_c[rs] + gt[:, c + 1:c + 2] * o_s[rs] + gt[:, c + 2:c + 3] * o_w[rs])
        o_ref[...] = jnp.concatenate(outs, axis=1)
        st_ref[0, pl.ds(0, W - Sd), :] = win_ref[0, pl.ds(Sd, W - Sd), :]
        st_ref[0, pl.ds(W - Sd, Sd), :] = new_ref[:, pl.ds(4 * KV_W, 2 * KV_W)]


def nsa_step(q, gate, kc, vc, kv, win, cache, page_table, wsel, Tp, Bd, Sd, *, pages_per_step):
    nq = q.shape[1]
    n_pages = page_table.shape[1]
    page = cache.shape[1]
    past = n_pages * page
    W = win.shape[1]
    NC = kc.shape[1]
    P = pages_per_step
    r0 = Tp // Sd
    R = B_HEADS * Sd
    rows = lambda w: pl.BlockSpec((Sd, w), lambda b, j, pt: (r0 + b, 0))
    per_b = lambda s1, s2: pl.BlockSpec((1, s1, s2), lambda b, j, pt: (b, 0, 0))
    pspec = lambda p: pl.BlockSpec((1, page, 2 * KV_W), lambda b, j, pt: (pt[b, j * P + p], 0, 1))
    return pl.pallas_call(
        functools.partial(_nsa_step_kernel, past=past, n_cmp=past // CMP_STRIDE - 1, pages_per_step=P),
        out_shape=[jax.ShapeDtypeStruct((Bd * Sd, nq), F32), jax.ShapeDtypeStruct(win.shape, F32)],
        grid_spec=pltpu.PrefetchScalarGridSpec(
            num_scalar_prefetch=1,
            grid=(Bd, n_pages // P),
            in_specs=[rows(nq), rows(LANES), per_b(NC, KV_W), per_b(NC, KV_W), rows(kv.shape[1]),
                      per_b(W, 2 * KV_W), pl.BlockSpec(wsel.shape, lambda b, j, pt: (0, 0))]
                     + [pspec(p) for p in range(P)],
            out_specs=[pl.BlockSpec((Sd, nq), lambda b, j, pt: (b, 0)), per_b(W, 2 * KV_W)],
            scratch_shapes=[pltpu.VMEM((R, KV_W), BF16), pltpu.VMEM((R, wsel.shape[1]), BF16),
                            pltpu.VMEM((R, 1), F32), pltpu.VMEM((R, 1), F32), pltpu.VMEM((R, KV_W), F32),
                            pltpu.VMEM((R, HEAD_DIM), F32)],
        ),
        compiler_params=_cparams(("parallel", "arbitrary")),
        name="nsa_step",
    )(page_table, q, gate, kc, vc, kv, win, wsel, *([cache] * P))


def _rope_tables(pos, width):
    half = ROT_DIM // 2
    inv = jnp.exp(-jnp.log(jnp.float32(ROPE_THETA)) * jnp.arange(half, dtype=F32) * (2.0 / ROT_DIM))
    ang = pos.astype(F32)[:, None] * inv[None, :]
    cos, sin = jnp.cos(ang), jnp.sin(ang)
    T = pos.shape[0]
    z8 = jnp.zeros((T, half), F32)
    rest = HEAD_DIM - ROT_DIM
    c = jnp.concatenate([cos, cos, jnp.ones((T, rest), F32)], axis=1)
    sa = jnp.concatenate([-sin, z8, jnp.zeros((T, rest), F32)], axis=1)
    sb = jnp.concatenate([z8, sin, jnp.zeros((T, rest), F32)], axis=1)
    reps = width // HEAD_DIM
    return tuple(jnp.tile(t, (1, reps)) for t in (c, sa, sb))


def _overlap_weights(n_cmp, n_cmp_pad, n_sb, n_sb_pad):
    j = jnp.arange(n_sb)
    cidx = (SEL_BLOCK // CMP_STRIDE) * j[:, None] - 1 + jnp.arange(len(SEL_OVERLAP_W))[None, :]
    wts = jnp.where((cidx >= 0) & (cidx < n_cmp), jnp.asarray(SEL_OVERLAP_W, F32), 0.0)
    onehot = (jnp.clip(cidx, 0, n_cmp - 1)[:, :, None] == jnp.arange(n_cmp_pad)[None, None, :]).astype(F32)
    wsel = jnp.einsum('jo,jon->nj', wts, onehot)
    return jnp.pad(wsel, ((0, 0), (0, n_sb_pad - n_sb))).astype(BF16)


def _pick(n, cands):
    for c in cands:
        if n % c == 0:
            return c
    return n


def _pad_to(n, m):
    return -(-n // m) * m


def kernel(x_prompt, x_sample, cache_a_g1, cache_a_g2, cache_a_g3, cache_b_kv, cache_b_win, page_table, g_mix, g_ffn, w_in_a, w_o_a, g_kv, w_kv_b, w_cmp1, w_cmp2, pe_cmp, w_qg_b, b_gate_b, w_o_b, w_peer_q, peer_subkeys, peer_u, peer_v, g_final):
    B, S, D = x_prompt.shape
    Bd, Sd, _ = x_sample.shape
    Tp, Ts = B * S, Bd * Sd
    T = Tp + Ts
    n_phys, page = cache_b_kv.shape[:2]
    n_pages = page_table.shape[1]
    past = n_pages * page
    hw = A_HEADS * HEAD_DIM
    tm_big = _pick(T, (1280, 640, 256, 128, 64, 32, 16, 8))
    tm_mid = _pick(T, (640, 256, 128, 64, 32, 16, 8))
    tm_peer = _pick(T, (256, 128))
    tm_exp = _pick(T, (320, 256, 128, 64, 32, 16, 8))
    tm_rows = _pick(math.gcd(Tp, Ts), (256, 128, 64, 32, 16, 8))
    pages_per_step = _pick(n_pages, (16, 8, 4, 2, 1))

    x = jnp.concatenate([x_prompt.reshape(Tp, D), x_sample.reshape(Ts, D)], axis=0)
    pos_p = jnp.arange(S, dtype=jnp.int32)
    pos_s = past + jnp.arange(Sd, dtype=jnp.int32)
    pos = jnp.concatenate([jnp.tile(pos_p, B), jnp.tile(pos_s, Bd)])
    tab512 = _rope_tables(pos, 512)
    tab256 = tuple(t[:, :KV_W] for t in tab512)

    n_grp = len(A_GROUPS)
    proj = rms_matmul(x, g_mix[0], w_in_a[0].astype(BF16), tm=tm_big, tn=hw, epilogue="rope",
                      flags=jnp.asarray([1, 1, 0] * n_grp, jnp.int32), tables=tab512)
    caches_a = (cache_a_g1, cache_a_g2, cache_a_g3)
    op, lp, os_, ls_, a_prompt, a_sample = [], [], [], [], [], []
    for g, (win, dil) in enumerate(A_GROUPS):
        o_g, l_g = band_attention(proj, g, win, dil, B, S)
        cache = caches_a[g]
        o_s, l_s, state = a_step(proj, g, win, dil, cache[0].reshape(Bd, cache.shape[2], 2 * hw), Tp, Bd, Sd)
        a_sample.append(state.reshape(cache.shape))
        keep = min(win, S)
        c0 = g * 3 * hw + hw
        kv_cols = jnp.stack([proj[(b + 1) * S - keep:(b + 1) * S, c0:c0 + 2 * hw] for b in range(B)])
        a_prompt.append(kv_cols.reshape(1, B, keep, 2, A_HEADS, HEAD_DIM))
        op.append(o_g)
        lp.append(l_g)
        os_.append(o_s)
        ls_.append(l_s)
    x = merge_matmul_residual(op + lp, os_ + ls_, w_o_a[0].astype(BF16), x, tm=tm_rows)
    x = peer_layer(x, g_ffn[0], w_peer_q[0], peer_subkeys[0], peer_u[0], peer_v[0],
                   tm_route=tm_peer, tm_exp=tm_exp, chunk=2048)

    kv, kvb = rms_matmul(x, g_kv, w_kv_b.astype(BF16), tm=tm_big, tn=KV_W, epilogue="rope",
                         flags=jnp.asarray([0, 0, 1, 0, 1, 0], jnp.int32), tables=tab256, emit_bf16=True)
    bkv_prompt = kv[:Tp, :4 * KV_W].reshape(B, S, 4, B_KV_HEADS, HEAD_DIM)
    bkv_sample = kv[Tp:, :4 * KV_W].reshape(Bd, Sd, 4, B_KV_HEADS, HEAD_DIM)
    keep_w = min(B_WINDOW, S)
    bwin_prompt = jnp.stack([kv[(b + 1) * S - keep_w:(b + 1) * S, 4 * KV_W:] for b in range(B)])
    bwin_prompt = bwin_prompt.reshape(B, keep_w, 2, B_KV_HEADS, HEAD_DIM)
    cache2d = cache_b_kv.reshape(n_phys, page, 4 * KV_W)
    cw_p = _compress_weights(w_cmp1, w_cmp2, pe_cmp, S // CMP_STRIDE)
    cw_s = cw_p if past == S else _compress_weights(w_cmp1, w_cmp2, pe_cmp, past // CMP_STRIDE)
    kc_p, vc_p = compress_prompt(kv, cw_p, B, S)
    kc_s, vc_s = compress_sample(cache2d, page_table, cw_s, pages_per_step=pages_per_step)

    nq = B_HEADS * HEAD_DIM
    w_qg = w_qg_b[0]
    q = rms_matmul(x, g_mix[1], w_qg[:, :nq].astype(BF16), tm=tm_big, tn=512, epilogue="rope",
                   flags=jnp.ones((nq // 512,), jnp.int32), tables=tab512)
    ngate = 3 * B_HEADS
    w_gate = jnp.pad(w_qg[:, nq:], ((0, 0), (0, LANES - ngate))).astype(BF16)
    b_gate = jnp.pad(b_gate_b[0], (0, LANES - ngate))
    gate = rms_matmul(x, g_mix[1], w_gate, tm=tm_big, tn=LANES, epilogue="sigmoid", bias=b_gate)

    n_cmp_p = S // CMP_STRIDE - 1
    wsel_p = _overlap_weights(n_cmp_p, S // CMP_STRIDE, S // SEL_BLOCK, _pad_to(S // SEL_BLOCK, LANES))
    o_b_p = nsa_prompt(q, gate, kc_p, vc_p, kvb, wsel_p, B, S, kt_size=min(1024, S))
    n_cmp_s = past // CMP_STRIDE - 1
    n_sb_s = -(-(past + Sd) // SEL_BLOCK)
    wsel_s = _overlap_weights(n_cmp_s, past // CMP_STRIDE, n_sb_s, _pad_to(n_sb_s, LANES))
    Wb = cache_b_win.shape[1]
    o_b_s, bwin_sample = nsa_step(q, gate, kc_s, vc_s, kv, cache_b_win.reshape(Bd, Wb, 2 * KV_W), cache2d,
                                  page_table, wsel_s, Tp, Bd, Sd, pages_per_step=pages_per_step)
    x = matmul_residual(o_b_p, o_b_s, w_o_b[0].astype(BF16), x, tm=tm_rows)
    x = peer_layer(x, g_ffn[1], w_peer_q[1], peer_subkeys[1], peer_u[1], peer_v[1],
                   tm_route=tm_peer, tm_exp=tm_exp, chunk=2048)
    y = rmsnorm_rows(x, g_final, tm=tm_mid)

    return (y[:Tp].reshape(B, S, D), y[Tp:].reshape(Bd, Sd, D),
            a_prompt[0], a_sample[0], a_prompt[1], a_sample[1], a_prompt[2], a_sample[2],
            bkv_prompt, bkv_sample, bwin_prompt, bwin_sample.reshape(cache_b_win.shape))
```
